```python
import math, functools
import jax, jax.numpy as jnp
from jax import lax
import numpy as np

D_MODEL = 1024
BATCH = 1
SEQ = 16384
DEPTH = 2
DEC_BATCH = 8
DEC_SEQ = 32
PAST_LEN = 1024

CHUNK = 64
QBLK = 128
H_A = 8
HD_A = 64
D_A = H_A * 2 * HD_A
H_M = 4
DH_M = 256
D_M = H_M * DH_M
N_BUCKETS = 32
REL_MAX_DIST = 128
D_FF = 2816
CONV_W = 3
NEG_INF = -1e30
IN_COLS = 3 * D_A + 4 * D_M + 2 * D_MODEL + 2 * H_M
IN_SPLITS = (D_A, 2 * D_A, 3 * D_A, 3 * D_A + D_M, 3 * D_A + 2 * D_M, 3 * D_A + 3 * D_M,
             3 * D_A + 4 * D_M, 3 * D_A + 4 * D_M + 2 * D_MODEL)

kernel_name = "hybrid_diffattn_mlstm_convffn_stream_step"


def rmsnorm(x, g, eps=1e-6):
    xf = x.astype(jnp.float32)
    y = xf * lax.rsqrt(jnp.mean(xf * xf, axis=-1, keepdims=True) + eps)
    return (y * g.astype(jnp.float32)).astype(x.dtype)


def rel_bucket(rel):
    half = N_BUCKETS // 2
    max_exact = half // 2
    base = jnp.where(rel > 0, half, 0)
    n = jnp.abs(rel)
    nf = jnp.maximum(n, 1).astype(jnp.float32)
    large = max_exact + (jnp.log(nf / max_exact) / math.log(REL_MAX_DIST / max_exact)
                         * (half - max_exact)).astype(jnp.int32)
    large = jnp.minimum(large, half - 1)
    return base + jnp.where(n < max_exact, n, large)


def diff_attn(q, k, v, qpos, kpos, rel_bias, lam):
    bias = jnp.transpose(rel_bias[rel_bucket(kpos[None, :] - qpos[:, None])], (2, 0, 1))
    bias = bias.astype(jnp.float32)
    mask = (kpos[None, :] // CHUNK) <= (qpos[:, None] // CHUNK)
    s = jnp.einsum('bqhcd,bkhcd->bchqk', q, k).astype(jnp.float32) * (HD_A ** -0.5) + bias
    s = jnp.where(mask, s, NEG_INF)
    p = jax.nn.softmax(s, axis=-1)
    a = p[:, 0] - lam * p[:, 1]
    return jnp.einsum('bhqk,bkhe->bqhe', a, v.astype(jnp.float32))


def attend_prompt(q, k, v, rel_bias, lam):
    B, S = q.shape[0], q.shape[1]
    nb = S // QBLK
    qb = q.reshape(B, nb, QBLK, H_A, 2, HD_A).swapaxes(0, 1)
    kpos = jnp.arange(S, dtype=jnp.int32)

    def one(args):
        qblk, i = args
        qpos = i * QBLK + jnp.arange(QBLK, dtype=jnp.int32)
        return diff_attn(qblk, k, v, qpos, kpos, rel_bias, lam)

    o = lax.map(one, (qb, jnp.arange(nb, dtype=jnp.int32)))
    return o.swapaxes(0, 1).reshape(B, S, H_A, 2 * HD_A)


def attend_sample(q, k, v, rel_bias, lam, cache_k, cache_v):
    B, T = q.shape[0], q.shape[1]
    P = cache_k.shape[1]
    kk = jnp.concatenate([cache_k.astype(k.dtype), k.reshape(B, T, H_A, 2 * HD_A)], axis=1)
    kk = kk.reshape(B, P + T, H_A, 2, HD_A)
    vv = jnp.concatenate([cache_v.astype(v.dtype), v], axis=1)
    qpos = P + jnp.arange(T, dtype=jnp.int32)
    kpos = jnp.arange(P + T, dtype=jnp.int32)
    return diff_attn(q, kk, vv, qpos, kpos, rel_bias, lam)


def mlstm_chunk(carry, inp):
    C, n, m = carry
    q, k, v, ig, lf = inp
    L = q.shape[1]
    b = jnp.cumsum(lf, axis=1)
    tri = jnp.tril(jnp.ones((L, L), dtype=bool))[None, :, :, None]
    dmat = jnp.where(tri, b[:, :, None, :] - b[:, None, :, :] + ig[:, None, :, :], -jnp.inf)
    inter = b + m[:, None, :]
    m_t = jnp.maximum(inter, jnp.max(dmat, axis=2))
    w_intra = jnp.exp(dmat - m_t[:, :, None, :])
    w_inter = jnp.exp(inter - m_t)
    A = w_intra * jnp.einsum('bthd,bshd->btsh', q, k)
    num = w_inter[..., None] * jnp.einsum('bthd,bhde->bthe', q, C) + jnp.einsum('btsh,bshe->bthe', A, v)
    den = w_inter * jnp.einsum('bthd,bhd->bth', q, n) + jnp.sum(A, axis=2)
    h = num / jnp.maximum(jnp.abs(den), jnp.exp(-m_t))[..., None]
    bL = b[:, -1]
    g = bL[:, None, :] - b + ig
    m_new = jnp.maximum(bL + m, jnp.max(g, axis=1))
    decay = jnp.exp(bL + m - m_new)
    ws = jnp.exp(g - m_new[:, None, :])
    C_new = decay[:, :, None, None] * C + jnp.einsum('bsh,bshd,bshe->bhde', ws, k, v)
    n_new = decay[..., None] * n + jnp.einsum('bsh,bshd->bhd', ws, k)
    return (C_new, n_new, m_new), h


def mlstm_scan(q, k, v, ig, lf, C, n, m):
    B, T = q.shape[0], q.shape[1]
    L = min(CHUNK, T)
    nc = T // L

    def to_chunks(a):
        return a.astype(jnp.float32).reshape((B, nc, L) + a.shape[2:]).swapaxes(0, 1)

    xs = (to_chunks(q), to_chunks(k), to_chunks(v), to_chunks(ig), to_chunks(lf))
    (C, n, m), h = lax.scan(mlstm_chunk, (C, n, m), xs)
    h = h.swapaxes(0, 1).reshape(B, T, H_M, DH_M)
    return h, C, n, m


def layer(x, c, l, lp, rel_bias, attend, mstate, conv_buf):
    (ada_w, ada_b, n1, n2, w_in, b_if, lam_p, sub_g, m_g,
     w_br_a, w_br_m, w_out, w_up, conv_w, conv_b, w_down) = lp
    B, T, _ = x.shape
    lam_init = 0.8 - 0.6 * math.exp(-0.3 * l)
    mod = (jax.nn.silu(c) @ ada_w + ada_b)[:, None, :]
    sh1, sc1, g1, sh2, sc2, g2 = jnp.split(mod, 6, axis=-1)

    xn = rmsnorm(x, n1) * (1 + sc1) + sh1
    z = xn @ w_in
    qa, ka, va, qm, km, vm, om, gts, ifg = jnp.split(z, IN_SPLITS, axis=-1)

    lp32 = lam_p.astype(jnp.float32)
    lam = jnp.exp(jnp.sum(lp32[0] * lp32[1])) - jnp.exp(jnp.sum(lp32[2] * lp32[3])) + lam_init
    qa = qa.reshape(B, T, H_A, 2, HD_A)
    ka4 = ka.reshape(B, T, H_A, 2, HD_A)
    va = va.reshape(B, T, H_A, 2 * HD_A)
    o_a = attend(qa, ka4, va, rel_bias, lam).astype(x.dtype)
    o_a = rmsnorm(o_a, sub_g) * (1 - lam_init)
    y_a = o_a.reshape(B, T, D_A) @ w_br_a

    qm = qm.reshape(B, T, H_M, DH_M)
    km = km.reshape(B, T, H_M, DH_M) * (DH_M ** -0.5)
    vm = vm.reshape(B, T, H_M, DH_M)
    pre = ifg.astype(jnp.float32) + b_if.astype(jnp.float32)
    ig = pre[..., :H_M]
    lf = jax.nn.log_sigmoid(pre[..., H_M:])
    h, C, n, m = mlstm_scan(qm, km, vm, ig, lf, *mstate)
    h = rmsnorm(h.astype(x.dtype), m_g.reshape(H_M, DH_M)) * jax.nn.sigmoid(om).reshape(B, T, H_M, DH_M)
    y_m = h.reshape(B, T, D_M) @ w_br_m

    ga, gm = jnp.split(jax.nn.sigmoid(gts), 2, axis=-1)
    x = x + g1 * ((ga * y_a + gm * y_m) @ w_out)

    xn2 = rmsnorm(x, n2) * (1 + sc2) + sh2
    u, gate = jnp.split(xn2 @ w_up, 2, axis=-1)
    ext = jnp.concatenate([conv_buf.astype(u.dtype), u], axis=1)
    conv = conv_b
    for j in range(CONV_W):
        conv = conv + conv_w[j] * ext[:, j:j + T]
    f = (jax.nn.gelu(conv) * gate) @ w_down
    x = x + g2 * f
    new_buf = ext[:, T:]
    state = (ka.reshape(B, T, H_A, 2 * HD_A), va,
             C.astype(x.dtype), n.astype(x.dtype), m.astype(x.dtype), new_buf)
    return x, state


def setup_inputs(seed: int = 0) -> dict:
    key = jax.random.key(seed)
    ks = jax.random.split(key, 32)
    f32 = jnp.float32

    def nrm(k, shape, scale):
        return jax.random.normal(k, shape, f32) * scale

    b_i = nrm(ks[10], (DEPTH, H_M), 0.1)
    b_f = jnp.linspace(3.0, 6.0, H_M, dtype=f32)[None, :] + nrm(ks[11], (DEPTH, H_M), 0.1)
    return {
        "x_prompt": nrm(ks[0], (BATCH, SEQ, D_MODEL), 1.0),
        "x_sample": nrm(ks[1], (DEC_BATCH, DEC_SEQ, D_MODEL), 1.0),
        "c_prompt": nrm(ks[2], (BATCH, D_MODEL), 1.0),
        "c_sample": nrm(ks[3], (DEC_BATCH, D_MODEL), 1.0),
        "cache_k": nrm(ks[4], (DEPTH, DEC_BATCH, PAST_LEN, H_A, 2 * HD_A), 1.0),
        "cache_v": nrm(ks[5], (DEPTH, DEC_BATCH, PAST_LEN, H_A, 2 * HD_A), 1.0),
        "state_C": nrm(ks[6], (DEPTH, DEC_BATCH, H_M, DH_M, DH_M), 0.1),
        "state_n": nrm(ks[7], (DEPTH, DEC_BATCH, H_M, DH_M), 0.1),
        "state_m": nrm(ks[8], (DEPTH, DEC_BATCH, H_M), 1.0),
        "state_conv": nrm(ks[9], (DEPTH, DEC_BATCH, CONV_W - 1, D_FF), 1.0),
        "rel_bias": nrm(ks[12], (N_BUCKETS, H_A), 0.5),
        "ada_w": nrm(ks[13], (DEPTH, D_MODEL, 6 * D_MODEL), 0.5 * D_MODEL ** -0.5),
        "ada_b": nrm(ks[14], (DEPTH, 6 * D_MODEL), 0.02),
        "norm1_g": 1.0 + nrm(ks[15], (DEPTH, D_MODEL), 0.05),
        "norm2_g": 1.0 + nrm(ks[16], (DEPTH, D_MODEL), 0.05),
        "w_in": nrm(ks[17], (DEPTH, D_MODEL, IN_COLS), D_MODEL ** -0.5),
        "b_if": jnp.concatenate([b_i, b_f], axis=-1),
        "lam_p": nrm(ks[18], (DEPTH, 4, HD_A), 0.1),
        "attn_sub_g": 1.0 + nrm(ks[19], (DEPTH, 2 * HD_A), 0.05),
        "mlstm_g": 1.0 + nrm(ks[20], (DEPTH, D_M), 0.05),
        "w_br_a": nrm(ks[21], (DEPTH, D_A, D_MODEL), D_A ** -0.5),
        "w_br_m": nrm(ks[22], (DEPTH, D_M, D_MODEL), D_M ** -0.5),
        "w_out": nrm(ks[23], (DEPTH, D_MODEL, D_MODEL), D_MODEL ** -0.5),
        "w_up": nrm(ks[24], (DEPTH, D_MODEL, 2 * D_FF), D_MODEL ** -0.5),
        "conv_w": nrm(ks[25], (DEPTH, CONV_W, D_FF), CONV_W ** -0.5),
        "conv_b": nrm(ks[26], (DEPTH, D_FF), 0.02),
        "w_down": nrm(ks[27], (DEPTH, D_FF, D_MODEL), D_FF ** -0.5),
        "final_g": 1.0 + nrm(ks[28], (D_MODEL,), 0.05),
    }


def reference(x_prompt, x_sample, c_prompt, c_sample, cache_k, cache_v, state_C, state_n, state_m,
              state_conv, rel_bias, ada_w, ada_b, norm1_g, norm2_g, w_in, b_if, lam_p, attn_sub_g,
              mlstm_g, w_br_a, w_br_m, w_out, w_up, conv_w, conv_b, w_down, final_g):
    def lp(l):
        return (ada_w[l], ada_b[l], norm1_g[l], norm2_g[l], w_in[l], b_if[l], lam_p[l], attn_sub_g[l],
                mlstm_g[l], w_br_a[l], w_br_m[l], w_out[l], w_up[l], conv_w[l], conv_b[l], w_down[l])

    x = x_prompt
    B = x.shape[0]
    st_p = []
    for l in range(DEPTH):
        m0 = (jnp.zeros((B, H_M, DH_M, DH_M), jnp.float32), jnp.zeros((B, H_M, DH_M), jnp.float32),
              jnp.zeros((B, H_M), jnp.float32))
        buf0 = jnp.zeros((B, CONV_W - 1, D_FF), x.dtype)
        x, st = layer(x, c_prompt, l, lp(l), rel_bias, attend_prompt, m0, buf0)
        st_p.append(st)
    y_prompt = rmsnorm(x, final_g)

    x = x_sample
    st_s = []
    for l in range(DEPTH):
        attend = functools.partial(attend_sample, cache_k=cache_k[l], cache_v=cache_v[l])
        ms = (state_C[l].astype(jnp.float32), state_n[l].astype(jnp.float32), state_m[l].astype(jnp.float32))
        x, st = layer(x, c_sample, l, lp(l), rel_bias, attend, ms, state_conv[l])
        st_s.append(st)
    y_sample = rmsnorm(x, final_g)

    k_prompt = jnp.stack([s[0] for s in st_p])
    v_prompt = jnp.stack([s[1] for s in st_p])
    C_prompt = jnp.stack([s[2] for s in st_p])
    n_prompt = jnp.stack([s[3] for s in st_p])
    m_prompt = jnp.stack([s[4] for s in st_p])
    conv_prompt = jnp.stack([s[5] for s in st_p])
    k_sample = jnp.stack([s[0] for s in st_s])
    v_sample = jnp.stack([s[1] for s in st_s])
    C_sample = jnp.stack([s[2] for s in st_s])
    n_sample = jnp.stack([s[3] for s in st_s])
    m_sample = jnp.stack([s[4] for s in st_s])
    conv_sample = jnp.stack([s[5] for s in st_s])
    return (y_prompt, y_sample, k_prompt, v_prompt, C_prompt, n_prompt, m_prompt, conv_prompt,
            k_sample, v_sample, C_sample, n_sample, m_sample, conv_sample)
```

```python
import functools
import math

import numpy as np
import jax
import jax.numpy as jnp
from jax import lax
from jax.experimental import pallas as pl
from jax.experimental.pallas import tpu as pltpu

F32 = jnp.float32
BF16 = jnp.bfloat16

CHUNK = 64
H_A = 8
HD_A = 64
HE_A = 2 * HD_A
H_M = 4
DH_M = 256
N_BUCKETS = 32
CONV_W = 3
NEG_INF = -1e30
EPS = 1e-6
FAR_BUCKET = N_BUCKETS // 2 - 1

V7X_VMEM_LIMIT = 56 * 1024 * 1024
ROW_TILE = 512
ATTN_TILE = 512
MLSTM_TILE = 256

NT_DIMS = (((1,), (1,)), ((), ()))


def _cparams(sem):
    return pltpu.CompilerParams(dimension_semantics=sem, vmem_limit_bytes=V7X_VMEM_LIMIT)


def _dot(a, b):
    return jnp.dot(a, b, preferred_element_type=F32)


def _rms(x, g):
    return x * lax.rsqrt(jnp.mean(x * x, axis=-1, keepdims=True) + EPS) * g


def _log_sigmoid(x):
    return jnp.minimum(x, 0.0) - jnp.log1p(jnp.exp(-jnp.abs(x)))


def _mod_spec(per_row, tm, d, col, row_index):
    if per_row:
        return pl.BlockSpec((tm, d), lambda *g: (row_index(*g), col))
    return pl.BlockSpec((1, d), lambda *g: (0, col))


def _ada_kernel(c_ref, w_ref, b_ref, o_ref):
    a = jax.nn.silu(c_ref[...]).astype(BF16)
    o_ref[...] = _dot(a, w_ref[...].astype(BF16)) + b_ref[...]


def _ada(c_all, ada_w, ada_b):
    depth, d, n = ada_w.shape
    rc = c_all.shape[0]
    tn = 1536
    return pl.pallas_call(
        _ada_kernel,
        grid=(depth, n // tn),
        in_specs=[pl.BlockSpec((rc, d), lambda l, j: (0, 0)),
                  pl.BlockSpec((None, d, tn), lambda l, j: (l, 0, j)),
                  pl.BlockSpec((None, 1, tn), lambda l, j: (l, 0, j))],
        out_specs=pl.BlockSpec((None, rc, tn), lambda l, j: (l, 0, j)),
        out_shape=jax.ShapeDtypeStruct((depth, rc, n), F32),
        compiler_params=_cparams(("arbitrary", "arbitrary")),
        name="ada_mod",
    )(c_all, ada_w, ada_b.reshape(depth, 1, n))


def _in_attn_kernel(x_ref, g_ref, sc_ref, sh_ref, w_ref, q_ref, k32_ref, kbf_ref, v32_ref, vbf_ref,
                    *, d, transposed):
    xn = (_rms(x_ref[...], g_ref[...]) * (1.0 + sc_ref[...]) + sh_ref[...]).astype(BF16)
    q = _dot(xn, w_ref[:, 0:d]) * (HD_A ** -0.5)
    k = _dot(xn, w_ref[:, d:2 * d])
    v = _dot(xn, w_ref[:, 2 * d:3 * d])
    k32_ref[...] = k
    kbf_ref[...] = k.astype(BF16)
    v32_ref[...] = v
    if transposed:
        q_ref[...] = q.T.astype(BF16)
        vbf_ref[...] = v.T.astype(BF16)
    else:
        q_ref[...] = q.astype(BF16)
        vbf_ref[...] = v.astype(BF16)


def _in_attn(x, mod, per_row, g, w_qkv, transposed):
    r, d = x.shape
    tm = min(r, ROW_TILE)
    row = lambda i: i
    nat = pl.BlockSpec((tm, d), lambda i: (i, 0))
    tr = pl.BlockSpec((d, tm), lambda i: (0, i))
    nat_shape = lambda dt: jax.ShapeDtypeStruct((r, d), dt)
    tr_shape = jax.ShapeDtypeStruct((d, r), BF16)
    return pl.pallas_call(
        functools.partial(_in_attn_kernel, d=d, transposed=transposed),
        grid=(r // tm,),
        in_specs=[nat, pl.BlockSpec((1, d), lambda i: (0, 0)),
                  _mod_spec(per_row, tm, d, 1, row), _mod_spec(per_row, tm, d, 0, row),
                  pl.BlockSpec((d, 3 * d), lambda i: (0, 0))],
        out_specs=[tr if transposed else nat, nat, nat, nat, tr if transposed else nat],
        out_shape=[tr_shape if transposed else nat_shape(BF16), nat_shape(F32), nat_shape(BF16),
                   nat_shape(F32), tr_shape if transposed else nat_shape(BF16)],
        compiler_params=_cparams(("arbitrary",)),
        name="in_attn",
    )(x, g, mod, mod, w_qkv)


def _in_mlstm_kernel(x_ref, g_ref, sc_ref, sh_ref, w_ref, wif_ref, bif_ref,
                     qm_ref, km_ref, vm_ref, om_ref, gts_ref, gate_ref, *, d):
    xn = (_rms(x_ref[...], g_ref[...]) * (1.0 + sc_ref[...]) + sh_ref[...]).astype(BF16)
    qm_ref[...] = _dot(xn, w_ref[:, 0:d]).astype(BF16)
    km_ref[...] = (_dot(xn, w_ref[:, d:2 * d]) * (DH_M ** -0.5)).astype(BF16)
    vm_ref[...] = _dot(xn, w_ref[:, 2 * d:3 * d]).astype(BF16)
    om_ref[...] = _dot(xn, w_ref[:, 3 * d:4 * d])
    gts_ref[...] = _dot(xn, w_ref[:, 4 * d:6 * d])
    pre = _dot(xn, wif_ref[...]) + bif_ref[...]
    gate_ref[...] = pre[:, 0:2 * H_M]


def _in_mlstm(x, mod, per_row, g, w_m, w_if, b_if):
    r, d = x.shape
    tm = min(r, ROW_TILE // 2)
    row = lambda i: i
    nat = pl.BlockSpec((tm, d), lambda i: (i, 0))
    const = lambda shape: pl.BlockSpec(shape, lambda i: (0, 0))
    return pl.pallas_call(
        functools.partial(_in_mlstm_kernel, d=d),
        grid=(r // tm,),
        in_specs=[nat, const((1, d)), _mod_spec(per_row, tm, d, 1, row), _mod_spec(per_row, tm, d, 0, row),
                  const(w_m.shape), const(w_if.shape), const(b_if.shape)],
        out_specs=[nat, nat, nat, nat, pl.BlockSpec((tm, 2 * d), lambda i: (i, 0)),
                   pl.BlockSpec((tm, 2 * H_M), lambda i: (i, 0))],
        out_shape=[jax.ShapeDtypeStruct((r, d), BF16)] * 3
        + [jax.ShapeDtypeStruct((r, d), F32), jax.ShapeDtypeStruct((r, 2 * d), F32),
           jax.ShapeDtypeStruct((r, 2 * H_M), F32)],
        compiler_params=_cparams(("arbitrary",)),
        name="in_mlstm",
    )(x, g, mod, mod, w_m, w_if, b_if)


def _bucket_np(rel):
    half = N_BUCKETS // 2
    max_exact = half // 2
    n = np.abs(rel)
    thresholds = np.ceil(max_exact * 2.0 ** (np.arange(1, half - max_exact) / 2.0)).astype(np.int64)
    large = max_exact + (n[..., None] >= thresholds).sum(-1)
    return np.where(rel > 0, half, 0) + np.where(n < max_exact, n, np.minimum(large, half - 1))


def _lam(lp):
    s1 = jnp.sum(lp[0:1] * lp[1:2], axis=-1, keepdims=True)
    s2 = jnp.sum(lp[2:3] * lp[3:4], axis=-1, keepdims=True)
    return jnp.exp(s1) - jnp.exp(s2)


def _attn_prompt_kernel(lamp_ref, qT_ref, k_ref, vT_ref, bias_ref, subg_ref, o_ref,
                        qa_s, qb_s, m_s, l_s, acc_s, *, t, lam_init):
    i = pl.program_id(1)
    qT = qT_ref[...]
    rowi = lax.broadcasted_iota(jnp.int32, qT.shape, 0)
    zero = jnp.zeros_like(qT)
    qa_s[...] = jnp.where(rowi < HD_A, qT, zero)
    qb_s[...] = jnp.where(rowi >= HD_A, qT, zero)
    m_s[...] = jnp.full(m_s.shape, NEG_INF, F32)
    l_s[...] = jnp.zeros(l_s.shape, F32)
    acc_s[...] = jnp.zeros(acc_s.shape, F32)

    def step(j, bias):
        off = pl.multiple_of(j * t, t)
        kb = k_ref[pl.ds(off, t), :]
        vb = vT_ref[:, pl.ds(off, t)]
        for c, q_s in enumerate((qa_s, qb_s)):
            s = _dot(kb, q_s[...])
            if bias is not None:
                s = s + bias
            m_old = m_s[c]
            m_new = jnp.maximum(m_old, jnp.max(s, axis=0, keepdims=True))
            e = jnp.exp(s - m_new)
            alpha = jnp.exp(m_old - m_new)
            l_s[c] = alpha * l_s[c] + jnp.sum(e, axis=0, keepdims=True)
            acc_s[c] = alpha * acc_s[c] + _dot(vb, e.astype(BF16))
            m_s[c] = m_new

    def far(j, carry):
        step(j, None)
        return carry

    lax.fori_loop(0, jnp.maximum(i - 1, 0), far, 0)

    @pl.when(i >= 1)
    def _():
        step(i - 1, bias_ref[0])

    step(i, bias_ref[1])

    lam = _lam(lamp_ref[...]) + lam_init
    oT = acc_s[0] / l_s[0] - lam * (acc_s[1] / l_s[1])
    o = _rms(oT.T, subg_ref[...]) * (1.0 - lam_init)
    o_ref[...] = o.astype(BF16)


def _attn_prompt(qT, kbf, vT, rel_bias, lam_p, sub_g, lam_init, t):
    d, s = qT.shape
    assert s % t == 0 and t % CHUNK == 0 and t > 90
    nq = s // t
    kk = np.arange(t)[:, None]
    qq = np.arange(t)[None, :]
    idx = np.stack([_bucket_np(kk - qq - t), _bucket_np(kk - qq)])
    visible = np.stack([np.ones((t, t), bool), (kk // CHUNK) <= (qq // CHUNK)])
    rb = rel_bias.astype(F32)
    tiles = jnp.transpose(rb[idx], (3, 0, 1, 2)) - rb[FAR_BUCKET][:, None, None, None]
    tiles = jnp.where(visible[None], tiles, NEG_INF)
    return pl.pallas_call(
        functools.partial(_attn_prompt_kernel, t=t, lam_init=lam_init),
        grid=(H_A, nq),
        in_specs=[pl.BlockSpec((4, HD_A), lambda h, i: (0, 0)),
                  pl.BlockSpec((HE_A, t), lambda h, i: (h, i)),
                  pl.BlockSpec((s, HE_A), lambda h, i: (0, h)),
                  pl.BlockSpec((HE_A, s), lambda h, i: (h, 0)),
                  pl.BlockSpec((None, 2, t, t), lambda h, i: (h, 0, 0, 0)),
                  pl.BlockSpec((1, HE_A), lambda h, i: (0, 0))],
        out_specs=pl.BlockSpec((t, HE_A), lambda h, i: (i, h)),
        out_shape=jax.ShapeDtypeStruct((s, d), BF16),
        scratch_shapes=[pltpu.VMEM((HE_A, t), BF16), pltpu.VMEM((HE_A, t), BF16),
                        pltpu.VMEM((2, 1, t), F32), pltpu.VMEM((2, 1, t), F32),
                        pltpu.VMEM((2, HE_A, t), F32)],
        compiler_params=_cparams(("arbitrary", "arbitrary")),
        name="attn_prompt",
    )(lam_p, qT, kbf, vT, tiles, sub_g)


def _attn_sample_kernel(lamp_ref, q_ref, kc_ref, vc_ref, kn_ref, vn_ref, bc_ref, bn_ref, subg_ref, o_ref,
                        *, lam_init):
    q = q_ref[...]
    lane = lax.broadcasted_iota(jnp.int32, q.shape, 1)
    zero = jnp.zeros_like(q)
    kc = kc_ref[...].astype(BF16)
    kn = kn_ref[...]
    probs = []
    for qc in (jnp.where(lane < HD_A, q, zero), jnp.where(lane >= HD_A, q, zero)):
        sc = lax.dot_general(qc, kc, NT_DIMS, preferred_element_type=F32) + bc_ref[...]
        sn = lax.dot_general(qc, kn, NT_DIMS, preferred_element_type=F32) + bn_ref[...]
        m = jnp.maximum(jnp.max(sc, axis=-1, keepdims=True), jnp.max(sn, axis=-1, keepdims=True))
        ec = jnp.exp(sc - m)
        en = jnp.exp(sn - m)
        l = jnp.sum(ec, axis=-1, keepdims=True) + jnp.sum(en, axis=-1, keepdims=True)
        probs.append((ec / l, en / l))
    lam = _lam(lamp_ref[...]) + lam_init
    ac = probs[0][0] - lam * probs[1][0]
    an = probs[0][1] - lam * probs[1][1]
    o = _dot(ac.astype(BF16), vc_ref[...].astype(BF16)) + _dot(an.astype(BF16), vn_ref[...])
    o_ref[...] = (_rms(o, subg_ref[...]) * (1.0 - lam_init)).astype(BF16)


def _attn_sample(q, kbf, vbf, cache_k, cache_v, layer, past, rel_bias, lam_p, sub_g, lam_init, b, t):
    r, d = q.shape
    qpos = past + np.arange(t)[:, None]
    kpos = np.arange(past + t)[None, :]
    idx = _bucket_np(kpos - qpos)
    visible = (kpos // CHUNK) <= (qpos // CHUNK)
    bias = jnp.where(visible[None], jnp.transpose(rel_bias.astype(F32)[idx], (2, 0, 1)), NEG_INF)
    nb = layer * b
    return pl.pallas_call(
        functools.partial(_attn_sample_kernel, lam_init=lam_init),
        grid=(b, H_A),
        in_specs=[pl.BlockSpec((4, HD_A), lambda i, h: (0, 0)),
                  pl.BlockSpec((t, HE_A), lambda i, h: (i, h)),
                  pl.BlockSpec((past, HE_A), lambda i, h: (nb + i, h)),
                  pl.BlockSpec((past, HE_A), lambda i, h: (nb + i, h)),
                  pl.BlockSpec((t, HE_A), lambda i, h: (i, h)),
                  pl.BlockSpec((t, HE_A), lambda i, h: (i, h)),
                  pl.BlockSpec((None, t, past), lambda i, h: (h, 0, 0)),
                  pl.BlockSpec((None, t, t), lambda i, h: (h, 0, 0)),
                  pl.BlockSpec((1, HE_A), lambda i, h: (0, 0))],
        out_specs=pl.BlockSpec((t, HE_A), lambda i, h: (i, h)),
        out_shape=jax.ShapeDtypeStruct((r, d), BF16),
        compiler_params=_cparams(("arbitrary", "arbitrary")),
        name="attn_sample",
    )(lam_p, q, cache_k, cache_v, kbf, vbf, bias[:, :, :past], bias[:, :, past:], sub_g)


def _mlstm_kernel(q_ref, k_ref, v_ref, om_ref, g_ref, gT_ref, mg_ref, c0_ref, n0_ref, m0_ref,
                  h_ref, c_ref, n_ref, m_ref, c_s, n_s, m_s, *, tl):
    step = pl.program_id(1)

    @pl.when(step == 0)
    def _():
        c_s[...] = c0_ref[...]
        n_s[...] = n0_ref[...]
        m_s[...] = m0_ref[...]

    ti = lax.broadcasted_iota(jnp.int32, (tl, tl), 0)
    si = lax.broadcasted_iota(jnp.int32, (tl, tl), 1)
    causal = si <= ti
    for h in range(H_M):
        sl = slice(h * DH_M, (h + 1) * DH_M)
        q = q_ref[:, sl]
        k = k_ref[:, sl]
        v = v_ref[:, sl]
        ig_c = g_ref[:, h:h + 1]
        lf_c = _log_sigmoid(g_ref[:, H_M + h:H_M + h + 1])
        ig_r = gT_ref[h:h + 1, :]
        lf_r = _log_sigmoid(gT_ref[H_M + h:H_M + h + 1, :])
        m = m_s[h]
        n_row = n_s[h]
        b_c = jnp.sum(jnp.where(causal, lf_r, 0.0), axis=1, keepdims=True)
        b_r = jnp.sum(jnp.where(ti <= si, lf_c, 0.0), axis=0, keepdims=True)
        dmat = jnp.where(causal, b_c - b_r + ig_r, -jnp.inf)
        inter = b_c + m
        m_t = jnp.maximum(inter, jnp.max(dmat, axis=1, keepdims=True))
        w_intra = jnp.exp(dmat - m_t)
        w_inter = jnp.exp(inter - m_t)
        a = w_intra * lax.dot_general(q, k, NT_DIMS, preferred_element_type=F32)
        num = w_inter * _dot(q, c_s[h].astype(BF16)) + _dot(a.astype(BF16), v)
        den = (w_inter * jnp.sum(q.astype(F32) * n_row, axis=1, keepdims=True)
               + jnp.sum(a, axis=1, keepdims=True))
        hh = num / jnp.maximum(jnp.abs(den), jnp.exp(-m_t))
        b_last = b_r[:, tl - 1:tl]
        g_c = b_last - b_c + ig_c
        m_new = jnp.maximum(b_last + m, jnp.max(g_c, axis=0, keepdims=True))
        decay = jnp.exp(b_last + m - m_new)
        kw = jnp.exp(g_c - m_new) * k.astype(F32)
        c_s[h] = decay * c_s[h] + _dot(kw.T.astype(BF16), v)
        n_s[h] = decay * n_row + jnp.sum(kw, axis=0, keepdims=True)
        m_s[h] = m_new
        hn = _rms(hh, mg_ref[:, sl])
        h_ref[:, sl] = (hn * jax.nn.sigmoid(om_ref[:, sl])).astype(BF16)

    @pl.when(step == pl.num_programs(1) - 1)
    def _():
        c_ref[...] = c_s[...]
        n_ref[...] = n_s[...]
        m_ref[...] = m_s[...]


def _mlstm(qm, km, vm, om, gates, m_g, c0, n0, m0, b, t, tl):
    r, d = qm.shape
    nc = t // tl
    gates_t = jnp.transpose(gates.reshape(b * nc, tl, 2 * H_M), (0, 2, 1))
    rows = pl.BlockSpec((tl, d), lambda i, c: (i * nc + c, 0))
    st_c = pl.BlockSpec((None, H_M, DH_M, DH_M), lambda i, c: (i, 0, 0, 0))
    st_n = pl.BlockSpec((None, H_M, 1, DH_M), lambda i, c: (i, 0, 0, 0))
    st_m = pl.BlockSpec((None, H_M, 1, 1), lambda i, c: (i, 0, 0, 0))
    h, c_new, n_new, m_new = pl.pallas_call(
        functools.partial(_mlstm_kernel, tl=tl),
        grid=(b, nc),
        in_specs=[rows, rows, rows, rows,
                  pl.BlockSpec((tl, 2 * H_M), lambda i, c: (i * nc + c, 0)),
                  pl.BlockSpec((None, 2 * H_M, tl), lambda i, c: (i * nc + c, 0, 0)),
                  pl.BlockSpec((1, d), lambda i, c: (0, 0)),
                  st_c, st_n, st_m],
        out_specs=[rows, st_c, st_n, st_m],
        out_shape=[jax.ShapeDtypeStruct((r, d), BF16),
                   jax.ShapeDtypeStruct((b, H_M, DH_M, DH_M), F32),
                   jax.ShapeDtypeStruct((b, H_M, 1, DH_M), F32),
                   jax.ShapeDtypeStruct((b, H_M, 1, 1), F32)],
        scratch_shapes=[pltpu.VMEM((H_M, DH_M, DH_M), F32), pltpu.VMEM((H_M, 1, DH_M), F32),
                        pltpu.VMEM((H_M, 1, 1), F32)],
        compiler_params=_cparams(("arbitrary", "arbitrary")),
        name="mlstm",
    )(qm, km, vm, om, gates, gates_t, m_g, c0, n0.reshape(b, H_M, 1, DH_M), m0.reshape(b, H_M, 1, 1))
    return h, c_new, n_new.reshape(b, H_M, DH_M), m_new.reshape(b, H_M)


def _merge_kernel(oa_ref, hm_ref, gts_ref, x_ref, g1_ref, wa_ref, wm_ref, wo_ref, o_ref, *, d):
    ya = _dot(oa_ref[...], wa_ref[...])
    ym = _dot(hm_ref[...], wm_ref[...])
    mix = jax.nn.sigmoid(gts_ref[:, 0:d]) * ya + jax.nn.sigmoid(gts_ref[:, d:2 * d]) * ym
    o_ref[...] = x_ref[...] + g1_ref[...] * _dot(mix.astype(BF16), wo_ref[...])


def _merge(oa, hm, gts, x, mod, per_row, wa, wm, wo):
    r, d = x.shape
    tm = min(r, ROW_TILE)
    nat = pl.BlockSpec((tm, d), lambda i: (i, 0))
    wspec = pl.BlockSpec((d, d), lambda i: (0, 0))
    return pl.pallas_call(
        functools.partial(_merge_kernel, d=d),
        grid=(r // tm,),
        in_specs=[nat, nat, pl.BlockSpec((tm, 2 * d), lambda i: (i, 0)), nat,
                  _mod_spec(per_row, tm, d, 2, lambda i: i), wspec, wspec, wspec],
        out_specs=nat,
        out_shape=jax.ShapeDtypeStruct((r, d), F32),
        compiler_params=_cparams(("arbitrary",)),
        name="merge_out",
    )(oa, hm, gts, x, mod, wa, wm, wo)


def _ffn_kernel(x_ref, g_ref, sc_ref, sh_ref, g2_ref, wu_ref, wg_ref, cw_ref, cb_ref, wd_ref, buf_ref, fg_ref,
                o_ref, nb_ref, xn_s, acc_s, tail_s, *, tm, final):
    ti = pl.program_id(1)
    c = pl.program_id(2)

    @pl.when(c == 0)
    def _():
        xn_s[...] = (_rms(x_ref[...], g_ref[...]) * (1.0 + sc_ref[...]) + sh_ref[...]).astype(BF16)
        acc_s[...] = jnp.zeros(acc_s.shape, F32)

    xn = xn_s[...]
    u = _dot(xn, wu_ref[...])
    gate = _dot(xn, wg_ref[...])
    first = ti == 0
    prev2 = jnp.where(first, buf_ref[0:1, :], tail_s[c, 6:7, :])
    prev1 = jnp.where(first, buf_ref[1:2, :], tail_s[c, 7:8, :])
    rowi = lax.broadcasted_iota(jnp.int32, u.shape, 0)
    u1 = jnp.where(rowi == 0, prev1, pltpu.roll(u, 1, 0))
    u2 = jnp.where(rowi == 0, prev2, jnp.where(rowi == 1, prev1, pltpu.roll(u, 2, 0)))
    conv = cb_ref[...] + cw_ref[0:1, :] * u2 + cw_ref[1:2, :] * u1 + cw_ref[2:3, :] * u
    f = jax.nn.gelu(conv) * gate
    acc_s[...] += _dot(f.astype(BF16), wd_ref[...])
    tail_s[c] = u[tm - 8:tm, :]
    nb_ref[...] = u[tm - (CONV_W - 1):tm, :]

    @pl.when(c == pl.num_programs(2) - 1)
    def _():
        y = x_ref[...] + g2_ref[...] * acc_s[...]
        if final:
            y = _rms(y, fg_ref[...])
        o_ref[...] = y


def _ffn(x, mod, per_row, g, w_up, conv_w, conv_b, w_down, conv_buf, final_g, b, t, final):
    r, d = x.shape
    dff = w_down.shape[0]
    tm = min(t, ROW_TILE)
    nt = t // tm
    nc = 2
    ck = dff // nc
    assert ck % 128 == 0 and tm >= 8
    row = lambda i, j, c: i * nt + j
    nat = pl.BlockSpec((tm, d), lambda i, j, c: (i * nt + j, 0))
    vec = pl.BlockSpec((1, d), lambda i, j, c: (0, 0))
    y, tails = pl.pallas_call(
        functools.partial(_ffn_kernel, tm=tm, final=final),
        grid=(b, nt, nc),
        in_specs=[nat, vec, _mod_spec(per_row, tm, d, 4, row), _mod_spec(per_row, tm, d, 3, row),
                  _mod_spec(per_row, tm, d, 5, row),
                  pl.BlockSpec((d, ck), lambda i, j, c: (0, c)),
                  pl.BlockSpec((d, ck), lambda i, j, c: (0, nc + c)),
                  pl.BlockSpec((CONV_W, ck), lambda i, j, c: (0, c)),
                  pl.BlockSpec((1, ck), lambda i, j, c: (0, c)),
                  pl.BlockSpec((ck, d), lambda i, j, c: (c, 0)),
                  pl.BlockSpec((None, CONV_W - 1, ck), lambda i, j, c: (i, 0, c)),
                  vec],
        out_specs=[nat, pl.BlockSpec((None, None, CONV_W - 1, ck), lambda i, j, c: (i, j, 0, c))],
        out_shape=[jax.ShapeDtypeStruct((r, d), F32), jax.ShapeDtypeStruct((b, nt, CONV_W - 1, dff), F32)],
        scratch_shapes=[pltpu.VMEM((tm, d), BF16), pltpu.VMEM((tm, d), F32), pltpu.VMEM((nc, 8, ck), F32)],
        compiler_params=_cparams(("arbitrary", "arbitrary", "arbitrary")),
        name="conv_ffn",
    )(x, g, mod, mod, mod, w_up, w_up, conv_w, conv_b, w_down, conv_buf, final_g)
    return y, tails[:, nt - 1]


def kernel(x_prompt, x_sample, c_prompt, c_sample, cache_k, cache_v, state_C, state_n, state_m, state_conv,
           rel_bias, ada_w, ada_b, norm1_g, norm2_g, w_in, b_if, lam_p, attn_sub_g, mlstm_g, w_br_a, w_br_m,
           w_out, w_up, conv_w, conv_b, w_down, final_g):
    bp, s, d = x_prompt.shape
    bs, ts, _ = x_sample.shape
    depth = w_in.shape[0]
    dff = w_down.shape[1]
    past = cache_k.shape[2]
    assert bp == 1 and d == H_A * HE_A == H_M * DH_M

    c_all = jnp.concatenate([c_prompt, c_sample, jnp.zeros((16 - bp - bs, d), F32)], axis=0)
    mod = _ada(c_all, ada_w, ada_b)
    ck2 = cache_k.reshape(depth * bs * past, d)
    cv2 = cache_v.reshape(depth * bs * past, d)

    xp = x_prompt.reshape(bp * s, d)
    xs = x_sample.reshape(bs * ts, d)
    tl_p = min(MLSTM_TILE, s)
    tl_s = min(CHUNK, ts)
    zero_c = jnp.zeros((bp, H_M, DH_M, DH_M), F32)
    zero_n = jnp.zeros((bp, H_M, DH_M), F32)
    zero_m = jnp.zeros((bp, H_M), F32)
    zero_buf = jnp.zeros((bp, CONV_W - 1, dff), F32)
    st_p, st_s = [], []
    for l in range(depth):
        lam_init = 0.8 - 0.6 * math.exp(-0.3 * l)
        last = l == depth - 1
        w_qkv = w_in[l, :, 0:3 * d].astype(BF16)
        w_m = w_in[l, :, 3 * d:9 * d].astype(BF16)
        w_if = jnp.pad(w_in[l, :, 9 * d:], ((0, 0), (0, 128 - 2 * H_M))).astype(BF16)
        bif = jnp.pad(b_if[l], (0, 128 - 2 * H_M)).reshape(1, 128)
        wa, wm, wo = w_br_a[l].astype(BF16), w_br_m[l].astype(BF16), w_out[l].astype(BF16)
        wu, wd = w_up[l].astype(BF16), w_down[l].astype(BF16)
        g1, g2 = norm1_g[l].reshape(1, d), norm2_g[l].reshape(1, d)
        sub_g, m_g = attn_sub_g[l].reshape(1, HE_A), mlstm_g[l].reshape(1, d)
        cb = conv_b[l].reshape(1, dff)
        fg = final_g.reshape(1, d)

        mod_p = mod[l, 0:bp]
        qT, k32, kbf, v32, vT = _in_attn(xp, mod_p, False, g1, w_qkv, True)
        qm, km, vm, om, gts, gates = _in_mlstm(xp, mod_p, False, g1, w_m, w_if, bif)
        oa = _attn_prompt(qT, kbf, vT, rel_bias, lam_p[l], sub_g, lam_init, min(ATTN_TILE, s))
        hm, c_new, n_new, m_new = _mlstm(qm, km, vm, om, gates, m_g, zero_c, zero_n, zero_m, bp, s, tl_p)
        xp = _merge(oa, hm, gts, xp, mod_p, False, wa, wm, wo)
        xp, buf_new = _ffn(xp, mod_p, False, g2, wu, conv_w[l], cb, wd, zero_buf, fg, bp, s, last)
        st_p.append((k32.reshape(bp, s, H_A, HE_A), v32.reshape(bp, s, H_A, HE_A), c_new, n_new, m_new, buf_new))

        mod_s = jnp.repeat(mod[l, bp:bp + bs], ts, axis=0)
        q, k32, kbf, v32, vbf = _in_attn(xs, mod_s, True, g1, w_qkv, False)
        qm, km, vm, om, gts, gates = _in_mlstm(xs, mod_s, True, g1, w_m, w_if, bif)
        oa = _attn_sample(q, kbf, vbf, ck2, cv2, l, past, rel_bias, lam_p[l], sub_g, lam_init, bs, ts)
        hm, c_new, n_new, m_new = _mlstm(qm, km, vm, om, gates, m_g, state_C[l], state_n[l], state_m[l],
                                         bs, ts, tl_s)
        xs = _merge(oa, hm, gts, xs, mod_s, True, wa, wm, wo)
        xs, buf_new = _ffn(xs, mod_s, True, g2, wu, conv_w[l], cb, wd, state_conv[l], fg, bs, ts, last)
        st_s.append((k32.reshape(bs, ts, H_A, HE_A), v32.reshape(bs, ts, H_A, HE_A), c_new, n_new, m_new, buf_new))

    outs_p = [jnp.stack([st[i] for st in st_p]) for i in range(6)]
    outs_s = [jnp.stack([st[i] for st in st_s]) for i in range(6)]
    return (xp.reshape(bp, s, d), xs.reshape(bs, ts, d), *outs_p, *outs_s)
```

```python
import functools
import math

import numpy as np
import jax
import jax.numpy as jnp
from jax import lax
from jax.experimental import pallas as pl
from jax.experimental.pallas import tpu as pltpu

F32 = jnp.float32
BF16 = jnp.bfloat16

CHUNK = 64
H_A = 8
HD_A = 64
HE_A = 2 * HD_A
H_M = 4
DH_M = 256
N_BUCKETS = 32
CONV_W = 3
NEG_INF = -1e30
EPS = 1e-6
FAR_BUCKET = N_BUCKETS // 2 - 1
LOG2E = math.log2(math.e)

V7X_VMEM_LIMIT = 56 * 1024 * 1024
ROW_TILE = 512
ATTN_TILE = 512
MLSTM_TILE = 256

NT_DIMS = (((1,), (1,)), ((), ()))


def _cparams(sem):
    return pltpu.CompilerParams(dimension_semantics=sem, vmem_limit_bytes=V7X_VMEM_LIMIT)


def _dot(a, b):
    return jnp.dot(a, b, preferred_element_type=F32)


def _rms(x, g):
    return x * lax.rsqrt(jnp.mean(x * x, axis=-1, keepdims=True) + EPS) * g


def _log_sigmoid(x):
    return jnp.minimum(x, 0.0) - jnp.log1p(jnp.exp(-jnp.abs(x)))


def _mod_spec(per_row, tm, d, col, row_index):
    if per_row:
        return pl.BlockSpec((tm, d), lambda *g: (row_index(*g), col))
    return pl.BlockSpec((1, d), lambda *g: (0, col))


def _ada_kernel(c_ref, w_ref, b_ref, o_ref):
    a = jax.nn.silu(c_ref[...]).astype(BF16)
    o_ref[...] = _dot(a, w_ref[...].astype(BF16)) + b_ref[...]


def _ada(c_all, ada_w, ada_b):
    depth, d, n = ada_w.shape
    rc = c_all.shape[0]
    tn = 1536
    return pl.pallas_call(
        _ada_kernel,
        grid=(depth, n // tn),
        in_specs=[pl.BlockSpec((rc, d), lambda l, j: (0, 0)),
                  pl.BlockSpec((None, d, tn), lambda l, j: (l, 0, j)),
                  pl.BlockSpec((None, 1, tn), lambda l, j: (l, 0, j))],
        out_specs=pl.BlockSpec((None, rc, tn), lambda l, j: (l, 0, j)),
        out_shape=jax.ShapeDtypeStruct((depth, rc, n), F32),
        compiler_params=_cparams(("arbitrary", "arbitrary")),
        name="ada_mod",
    )(c_all, ada_w, ada_b.reshape(depth, 1, n))


def _in_attn_kernel(x_ref, g_ref, sc_ref, sh_ref, w_ref, q_ref, k32_ref, kbf_ref, v32_ref, vbf_ref,
                    *, d, transposed):
    xn = (_rms(x_ref[...], g_ref[...]) * (1.0 + sc_ref[...]) + sh_ref[...]).astype(BF16)
    q = _dot(xn, w_ref[:, 0:d]) * (HD_A ** -0.5 * (LOG2E if transposed else 1.0))
    k = _dot(xn, w_ref[:, d:2 * d])
    v = _dot(xn, w_ref[:, 2 * d:3 * d])
    k32_ref[...] = k
    kbf_ref[...] = k.astype(BF16)
    v32_ref[...] = v
    if transposed:
        q_ref[...] = q.T.astype(BF16)
        vbf_ref[...] = v.T.astype(BF16)
    else:
        q_ref[...] = q.astype(BF16)
        vbf_ref[...] = v.astype(BF16)


def _in_attn(x, mod, per_row, g, w_qkv, transposed):
    r, d = x.shape
    tm = min(r, ROW_TILE)
    row = lambda i: i
    nat = pl.BlockSpec((tm, d), lambda i: (i, 0))
    tr = pl.BlockSpec((d, tm), lambda i: (0, i))
    nat_shape = lambda dt: jax.ShapeDtypeStruct((r, d), dt)
    tr_shape = jax.ShapeDtypeStruct((d, r), BF16)
    return pl.pallas_call(
        functools.partial(_in_attn_kernel, d=d, transposed=transposed),
        grid=(r // tm,),
        in_specs=[nat, pl.BlockSpec((1, d), lambda i: (0, 0)),
                  _mod_spec(per_row, tm, d, 1, row), _mod_spec(per_row, tm, d, 0, row),
                  pl.BlockSpec((d, 3 * d), lambda i: (0, 0))],
        out_specs=[tr if transposed else nat, nat, nat, nat, tr if transposed else nat],
        out_shape=[tr_shape if transposed else nat_shape(BF16), nat_shape(F32), nat_shape(BF16),
                   nat_shape(F32), tr_shape if transposed else nat_shape(BF16)],
        compiler_params=_cparams(("arbitrary",)),
        name="in_attn",
    )(x, g, mod, mod, w_qkv)


def _in_mlstm_kernel(x_ref, g_ref, sc_ref, sh_ref, w_ref, wif_ref, bif_ref,
                     qm_ref, km_ref, vm_ref, om_ref, gts_ref, gate_ref, *, d):
    xn = (_rms(x_ref[...], g_ref[...]) * (1.0 + sc_ref[...]) + sh_ref[...]).astype(BF16)
    qm_ref[...] = _dot(xn, w_ref[:, 0:d]).astype(BF16)
    km_ref[...] = (_dot(xn, w_ref[:, d:2 * d]) * (DH_M ** -0.5)).astype(BF16)
    vm_ref[...] = _dot(xn, w_ref[:, 2 * d:3 * d]).astype(BF16)
    om_ref[...] = _dot(xn, w_ref[:, 3 * d:4 * d])
    gts_ref[...] = _dot(xn, w_ref[:, 4 * d:6 * d])
    pre = _dot(xn, wif_ref[...]) + bif_ref[...]
    gate_ref[...] = pre[:, 0:2 * H_M]


def _in_mlstm(x, mod, per_row, g, w_m, w_if, b_if):
    r, d = x.shape
    tm = min(r, ROW_TILE // 2)
    row = lambda i: i
    nat = pl.BlockSpec((tm, d), lambda i: (i, 0))
    const = lambda shape: pl.BlockSpec(shape, lambda i: (0, 0))
    return pl.pallas_call(
        functools.partial(_in_mlstm_kernel, d=d),
        grid=(r // tm,),
        in_specs=[nat, const((1, d)), _mod_spec(per_row, tm, d, 1, row), _mod_spec(per_row, tm, d, 0, row),
                  const(w_m.shape), const(w_if.shape), const(b_if.shape)],
        out_specs=[nat, nat, nat, nat, pl.BlockSpec((tm, 2 * d), lambda i: (i, 0)),
                   pl.BlockSpec((tm, 2 * H_M), lambda i: (i, 0))],
        out_shape=[jax.ShapeDtypeStruct((r, d), BF16)] * 3
        + [jax.ShapeDtypeStruct((r, d), F32), jax.ShapeDtypeStruct((r, 2 * d), F32),
           jax.ShapeDtypeStruct((r, 2 * H_M), F32)],
        compiler_params=_cparams(("arbitrary",)),
        name="in_mlstm",
    )(x, g, mod, mod, w_m, w_if, b_if)


def _bucket_np(rel):
    half = N_BUCKETS // 2
    max_exact = half // 2
    n = np.abs(rel)
    thresholds = np.ceil(max_exact * 2.0 ** (np.arange(1, half - max_exact) / 2.0)).astype(np.int64)
    large = max_exact + (n[..., None] >= thresholds).sum(-1)
    return np.where(rel > 0, half, 0) + np.where(n < max_exact, n, np.minimum(large, half - 1))


def _lam(lp):
    s1 = jnp.sum(lp[0:1] * lp[1:2], axis=-1, keepdims=True)
    s2 = jnp.sum(lp[2:3] * lp[3:4], axis=-1, keepdims=True)
    return jnp.exp(s1) - jnp.exp(s2)


def _bias_steps(lo, hi):
    rel = np.arange(lo, hi + 1)
    b = _bucket_np(rel)
    starts = np.concatenate([[0], np.nonzero(b[1:] != b[:-1])[0] + 1])
    return [(int(rel[k]), int(b[k])) for k in starts]


def _attn_prompt_kernel(rb_ref, lamp_ref, qT_ref, k_ref, vT_ref, subg_ref, o_ref,
                        q_s, bias_s, s_s, mx_s, m_s, l_s, acc_s, *, t, lam_init):
    h = pl.program_id(0)
    i = pl.program_id(1)
    rows = 64
    chunk_shift = CHUNK.bit_length() - 1

    @pl.when(i == 0)
    def _():
        far = rb_ref[FAR_BUCKET, h]
        for tile, shift in enumerate((t, 0)):
            steps = _bias_steps(-shift - (t - 1), -shift + (t - 1))

            def fill(r, carry, tile=tile, shift=shift, steps=steps):
                r0 = pl.multiple_of(r * rows, rows)
                kk = lax.broadcasted_iota(jnp.int32, (rows, t), 0) + r0
                qq = lax.broadcasted_iota(jnp.int32, (rows, t), 1)
                rel = kk - qq - shift
                v = jnp.full((rows, t), (rb_ref[steps[0][1], h] - far) * LOG2E, F32)
                for start, bucket in steps[1:]:
                    v = jnp.where(rel >= start, (rb_ref[bucket, h] - far) * LOG2E, v)
                if tile == 1:
                    visible = jnp.right_shift(kk, chunk_shift) <= jnp.right_shift(qq, chunk_shift)
                    v = jnp.where(visible, v, NEG_INF)
                bias_s[tile, pl.ds(r0, rows), :] = v
                return carry

            lax.fori_loop(0, t // rows, fill, 0)

    qT = qT_ref[...]
    rowi = lax.broadcasted_iota(jnp.int32, qT.shape, 0)
    zero = jnp.zeros_like(qT)
    q_s[0] = jnp.where(rowi < HD_A, qT, zero)
    q_s[1] = jnp.where(rowi >= HD_A, qT, zero)
    m_s[...] = jnp.full(m_s.shape, NEG_INF, F32)
    l_s[...] = jnp.zeros(l_s.shape, F32)
    acc_s[...] = jnp.zeros(acc_s.shape, F32)

    def qk_into(slot, j, tile):
        kb = k_ref[pl.ds(pl.multiple_of(j * t, t), t), :]
        for c in range(2):
            s = _dot(kb, q_s[c])
            if tile is not None:
                s = s + bias_s[tile]
            s_s[slot, c] = s
            mx_s[slot, c] = jnp.max(s, axis=0, keepdims=True)

    def softmax_pv(slot, j):
        vb = vT_ref[:, pl.ds(pl.multiple_of(j * t, t), t)]
        for c in range(2):
            m_old = m_s[c]
            m_new = jnp.maximum(m_old, mx_s[slot, c])
            e = jnp.exp2(s_s[slot, c] - m_new)
            alpha = jnp.exp2(m_old - m_new)
            l_s[c] = alpha * l_s[c] + jnp.sum(e, axis=0, keepdims=True)
            acc_s[c] = alpha * acc_s[c] + _dot(vb, e.astype(BF16))
            m_s[c] = m_new

    nfar = jnp.maximum(i - 1, 0)

    @pl.when(nfar > 0)
    def _():
        qk_into(0, 0, None)

    def pair(p, carry):
        a = 2 * p
        qk_into(1, a + 1, None)
        softmax_pv(0, a)
        qk_into(0, jnp.minimum(a + 2, nfar - 1), None)
        softmax_pv(1, a + 1)
        return carry

    lax.fori_loop(0, lax.shift_right_logical(nfar, 1), pair, 0)

    @pl.when(jnp.bitwise_and(nfar, 1) == 1)
    def _():
        softmax_pv(0, nfar - 1)

    @pl.when(i >= 1)
    def _():
        qk_into(1, i - 1, 0)
        softmax_pv(1, i - 1)

    qk_into(0, i, 1)
    softmax_pv(0, i)

    lam = _lam(lamp_ref[...]) + lam_init
    oT = acc_s[0] / l_s[0] - lam * (acc_s[1] / l_s[1])
    o = _rms(oT.T, subg_ref[...]) * (1.0 - lam_init)
    o_ref[...] = o.astype(BF16)


def _attn_prompt(qT, kbf, vT, rel_bias, lam_p, sub_g, lam_init, t):
    d, s = qT.shape
    assert s % t == 0 and t % 64 == 0 and t % CHUNK == 0 and CHUNK & (CHUNK - 1) == 0
    assert _bucket_np(np.array([-(t + 1)]))[0] == FAR_BUCKET
    nq = s // t
    return pl.pallas_call(
        functools.partial(_attn_prompt_kernel, t=t, lam_init=lam_init),
        grid=(H_A, nq),
        in_specs=[pl.BlockSpec(memory_space=pltpu.SMEM),
                  pl.BlockSpec((4, HD_A), lambda h, i: (0, 0)),
                  pl.BlockSpec((HE_A, t), lambda h, i: (h, i)),
                  pl.BlockSpec((s, HE_A), lambda h, i: (0, h)),
                  pl.BlockSpec((HE_A, s), lambda h, i: (h, 0)),
                  pl.BlockSpec((1, HE_A), lambda h, i: (0, 0))],
        out_specs=pl.BlockSpec((t, HE_A), lambda h, i: (i, h)),
        out_shape=jax.ShapeDtypeStruct((s, d), BF16),
        scratch_shapes=[pltpu.VMEM((2, HE_A, t), BF16), pltpu.VMEM((2, t, t), F32),
                        pltpu.VMEM((2, 2, t, t), F32), pltpu.VMEM((2, 2, 1, t), F32),
                        pltpu.VMEM((2, 1, t), F32), pltpu.VMEM((2, 1, t), F32),
                        pltpu.VMEM((2, HE_A, t), F32)],
        compiler_params=_cparams(("arbitrary", "arbitrary")),
        name="attn_prompt",
    )(rel_bias.astype(F32), lam_p, qT, kbf, vT, sub_g)


def _attn_sample_kernel(lamp_ref, q_ref, kc_ref, vc_ref, kn_ref, vn_ref, bc_ref, bn_ref, subg_ref, o_ref,
                        *, lam_init):
    q = q_ref[...]
    lane = lax.broadcasted_iota(jnp.int32, q.shape, 1)
    zero = jnp.zeros_like(q)
    kc = kc_ref[...].astype(BF16)
    kn = kn_ref[...]
    probs = []
    for qc in (jnp.where(lane < HD_A, q, zero), jnp.where(lane >= HD_A, q, zero)):
        sc = lax.dot_general(qc, kc, NT_DIMS, preferred_element_type=F32) + bc_ref[...]
        sn = lax.dot_general(qc, kn, NT_DIMS, preferred_element_type=F32) + bn_ref[...]
        m = jnp.maximum(jnp.max(sc, axis=-1, keepdims=True), jnp.max(sn, axis=-1, keepdims=True))
        ec = jnp.exp(sc - m)
        en = jnp.exp(sn - m)
        l = jnp.sum(ec, axis=-1, keepdims=True) + jnp.sum(en, axis=-1, keepdims=True)
        probs.append((ec / l, en / l))
    lam = _lam(lamp_ref[...]) + lam_init
    ac = probs[0][0] - lam * probs[1][0]
    an = probs[0][1] - lam * probs[1][1]
    o = _dot(ac.astype(BF16), vc_ref[...].astype(BF16)) + _dot(an.astype(BF16), vn_ref[...])
    o_ref[...] = (_rms(o, subg_ref[...]) * (1.0 - lam_init)).astype(BF16)


def _attn_sample(q, kbf, vbf, cache_k, cache_v, layer, past, rel_bias, lam_p, sub_g, lam_init, b, t):
    r, d = q.shape
    qpos = past + np.arange(t)[:, None]
    kpos = np.arange(past + t)[None, :]
    idx = _bucket_np(kpos - qpos)
    visible = (kpos // CHUNK) <= (qpos // CHUNK)
    bias = jnp.where(visible[None], jnp.transpose(rel_bias.astype(F32)[idx], (2, 0, 1)), NEG_INF)
    nb = layer * b
    return pl.pallas_call(
        functools.partial(_attn_sample_kernel, lam_init=lam_init),
        grid=(b, H_A),
        in_specs=[pl.BlockSpec((4, HD_A), lambda i, h: (0, 0)),
                  pl.BlockSpec((t, HE_A), lambda i, h: (i, h)),
                  pl.BlockSpec((past, HE_A), lambda i, h: (nb + i, h)),
                  pl.BlockSpec((past, HE_A), lambda i, h: (nb + i, h)),
                  pl.BlockSpec((t, HE_A), lambda i, h: (i, h)),
                  pl.BlockSpec((t, HE_A), lambda i, h: (i, h)),
                  pl.BlockSpec((None, t, past), lambda i, h: (h, 0, 0)),
                  pl.BlockSpec((None, t, t), lambda i, h: (h, 0, 0)),
                  pl.BlockSpec((1, HE_A), lambda i, h: (0, 0))],
        out_specs=pl.BlockSpec((t, HE_A), lambda i, h: (i, h)),
        out_shape=jax.ShapeDtypeStruct((r, d), BF16),
        compiler_params=_cparams(("arbitrary", "arbitrary")),
        name="attn_sample",
    )(lam_p, q, cache_k, cache_v, kbf, vbf, bias[:, :, :past], bias[:, :, past:], sub_g)


def _mlstm_kernel(q_ref, k_ref, v_ref, om_ref, g_ref, gT_ref, mg_ref, c0_ref, n0_ref, m0_ref,
                  h_ref, c_ref, n_ref, m_ref, c_s, n_s, m_s, *, tl):
    step = pl.program_id(1)

    @pl.when(step == 0)
    def _():
        c_s[...] = c0_ref[...]
        n_s[...] = n0_ref[...]
        m_s[...] = m0_ref[...]

    ti = lax.broadcasted_iota(jnp.int32, (tl, tl), 0)
    si = lax.broadcasted_iota(jnp.int32, (tl, tl), 1)
    causal = si <= ti
    for h in range(H_M):
        sl = slice(h * DH_M, (h + 1) * DH_M)
        q = q_ref[:, sl]
        k = k_ref[:, sl]
        v = v_ref[:, sl]
        ig_c = g_ref[:, h:h + 1]
        lf_c = _log_sigmoid(g_ref[:, H_M + h:H_M + h + 1])
        ig_r = gT_ref[h:h + 1, :]
        lf_r = _log_sigmoid(gT_ref[H_M + h:H_M + h + 1, :])
        m = m_s[h]
        n_row = n_s[h]
        b_c = jnp.sum(jnp.where(causal, lf_r, 0.0), axis=1, keepdims=True)
        b_r = jnp.sum(jnp.where(ti <= si, lf_c, 0.0), axis=0, keepdims=True)
        dmat = jnp.where(causal, b_c - b_r + ig_r, -jnp.inf)
        inter = b_c + m
        m_t = jnp.maximum(inter, jnp.max(dmat, axis=1, keepdims=True))
        w_intra = jnp.exp(dmat - m_t)
        w_inter = jnp.exp(inter - m_t)
        a = w_intra * lax.dot_general(q, k, NT_DIMS, preferred_element_type=F32)
        num = w_inter * _dot(q, c_s[h].astype(BF16)) + _dot(a.astype(BF16), v)
        den = (w_inter * jnp.sum(q.astype(F32) * n_row, axis=1, keepdims=True)
               + jnp.sum(a, axis=1, keepdims=True))
        hh = num / jnp.maximum(jnp.abs(den), jnp.exp(-m_t))
        b_last = b_r[:, tl - 1:tl]
        g_c = b_last - b_c + ig_c
        m_new = jnp.maximum(b_last + m, jnp.max(g_c, axis=0, keepdims=True))
        decay = jnp.exp(b_last + m - m_new)
        kw = jnp.exp(g_c - m_new) * k.astype(F32)
        c_s[h] = decay * c_s[h] + _dot(kw.T.astype(BF16), v)
        n_s[h] = decay * n_row + jnp.sum(kw, axis=0, keepdims=True)
        m_s[h] = m_new
        hn = _rms(hh, mg_ref[:, sl])
        h_ref[:, sl] = (hn * jax.nn.sigmoid(om_ref[:, sl])).astype(BF16)

    @pl.when(step == pl.num_programs(1) - 1)
    def _():
        c_ref[...] = c_s[...]
        n_ref[...] = n_s[...]
        m_ref[...] = m_s[...]


def _mlstm(qm, km, vm, om, gates, m_g, c0, n0, m0, b, t, tl):
    r, d = qm.shape
    nc = t // tl
    gates_t = jnp.transpose(gates.reshape(b * nc, tl, 2 * H_M), (0, 2, 1))
    rows = pl.BlockSpec((tl, d), lambda i, c: (i * nc + c, 0))
    st_c = pl.BlockSpec((None, H_M, DH_M, DH_M), lambda i, c: (i, 0, 0, 0))
    st_n = pl.BlockSpec((None, H_M, 1, DH_M), lambda i, c: (i, 0, 0, 0))
    st_m = pl.BlockSpec((None, H_M, 1, 1), lambda i, c: (i, 0, 0, 0))
    h, c_new, n_new, m_new = pl.pallas_call(
        functools.partial(_mlstm_kernel, tl=tl),
        grid=(b, nc),
        in_specs=[rows, rows, rows, rows,
                  pl.BlockSpec((tl, 2 * H_M), lambda i, c: (i * nc + c, 0)),
                  pl.BlockSpec((None, 2 * H_M, tl), lambda i, c: (i * nc + c, 0, 0)),
                  pl.BlockSpec((1, d), lambda i, c: (0, 0)),
                  st_c, st_n, st_m],
        out_specs=[rows, st_c, st_n, st_m],
        out_shape=[jax.ShapeDtypeStruct((r, d), BF16),
                   jax.ShapeDtypeStruct((b, H_M, DH_M, DH_M), F32),
                   jax.ShapeDtypeStruct((b, H_M, 1, DH_M), F32),
                   jax.ShapeDtypeStruct((b, H_M, 1, 1), F32)],
        scratch_shapes=[pltpu.VMEM((H_M, DH_M, DH_M), F32), pltpu.VMEM((H_M, 1, DH_M), F32),
                        pltpu.VMEM((H_M, 1, 1), F32)],
        compiler_params=_cparams(("arbitrary", "arbitrary")),
        name="mlstm",
    )(qm, km, vm, om, gates, gates_t, m_g, c0, n0.reshape(b, H_M, 1, DH_M), m0.reshape(b, H_M, 1, 1))
    return h, c_new, n_new.reshape(b, H_M, DH_M), m_new.reshape(b, H_M)


def _merge_kernel(oa_ref, hm_ref, gts_ref, x_ref, g1_ref, wa_ref, wm_ref, wo_ref, o_ref, *, d):
    ya = _dot(oa_ref[...], wa_ref[...])
    ym = _dot(hm_ref[...], wm_ref[...])
    mix = jax.nn.sigmoid(gts_ref[:, 0:d]) * ya + jax.nn.sigmoid(gts_ref[:, d:2 * d]) * ym
    o_ref[...] = x_ref[...] + g1_ref[...] * _dot(mix.astype(BF16), wo_ref[...])


def _merge(oa, hm, gts, x, mod, per_row, wa, wm, wo):
    r, d = x.shape
    tm = min(r, ROW_TILE)
    nat = pl.BlockSpec((tm, d), lambda i: (i, 0))
    wspec = pl.BlockSpec((d, d), lambda i: (0, 0))
    return pl.pallas_call(
        functools.partial(_merge_kernel, d=d),
        grid=(r // tm,),
        in_specs=[nat, nat, pl.BlockSpec((tm, 2 * d), lambda i: (i, 0)), nat,
                  _mod_spec(per_row, tm, d, 2, lambda i: i), wspec, wspec, wspec],
        out_specs=nat,
        out_shape=jax.ShapeDtypeStruct((r, d), F32),
        compiler_params=_cparams(("arbitrary",)),
        name="merge_out",
    )(oa, hm, gts, x, mod, wa, wm, wo)


def _ffn_kernel(x_ref, g_ref, sc_ref, sh_ref, g2_ref, wu_ref, wg_ref, cw_ref, cb_ref, wd_ref, buf_ref, fg_ref,
                o_ref, nb_ref, xn_s, acc_s, tail_s, *, tm, final):
    ti = pl.program_id(1)
    c = pl.program_id(2)

    @pl.when(c == 0)
    def _():
        xn_s[...] = (_rms(x_ref[...], g_ref[...]) * (1.0 + sc_ref[...]) + sh_ref[...]).astype(BF16)
        acc_s[...] = jnp.zeros(acc_s.shape, F32)

    xn = xn_s[...]
    u = _dot(xn, wu_ref[...])
    gate = _dot(xn, wg_ref[...])
    first = ti == 0
    prev2 = jnp.where(first, buf_ref[0:1, :], tail_s[c, 6:7, :])
    prev1 = jnp.where(first, buf_ref[1:2, :], tail_s[c, 7:8, :])
    rowi = lax.broadcasted_iota(jnp.int32, u.shape, 0)
    u1 = jnp.where(rowi == 0, prev1, pltpu.roll(u, 1, 0))
    u2 = jnp.where(rowi == 0, prev2, jnp.where(rowi == 1, prev1, pltpu.roll(u, 2, 0)))
    conv = cb_ref[...] + cw_ref[0:1, :] * u2 + cw_ref[1:2, :] * u1 + cw_ref[2:3, :] * u
    f = jax.nn.gelu(conv) * gate
    acc_s[...] += _dot(f.astype(BF16), wd_ref[...])
    tail_s[c] = u[tm - 8:tm, :]
    nb_ref[...] = u[tm - (CONV_W - 1):tm, :]

    @pl.when(c == pl.num_programs(2) - 1)
    def _():
        y = x_ref[...] + g2_ref[...] * acc_s[...]
        if final:
            y = _rms(y, fg_ref[...])
        o_ref[...] = y


def _ffn(x, mod, per_row, g, w_up, conv_w, conv_b, w_down, conv_buf, final_g, b, t, final):
    r, d = x.shape
    dff = w_down.shape[0]
    tm = min(t, ROW_TILE)
    nt = t // tm
    nc = 2
    ck = dff // nc
    assert ck % 128 == 0 and tm >= 8
    row = lambda i, j, c: i * nt + j
    nat = pl.BlockSpec((tm, d), lambda i, j, c: (i * nt + j, 0))
    vec = pl.BlockSpec((1, d), lambda i, j, c: (0, 0))
    y, tails = pl.pallas_call(
        functools.partial(_ffn_kernel, tm=tm, final=final),
        grid=(b, nt, nc),
        in_specs=[nat, vec, _mod_spec(per_row, tm, d, 4, row), _mod_spec(per_row, tm, d, 3, row),
                  _mod_spec(per_row, tm, d, 5, row),
                  pl.BlockSpec((d, ck), lambda i, j, c: (0, c)),
                  pl.BlockSpec((d, ck), lambda i, j, c: (0, nc + c)),
                  pl.BlockSpec((CONV_W, ck), lambda i, j, c: (0, c)),
                  pl.BlockSpec((1, ck), lambda i, j, c: (0, c)),
                  pl.BlockSpec((ck, d), lambda i, j, c: (c, 0)),
                  pl.BlockSpec((None, CONV_W - 1, ck), lambda i, j, c: (i, 0, c)),
                  vec],
        out_specs=[nat, pl.BlockSpec((None, None, CONV_W - 1, ck), lambda i, j, c: (i, j, 0, c))],
        out_shape=[jax.ShapeDtypeStruct((r, d), F32), jax.ShapeDtypeStruct((b, nt, CONV_W - 1, dff), F32)],
        scratch_shapes=[pltpu.VMEM((tm, d), BF16), pltpu.VMEM((tm, d), F32), pltpu.VMEM((nc, 8, ck), F32)],
        compiler_params=_cparams(("arbitrary", "arbitrary", "arbitrary")),
        name="conv_ffn",
    )(x, g, mod, mod, mod, w_up, w_up, conv_w, conv_b, w_down, conv_buf, final_g)
    return y, tails[:, nt - 1]


def kernel(x_prompt, x_sample, c_prompt, c_sample, cache_k, cache_v, state_C, state_n, state_m, state_conv,
           rel_bias, ada_w, ada_b, norm1_g, norm2_g, w_in, b_if, lam_p, attn_sub_g, mlstm_g, w_br_a, w_br_m,
           w_out, w_up, conv_w, conv_b, w_down, final_g):
    bp, s, d = x_prompt.shape
    bs, ts, _ = x_sample.shape
    depth = w_in.shape[0]
    dff = w_down.shape[1]
    past = cache_k.shape[2]
    assert bp == 1 and d == H_A * HE_A == H_M * DH_M

    c_all = jnp.concatenate([c_prompt, c_sample, jnp.zeros((16 - bp - bs, d), F32)], axis=0)
    mod = _ada(c_all, ada_w, ada_b)
    ck2 = cache_k.reshape(depth * bs * past, d)
    cv2 = cache_v.reshape(depth * bs * past, d)

    xp = x_prompt.reshape(bp * s, d)
    xs = x_sample.reshape(bs * ts, d)
    tl_p = min(MLSTM_TILE, s)
    tl_s = min(CHUNK, ts)
    zero_c = jnp.zeros((bp, H_M, DH_M, DH_M), F32)
    zero_n = jnp.zeros((bp, H_M, DH_M), F32)
    zero_m = jnp.zeros((bp, H_M), F32)
    zero_buf = jnp.zeros((bp, CONV_W - 1, dff), F32)
    st_p, st_s = [], []
    for l in range(depth):
        lam_init = 0.8 - 0.6 * math.exp(-0.3 * l)
        last = l == depth - 1
        w_qkv = w_in[l, :, 0:3 * d].astype(BF16)
        w_m = w_in[l, :, 3 * d:9 * d].astype(BF16)
        w_if = jnp.pad(w_in[l, :, 9 * d:], ((0, 0), (0, 128 - 2 * H_M))).astype(BF16)
        bif = jnp.pad(b_if[l], (0, 128 - 2 * H_M)).reshape(1, 128)
        wa, wm, wo = w_br_a[l].astype(BF16), w_br_m[l].astype(BF16), w_out[l].astype(BF16)
        wu, wd = w_up[l].astype(BF16), w_down[l].astype(BF16)
        g1, g2 = norm1_g[l].reshape(1, d), norm2_g[l].reshape(1, d)
        sub_g, m_g = attn_sub_g[l].reshape(1, HE_A), mlstm_g[l].reshape(1, d)
        cb = conv_b[l].reshape(1, dff)
        fg = final_g.reshape(1, d)

        mod_p = mod[l, 0:bp]
        qT, k32, kbf, v32, vT = _in_attn(xp, mod_p, False, g1, w_qkv, True)
        qm, km, vm, om, gts, gates = _in_mlstm(xp, mod_p, False, g1, w_m, w_if, bif)
        oa = _attn_prompt(qT, kbf, vT, rel_bias, lam_p[l], sub_g, lam_init, min(ATTN_TILE, s))
        hm, c_new, n_new, m_new = _mlstm(qm, km, vm, om, gates, m_g, zero_c, zero_n, zero_m, bp, s, tl_p)
        xp = _merge(oa, hm, gts, xp, mod_p, False, wa, wm, wo)
        xp, buf_new = _ffn(xp, mod_p, False, g2, wu, conv_w[l], cb, wd, zero_buf, fg, bp, s, last)
        st_p.append((k32.reshape(bp, s, H_A, HE_A), v32.reshape(bp, s, H_A, HE_A), c_new, n_new, m_new, buf_new))

        mod_s = jnp.repeat(mod[l, bp:bp + bs], ts, axis=0)
        q, k32, kbf, v32, vbf = _in_attn(xs, mod_s, True, g1, w_qkv, False)
        qm, km, vm, om, gts, gates = _in_mlstm(xs, mod_s, True, g1, w_m, w_if, bif)
        oa = _attn_sample(q, kbf, vbf, ck2, cv2, l, past, rel_bias, lam_p[l], sub_g, lam_init, bs, ts)
        hm, c_new, n_new, m_new = _mlstm(qm, km, vm, om, gates, m_g, state_C[l], state_n[l], state_m[l],
                                         bs, ts, tl_s)
        xs = _merge(oa, hm, gts, xs, mod_s, True, wa, wm, wo)
        xs, buf_new = _ffn(xs, mod_s, True, g2, wu, conv_w[l], cb, wd, state_conv[l], fg, bs, ts, last)
        st_s.append((k32.reshape(bs, ts, H_A, HE_A), v32.reshape(bs, ts, H_A, HE_A), c_new, n_new, m_new, buf_new))

    outs_p = [jnp.stack([st[i] for st in st_p]) for i in range(6)]
    outs_s = [jnp.stack([st[i] for st in st_s]) for i in range(6)]
    return (xp.reshape(bp, s, d), xs.reshape(bs, ts, d), *outs_p, *outs_s)
```

```python
import functools
import math

import numpy as np
import jax
import jax.numpy as jnp
from jax import lax
from jax.experimental import pallas as pl
from jax.experimental.pallas import tpu as pltpu

F32 = jnp.float32
BF16 = jnp.bfloat16

CHUNK = 64
H_A = 8
HD_A = 64
HE_A = 2 * HD_A
H_M = 4
DH_M = 256
N_BUCKETS = 32
CONV_W = 3
NEG_INF = -1e30
EPS = 1e-6
FAR_BUCKET = N_BUCKETS // 2 - 1
LOG2E = math.log2(math.e)

V7X_VMEM_LIMIT = 56 * 1024 * 1024
ROW_TILE = 512
ATTN_TILE = 512
ATTN_FAR_BLOCKS = 4
BF16_ROWS = 16
MLSTM_TILE = 256

NT_DIMS = (((1,), (1,)), ((), ()))


def _cparams(sem):
    return pltpu.CompilerParams(dimension_semantics=sem, vmem_limit_bytes=V7X_VMEM_LIMIT)


def _dot(a, b):
    return jnp.dot(a, b, preferred_element_type=F32)


def _rms(x, g):
    return x * lax.rsqrt(jnp.mean(x * x, axis=-1, keepdims=True) + EPS) * g


def _log_sigmoid(x):
    return jnp.minimum(x, 0.0) - jnp.log1p(jnp.exp(-jnp.abs(x)))


def _mod_spec(per_row, tm, d, col, row_index):
    if per_row:
        return pl.BlockSpec((tm, d), lambda *g: (row_index(*g), col))
    return pl.BlockSpec((1, d), lambda *g: (0, col))


def _ada_kernel(c_ref, w_ref, b_ref, o_ref):
    a = jax.nn.silu(c_ref[...]).astype(BF16)
    o_ref[...] = _dot(a, w_ref[...].astype(BF16)) + b_ref[...]


def _ada(c_all, ada_w, ada_b):
    depth, d, n = ada_w.shape
    rc = c_all.shape[0]
    tn = 1536
    return pl.pallas_call(
        _ada_kernel,
        grid=(depth, n // tn),
        in_specs=[pl.BlockSpec((rc, d), lambda l, j: (0, 0)),
                  pl.BlockSpec((None, d, tn), lambda l, j: (l, 0, j)),
                  pl.BlockSpec((None, 1, tn), lambda l, j: (l, 0, j))],
        out_specs=pl.BlockSpec((None, rc, tn), lambda l, j: (l, 0, j)),
        out_shape=jax.ShapeDtypeStruct((depth, rc, n), F32),
        compiler_params=_cparams(("arbitrary", "arbitrary")),
        name="ada_mod",
    )(c_all, ada_w, ada_b.reshape(depth, 1, n))


def _in_attn_kernel(x_ref, g_ref, sc_ref, sh_ref, w_ref, q_ref, k32_ref, kbf_ref, v32_ref, vbf_ref,
                    *, d, transposed):
    xn = (_rms(x_ref[...], g_ref[...]) * (1.0 + sc_ref[...]) + sh_ref[...]).astype(BF16)
    q = _dot(xn, w_ref[:, 0:d]) * (HD_A ** -0.5 * (LOG2E if transposed else 1.0))
    k = _dot(xn, w_ref[:, d:2 * d])
    v = _dot(xn, w_ref[:, 2 * d:3 * d])
    k32_ref[...] = k
    kbf_ref[...] = k.astype(BF16)
    v32_ref[...] = v
    if transposed:
        q_ref[...] = q.T.astype(BF16)
        vbf_ref[...] = v.T.astype(BF16)
    else:
        q_ref[...] = q.astype(BF16)
        vbf_ref[...] = v.astype(BF16)


def _in_attn(x, mod, per_row, g, w_qkv, transposed):
    r, d = x.shape
    tm = min(r, ROW_TILE)
    row = lambda i: i
    nat = pl.BlockSpec((tm, d), lambda i: (i, 0))
    tr = pl.BlockSpec((d, tm), lambda i: (0, i))
    nat_shape = lambda dt: jax.ShapeDtypeStruct((r, d), dt)
    tr_shape = jax.ShapeDtypeStruct((d, r), BF16)
    return pl.pallas_call(
        functools.partial(_in_attn_kernel, d=d, transposed=transposed),
        grid=(r // tm,),
        in_specs=[nat, pl.BlockSpec((1, d), lambda i: (0, 0)),
                  _mod_spec(per_row, tm, d, 1, row), _mod_spec(per_row, tm, d, 0, row),
                  pl.BlockSpec((d, 3 * d), lambda i: (0, 0))],
        out_specs=[tr if transposed else nat, nat, nat, nat, tr if transposed else nat],
        out_shape=[tr_shape if transposed else nat_shape(BF16), nat_shape(F32), nat_shape(BF16),
                   nat_shape(F32), tr_shape if transposed else nat_shape(BF16)],
        compiler_params=_cparams(("arbitrary",)),
        name="in_attn",
    )(x, g, mod, mod, w_qkv)


def _in_mlstm_kernel(x_ref, g_ref, sc_ref, sh_ref, w_ref, wif_ref, bif_ref,
                     qm_ref, km_ref, vm_ref, om_ref, gts_ref, gate_ref, *, d):
    xn = (_rms(x_ref[...], g_ref[...]) * (1.0 + sc_ref[...]) + sh_ref[...]).astype(BF16)
    qm_ref[...] = _dot(xn, w_ref[:, 0:d]).astype(BF16)
    km_ref[...] = (_dot(xn, w_ref[:, d:2 * d]) * (DH_M ** -0.5)).astype(BF16)
    vm_ref[...] = _dot(xn, w_ref[:, 2 * d:3 * d]).astype(BF16)
    om_ref[...] = _dot(xn, w_ref[:, 3 * d:4 * d])
    gts_ref[...] = _dot(xn, w_ref[:, 4 * d:6 * d])
    pre = _dot(xn, wif_ref[...]) + bif_ref[...]
    gate_ref[...] = pre[:, 0:2 * H_M]


def _in_mlstm(x, mod, per_row, g, w_m, w_if, b_if):
    r, d = x.shape
    tm = min(r, ROW_TILE // 2)
    row = lambda i: i
    nat = pl.BlockSpec((tm, d), lambda i: (i, 0))
    const = lambda shape: pl.BlockSpec(shape, lambda i: (0, 0))
    return pl.pallas_call(
        functools.partial(_in_mlstm_kernel, d=d),
        grid=(r // tm,),
        in_specs=[nat, const((1, d)), _mod_spec(per_row, tm, d, 1, row), _mod_spec(per_row, tm, d, 0, row),
                  const(w_m.shape), const(w_if.shape), const(b_if.shape)],
        out_specs=[nat, nat, nat, nat, pl.BlockSpec((tm, 2 * d), lambda i: (i, 0)),
                   pl.BlockSpec((tm, 2 * H_M), lambda i: (i, 0))],
        out_shape=[jax.ShapeDtypeStruct((r, d), BF16)] * 3
        + [jax.ShapeDtypeStruct((r, d), F32), jax.ShapeDtypeStruct((r, 2 * d), F32),
           jax.ShapeDtypeStruct((r, 2 * H_M), F32)],
        compiler_params=_cparams(("arbitrary",)),
        name="in_mlstm",
    )(x, g, mod, mod, w_m, w_if, b_if)


def _bucket_np(rel):
    half = N_BUCKETS // 2
    max_exact = half // 2
    n = np.abs(rel)
    thresholds = np.ceil(max_exact * 2.0 ** (np.arange(1, half - max_exact) / 2.0)).astype(np.int64)
    large = max_exact + (n[..., None] >= thresholds).sum(-1)
    return np.where(rel > 0, half, 0) + np.where(n < max_exact, n, np.minimum(large, half - 1))


def _lam(lp):
    s1 = jnp.sum(lp[0:1] * lp[1:2], axis=-1, keepdims=True)
    s2 = jnp.sum(lp[2:3] * lp[3:4], axis=-1, keepdims=True)
    return jnp.exp(s1) - jnp.exp(s2)


def _bias_steps(lo, hi):
    rel = np.arange(lo, hi + 1)
    b = _bucket_np(rel)
    starts = np.concatenate([[0], np.nonzero(b[1:] != b[:-1])[0] + 1])
    return [(int(rel[k]), int(b[k])) for k in starts]


def _rel_bias_tile(rb_ref, h, rel, lo, hi, shift, scale):
    steps = _bias_steps(lo, hi)
    v = jnp.full(rel.shape, (rb_ref[steps[0][1], h] - shift) * scale, F32)
    for start, bucket in steps[1:]:
        v = jnp.where(rel >= start, (rb_ref[bucket, h] - shift) * scale, v)
    return v


def _attn_prompt_kernel(rb_ref, lamp_ref, qT_ref, k_ref, vT_ref, subg_ref, o_ref,
                        q_s, bias_s, m_s, acc_s, *, t, far_blocks, lam_init):
    h = pl.program_id(0)
    i = pl.program_id(1)
    rows = 64
    chunk_shift = CHUNK.bit_length() - 1

    @pl.when(i == 0)
    def _():
        far = rb_ref[FAR_BUCKET, h]
        for tile, shift in enumerate((t, 0)):

            def fill(r, carry, tile=tile, shift=shift):
                r0 = pl.multiple_of(r * rows, rows)
                kk = lax.broadcasted_iota(jnp.int32, (rows, t), 0) + r0
                qq = lax.broadcasted_iota(jnp.int32, (rows, t), 1)
                v = _rel_bias_tile(rb_ref, h, kk - qq - shift, -shift - (t - 1), -shift + (t - 1), far, LOG2E)
                if tile == 1:
                    visible = jnp.right_shift(kk, chunk_shift) <= jnp.right_shift(qq, chunk_shift)
                    v = jnp.where(visible, v, NEG_INF)
                bias_s[tile, pl.ds(r0, rows), :] = v
                return carry

            lax.fori_loop(0, t // rows, fill, 0)

    qT = qT_ref[...]
    rowi = lax.broadcasted_iota(jnp.int32, qT.shape, 0)
    zero = jnp.zeros_like(qT)
    q_s[0] = jnp.where(rowi < HD_A, qT, zero)
    q_s[1] = jnp.where(rowi >= HD_A, qT, zero)

    def keys(j0, n):
        off = pl.multiple_of(j0 * t, t)
        tail = (lax.broadcasted_iota(jnp.int32, (BF16_ROWS, n), 0) == 0).astype(BF16)
        return k_ref[pl.ds(off, n), :], jnp.concatenate([vT_ref[:, pl.ds(off, n)], tail], axis=0)

    def scores(kb, c, tile):
        s = _dot(kb, q_s[c])
        return s if tile is None else s + bias_s[tile]

    def fixed_shift_block(j0, n, tile):
        kb, vb = keys(j0, n)
        for c in range(2):
            e = jnp.exp2(scores(kb, c, tile) - m_s[c])
            acc_s[c] += _dot(vb, e.astype(BF16))

    def running_max_block(j0, tile):
        kb, vb = keys(j0, t)
        for c in range(2):
            s = scores(kb, c, tile)
            m_old = m_s[c]
            m_new = jnp.maximum(m_old, jnp.max(s, axis=0, keepdims=True))
            acc_s[c] = jnp.exp2(m_old - m_new) * acc_s[c] + _dot(vb, jnp.exp2(s - m_new).astype(BF16))
            m_s[c] = m_new

    def finish():
        lam = _lam(lamp_ref[...]) + lam_init
        a0, a1 = acc_s[0], acc_s[1]
        l0, l1 = a0[HE_A:HE_A + 1], a1[HE_A:HE_A + 1]
        oT = a0[0:HE_A] / l0 - lam * (a1[0:HE_A] / l1)
        o_ref[...] = (_rms(oT.T, subg_ref[...]) * (1.0 - lam_init)).astype(BF16)
        big = float(np.finfo(np.float32).max)
        bad_l = jnp.where((l0 <= big) & (l1 <= big), 0.0, 1.0)
        bad_o = jnp.where(jnp.abs(oT) <= big, 0.0, 1.0)
        return jnp.max(bad_l) + jnp.max(bad_o)

    nfar = jnp.maximum(i - 1, 0)
    nbig = nfar // far_blocks

    kb, vb = keys(i, t)
    for c in range(2):
        s = scores(kb, c, 1)
        m = jnp.max(s, axis=0, keepdims=True)
        m_s[c] = m
        acc_s[c] = _dot(vb, jnp.exp2(s - m).astype(BF16))

    @pl.when(i >= 1)
    def _():
        fixed_shift_block(i - 1, t, 0)

    def big_step(p, carry):
        fixed_shift_block(p * far_blocks, far_blocks * t, None)
        return carry

    def small_step(j, carry):
        fixed_shift_block(j, t, None)
        return carry

    lax.fori_loop(0, nbig, big_step, 0)
    lax.fori_loop(nbig * far_blocks, nfar, small_step, 0)

    @pl.when(finish() > 0.0)
    def _():
        m_s[...] = jnp.full(m_s.shape, NEG_INF, F32)
        acc_s[...] = jnp.zeros(acc_s.shape, F32)

        def step(j, carry):
            running_max_block(j, None)
            return carry

        lax.fori_loop(0, nfar, step, 0)

        @pl.when(i >= 1)
        def _():
            running_max_block(i - 1, 0)

        running_max_block(i, 1)
        finish()


def _attn_prompt(qT, kbf, vT, rel_bias, lam_p, sub_g, lam_init, t):
    d, s = qT.shape
    assert s % t == 0 and t % 64 == 0 and t % CHUNK == 0 and CHUNK & (CHUNK - 1) == 0
    assert _bucket_np(np.array([-(t + 1)]))[0] == FAR_BUCKET
    nq = s // t
    return pl.pallas_call(
        functools.partial(_attn_prompt_kernel, t=t, far_blocks=ATTN_FAR_BLOCKS, lam_init=lam_init),
        grid=(H_A, nq),
        in_specs=[pl.BlockSpec(memory_space=pltpu.SMEM),
                  pl.BlockSpec((4, HD_A), lambda h, i: (0, 0)),
                  pl.BlockSpec((HE_A, t), lambda h, i: (h, i)),
                  pl.BlockSpec((s, HE_A), lambda h, i: (0, h)),
                  pl.BlockSpec((HE_A, s), lambda h, i: (h, 0)),
                  pl.BlockSpec((1, HE_A), lambda h, i: (0, 0))],
        out_specs=pl.BlockSpec((t, HE_A), lambda h, i: (i, h)),
        out_shape=jax.ShapeDtypeStruct((s, d), BF16),
        scratch_shapes=[pltpu.VMEM((2, HE_A, t), BF16), pltpu.VMEM((2, t, t), F32),
                        pltpu.VMEM((2, 1, t), F32), pltpu.VMEM((2, HE_A + BF16_ROWS, t), F32)],
        compiler_params=_cparams(("arbitrary", "arbitrary")),
        name="attn_prompt",
    )(rel_bias.astype(F32), lam_p, qT, kbf, vT, sub_g)


def _attn_sample_kernel(rb_ref, lamp_ref, q_ref, kc_ref, vc_ref, kn_ref, vn_ref, subg_ref, o_ref,
                        *, past, lam_init):
    t = q_ref.shape[0]
    chunk_shift = CHUNK.bit_length() - 1
    lam = _lam(lamp_ref[...]) + lam_init

    def positions(n_keys, first_key):
        qpos = past + lax.broadcasted_iota(jnp.int32, (t, n_keys), 0)
        kpos = first_key + lax.broadcasted_iota(jnp.int32, (t, n_keys), 1)
        return kpos - qpos, jnp.right_shift(kpos, chunk_shift) <= jnp.right_shift(qpos, chunk_shift)

    rel_c, vis_c = positions(past, 0)
    rel_n, vis_n = positions(t, past)
    for h in range(H_A):
        sl = slice(h * HE_A, (h + 1) * HE_A)
        bias_c = jnp.where(vis_c, _rel_bias_tile(rb_ref, h, rel_c, -(past + t - 1), -1, 0.0, 1.0), NEG_INF)
        bias_n = jnp.where(vis_n, _rel_bias_tile(rb_ref, h, rel_n, -(t - 1), t - 1, 0.0, 1.0), NEG_INF)
        q = q_ref[:, sl]
        lane = lax.broadcasted_iota(jnp.int32, q.shape, 1)
        zero = jnp.zeros_like(q)
        q2 = jnp.concatenate([jnp.where(lane < HD_A, q, zero), jnp.where(lane >= HD_A, q, zero)], axis=0)
        kc = kc_ref[:, h, :].astype(BF16)
        sc = lax.dot_general(q2, kc, NT_DIMS, preferred_element_type=F32) + jnp.concatenate([bias_c, bias_c], 0)
        sn = (lax.dot_general(q2, kn_ref[:, sl], NT_DIMS, preferred_element_type=F32)
              + jnp.concatenate([bias_n, bias_n], 0))
        m = jnp.maximum(jnp.max(sc, axis=-1, keepdims=True), jnp.max(sn, axis=-1, keepdims=True))
        ec = jnp.exp(sc - m)
        en = jnp.exp(sn - m)
        l = jnp.sum(ec, axis=-1, keepdims=True) + jnp.sum(en, axis=-1, keepdims=True)
        pc = ec / l
        pn = en / l
        ac = pc[0:t] - lam * pc[t:2 * t]
        an = pn[0:t] - lam * pn[t:2 * t]
        o = _dot(ac.astype(BF16), vc_ref[:, h, :].astype(BF16)) + _dot(an.astype(BF16), vn_ref[:, sl])
        o_ref[:, sl] = (_rms(o, subg_ref[...]) * (1.0 - lam_init)).astype(BF16)


def _attn_sample(q, kbf, vbf, cache_k, cache_v, layer, rel_bias, lam_p, sub_g, lam_init, b, t):
    r, d = q.shape
    past = cache_k.shape[2]
    rows = pl.BlockSpec((t, d), lambda i: (i, 0))
    cache = pl.BlockSpec((None, None, past, H_A, HE_A), lambda i: (layer, i, 0, 0, 0))
    return pl.pallas_call(
        functools.partial(_attn_sample_kernel, past=past, lam_init=lam_init),
        grid=(b,),
        in_specs=[pl.BlockSpec(memory_space=pltpu.SMEM),
                  pl.BlockSpec((4, HD_A), lambda i: (0, 0)),
                  rows, cache, cache, rows, rows,
                  pl.BlockSpec((1, HE_A), lambda i: (0, 0))],
        out_specs=rows,
        out_shape=jax.ShapeDtypeStruct((r, d), BF16),
        compiler_params=_cparams(("arbitrary",)),
        name="attn_sample",
    )(rel_bias.astype(F32), lam_p, q, cache_k, cache_v, kbf, vbf, sub_g)


def _mlstm_kernel(q_ref, k_ref, v_ref, om_ref, g_ref, gT_ref, mg_ref, c0_ref, n0_ref, m0_ref,
                  h_ref, c_ref, n_ref, m_ref, c_s, n_s, m_s, *, tl):
    step = pl.program_id(1)

    @pl.when(step == 0)
    def _():
        c_s[...] = c0_ref[...]
        n_s[...] = n0_ref[...]
        m_s[...] = m0_ref[...]

    ti = lax.broadcasted_iota(jnp.int32, (tl, tl), 0)
    si = lax.broadcasted_iota(jnp.int32, (tl, tl), 1)
    causal = si <= ti
    for h in range(H_M):
        sl = slice(h * DH_M, (h + 1) * DH_M)
        q = q_ref[:, sl]
        k = k_ref[:, sl]
        v = v_ref[:, sl]
        ig_c = g_ref[:, h:h + 1]
        lf_c = _log_sigmoid(g_ref[:, H_M + h:H_M + h + 1])
        ig_r = gT_ref[h:h + 1, :]
        lf_r = _log_sigmoid(gT_ref[H_M + h:H_M + h + 1, :])
        m = m_s[h]
        n_row = n_s[h]
        b_c = jnp.sum(jnp.where(causal, lf_r, 0.0), axis=1, keepdims=True)
        b_r = jnp.sum(jnp.where(ti <= si, lf_c, 0.0), axis=0, keepdims=True)
        dmat = jnp.where(causal, b_c - b_r + ig_r, -jnp.inf)
        inter = b_c + m
        m_t = jnp.maximum(inter, jnp.max(dmat, axis=1, keepdims=True))
        w_intra = jnp.exp(dmat - m_t)
        w_inter = jnp.exp(inter - m_t)
        a = w_intra * lax.dot_general(q, k, NT_DIMS, preferred_element_type=F32)
        num = w_inter * _dot(q, c_s[h].astype(BF16)) + _dot(a.astype(BF16), v)
        den = (w_inter * jnp.sum(q.astype(F32) * n_row, axis=1, keepdims=True)
               + jnp.sum(a, axis=1, keepdims=True))
        hh = num / jnp.maximum(jnp.abs(den), jnp.exp(-m_t))
        b_last = b_r[:, tl - 1:tl]
        g_c = b_last - b_c + ig_c
        m_new = jnp.maximum(b_last + m, jnp.max(g_c, axis=0, keepdims=True))
        decay = jnp.exp(b_last + m - m_new)
        kw = jnp.exp(g_c - m_new) * k.astype(F32)
        c_s[h] = decay * c_s[h] + _dot(kw.T.astype(BF16), v)
        n_s[h] = decay * n_row + jnp.sum(kw, axis=0, keepdims=True)
        m_s[h] = m_new
        hn = _rms(hh, mg_ref[:, sl])
        h_ref[:, sl] = (hn * jax.nn.sigmoid(om_ref[:, sl])).astype(BF16)

    @pl.when(step == pl.num_programs(1) - 1)
    def _():
        c_ref[...] = c_s[...]
        n_ref[...] = n_s[...]
        m_ref[...] = m_s[...]


def _mlstm(qm, km, vm, om, gates, m_g, c0, n0, m0, b, t, tl):
    r, d = qm.shape
    nc = t // tl
    gates_t = jnp.transpose(gates.reshape(b * nc, tl, 2 * H_M), (0, 2, 1))
    rows = pl.BlockSpec((tl, d), lambda i, c: (i * nc + c, 0))
    st_c = pl.BlockSpec((None, H_M, DH_M, DH_M), lambda i, c: (i, 0, 0, 0))
    st_n = pl.BlockSpec((None, H_M, 1, DH_M), lambda i, c: (i, 0, 0, 0))
    st_m = pl.BlockSpec((None, H_M, 1, 1), lambda i, c: (i, 0, 0, 0))
    h, c_new, n_new, m_new = pl.pallas_call(
        functools.partial(_mlstm_kernel, tl=tl),
        grid=(b, nc),
        in_specs=[rows, rows, rows, rows,
                  pl.BlockSpec((tl, 2 * H_M), lambda i, c: (i * nc + c, 0)),
                  pl.BlockSpec((None, 2 * H_M, tl), lambda i, c: (i * nc + c, 0, 0)),
                  pl.BlockSpec((1, d), lambda i, c: (0, 0)),
                  st_c, st_n, st_m],
        out_specs=[rows, st_c, st_n, st_m],
        out_shape=[jax.ShapeDtypeStruct((r, d), BF16),
                   jax.ShapeDtypeStruct((b, H_M, DH_M, DH_M), F32),
                   jax.ShapeDtypeStruct((b, H_M, 1, DH_M), F32),
                   jax.ShapeDtypeStruct((b, H_M, 1, 1), F32)],
        scratch_shapes=[pltpu.VMEM((H_M, DH_M, DH_M), F32), pltpu.VMEM((H_M, 1, DH_M), F32),
                        pltpu.VMEM((H_M, 1, 1), F32)],
        compiler_params=_cparams(("arbitrary", "arbitrary")),
        name="mlstm",
    )(qm, km, vm, om, gates, gates_t, m_g, c0, n0.reshape(b, H_M, 1, DH_M), m0.reshape(b, H_M, 1, 1))
    return h, c_new, n_new.reshape(b, H_M, DH_M), m_new.reshape(b, H_M)


def _merge_kernel(oa_ref, hm_ref, gts_ref, x_ref, g1_ref, wa_ref, wm_ref, wo_ref, o_ref, *, d):
    ya = _dot(oa_ref[...], wa_ref[...])
    ym = _dot(hm_ref[...], wm_ref[...])
    mix = jax.nn.sigmoid(gts_ref[:, 0:d]) * ya + jax.nn.sigmoid(gts_ref[:, d:2 * d]) * ym
    o_ref[...] = x_ref[...] + g1_ref[...] * _dot(mix.astype(BF16), wo_ref[...])


def _merge(oa, hm, gts, x, mod, per_row, wa, wm, wo):
    r, d = x.shape
    tm = min(r, ROW_TILE)
    nat = pl.BlockSpec((tm, d), lambda i: (i, 0))
    wspec = pl.BlockSpec((d, d), lambda i: (0, 0))
    return pl.pallas_call(
        functools.partial(_merge_kernel, d=d),
        grid=(r // tm,),
        in_specs=[nat, nat, pl.BlockSpec((tm, 2 * d), lambda i: (i, 0)), nat,
                  _mod_spec(per_row, tm, d, 2, lambda i: i), wspec, wspec, wspec],
        out_specs=nat,
        out_shape=jax.ShapeDtypeStruct((r, d), F32),
        compiler_params=_cparams(("arbitrary",)),
        name="merge_out",
    )(oa, hm, gts, x, mod, wa, wm, wo)


def _ffn_kernel(x_ref, g_ref, sc_ref, sh_ref, g2_ref, wu_ref, wg_ref, cw_ref, cb_ref, wd_ref, buf_ref, fg_ref,
                o_ref, nb_ref, xn_s, acc_s, tail_s, *, tm, final):
    ti = pl.program_id(1)
    c = pl.program_id(2)

    @pl.when(c == 0)
    def _():
        xn_s[...] = (_rms(x_ref[...], g_ref[...]) * (1.0 + sc_ref[...]) + sh_ref[...]).astype(BF16)
        acc_s[...] = jnp.zeros(acc_s.shape, F32)

    xn = xn_s[...]
    u = _dot(xn, wu_ref[...])
    gate = _dot(xn, wg_ref[...])
    first = ti == 0
    prev2 = jnp.where(first, buf_ref[0:1, :], tail_s[c, 6:7, :])
    prev1 = jnp.where(first, buf_ref[1:2, :], tail_s[c, 7:8, :])
    rowi = lax.broadcasted_iota(jnp.int32, u.shape, 0)
    u1 = jnp.where(rowi == 0, prev1, pltpu.roll(u, 1, 0))
    u2 = jnp.where(rowi == 0, prev2, jnp.where(rowi == 1, prev1, pltpu.roll(u, 2, 0)))
    conv = cb_ref[...] + cw_ref[0:1, :] * u2 + cw_ref[1:2, :] * u1 + cw_ref[2:3, :] * u
    f = jax.nn.gelu(conv) * gate
    acc_s[...] += _dot(f.astype(BF16), wd_ref[...])
    tail_s[c] = u[tm - 8:tm, :]
    nb_ref[...] = u[tm - (CONV_W - 1):tm, :]

    @pl.when(c == pl.num_programs(2) - 1)
    def _():
        y = x_ref[...] + g2_ref[...] * acc_s[...]
        if final:
            y = _rms(y, fg_ref[...])
        o_ref[...] = y


def _ffn(x, mod, per_row, g, w_up, conv_w, conv_b, w_down, conv_buf, final_g, b, t, final):
    r, d = x.shape
    dff = w_down.shape[0]
    tm = min(t, ROW_TILE)
    nt = t // tm
    nc = 2
    ck = dff // nc
    assert ck % 128 == 0 and tm >= 8
    row = lambda i, j, c: i * nt + j
    nat = pl.BlockSpec((tm, d), lambda i, j, c: (i * nt + j, 0))
    vec = pl.BlockSpec((1, d), lambda i, j, c: (0, 0))
    y, tails = pl.pallas_call(
        functools.partial(_ffn_kernel, tm=tm, final=final),
        grid=(b, nt, nc),
        in_specs=[nat, vec, _mod_spec(per_row, tm, d, 4, row), _mod_spec(per_row, tm, d, 3, row),
                  _mod_spec(per_row, tm, d, 5, row),
                  pl.BlockSpec((d, ck), lambda i, j, c: (0, c)),
                  pl.BlockSpec((d, ck), lambda i, j, c: (0, nc + c)),
                  pl.BlockSpec((CONV_W, ck), lambda i, j, c: (0, c)),
                  pl.BlockSpec((1, ck), lambda i, j, c: (0, c)),
                  pl.BlockSpec((ck, d), lambda i, j, c: (c, 0)),
                  pl.BlockSpec((None, CONV_W - 1, ck), lambda i, j, c: (i, 0, c)),
                  vec],
        out_specs=[nat, pl.BlockSpec((None, None, CONV_W - 1, ck), lambda i, j, c: (i, j, 0, c))],
        out_shape=[jax.ShapeDtypeStruct((r, d), F32), jax.ShapeDtypeStruct((b, nt, CONV_W - 1, dff), F32)],
        scratch_shapes=[pltpu.VMEM((tm, d), BF16), pltpu.VMEM((tm, d), F32), pltpu.VMEM((nc, 8, ck), F32)],
        compiler_params=_cparams(("arbitrary", "arbitrary", "arbitrary")),
        name="conv_ffn",
    )(x, g, mod, mod, mod, w_up, w_up, conv_w, conv_b, w_down, conv_buf, final_g)
    return y, tails[:, nt - 1]


def kernel(x_prompt, x_sample, c_prompt, c_sample, cache_k, cache_v, state_C, state_n, state_m, state_conv,
           rel_bias, ada_w, ada_b, norm1_g, norm2_g, w_in, b_if, lam_p, attn_sub_g, mlstm_g, w_br_a, w_br_m,
           w_out, w_up, conv_w, conv_b, w_down, final_g):
    bp, s, d = x_prompt.shape
    bs, ts, _ = x_sample.shape
    depth = w_in.shape[0]
    dff = w_down.shape[1]
    assert bp == 1 and d == H_A * HE_A == H_M * DH_M

    c_all = jnp.concatenate([c_prompt, c_sample, jnp.zeros((16 - bp - bs, d), F32)], axis=0)
    mod = _ada(c_all, ada_w, ada_b)

    xp = x_prompt.reshape(bp * s, d)
    xs = x_sample.reshape(bs * ts, d)
    tl_p = min(MLSTM_TILE, s)
    tl_s = min(CHUNK, ts)
    zero_c = jnp.zeros((bp, H_M, DH_M, DH_M), F32)
    zero_n = jnp.zeros((bp, H_M, DH_M), F32)
    zero_m = jnp.zeros((bp, H_M), F32)
    zero_buf = jnp.zeros((bp, CONV_W - 1, dff), F32)
    st_p, st_s = [], []
    for l in range(depth):
        lam_init = 0.8 - 0.6 * math.exp(-0.3 * l)
        last = l == depth - 1
        w_qkv = w_in[l, :, 0:3 * d].astype(BF16)
        w_m = w_in[l, :, 3 * d:9 * d].astype(BF16)
        w_if = jnp.pad(w_in[l, :, 9 * d:], ((0, 0), (0, 128 - 2 * H_M))).astype(BF16)
        bif = jnp.pad(b_if[l], (0, 128 - 2 * H_M)).reshape(1, 128)
        wa, wm, wo = w_br_a[l].astype(BF16), w_br_m[l].astype(BF16), w_out[l].astype(BF16)
        wu, wd = w_up[l].astype(BF16), w_down[l].astype(BF16)
        g1, g2 = norm1_g[l].reshape(1, d), norm2_g[l].reshape(1, d)
        sub_g, m_g = attn_sub_g[l].reshape(1, HE_A), mlstm_g[l].reshape(1, d)
        cb = conv_b[l].reshape(1, dff)
        fg = final_g.reshape(1, d)

        mod_p = mod[l, 0:bp]
        qT, k32, kbf, v32, vT = _in_attn(xp, mod_p, False, g1, w_qkv, True)
        qm, km, vm, om, gts, gates = _in_mlstm(xp, mod_p, False, g1, w_m, w_if, bif)
        oa = _attn_prompt(qT, kbf, vT, rel_bias, lam_p[l], sub_g, lam_init, min(ATTN_TILE, s))
        hm, c_new, n_new, m_new = _mlstm(qm, km, vm, om, gates, m_g, zero_c, zero_n, zero_m, bp, s, tl_p)
        xp = _merge(oa, hm, gts, xp, mod_p, False, wa, wm, wo)
        xp, buf_new = _ffn(xp, mod_p, False, g2, wu, conv_w[l], cb, wd, zero_buf, fg, bp, s, last)
        st_p.append((k32.reshape(bp, s, H_A, HE_A), v32.reshape(bp, s, H_A, HE_A), c_new, n_new, m_new, buf_new))

        mod_s = jnp.repeat(mod[l, bp:bp + bs], ts, axis=0)
        q, k32, kbf, v32, vbf = _in_attn(xs, mod_s, True, g1, w_qkv, False)
        qm, km, vm, om, gts, gates = _in_mlstm(xs, mod_s, True, g1, w_m, w_if, bif)
        oa = _attn_sample(q, kbf, vbf, cache_k, cache_v, l, rel_bias, lam_p[l], sub_g, lam_init, bs, ts)
        hm, c_new, n_new, m_new = _mlstm(qm, km, vm, om, gates, m_g, state_C[l], state_n[l], state_m[l],
                                         bs, ts, tl_s)
        xs = _merge(oa, hm, gts, xs, mod_s, True, wa, wm, wo)
        xs, buf_new = _ffn(xs, mod_s, True, g2, wu, conv_w[l], cb, wd, state_conv[l], fg, bs, ts, last)
        st_s.append((k32.reshape(bs, ts, H_A, HE_A), v32.reshape(bs, ts, H_A, HE_A), c_new, n_new, m_new, buf_new))

    outs_p = [jnp.stack(a) for a in zip(*st_p)]
    outs_s = [jnp.stack(a) for a in zip(*st_s)]
    return (xp.reshape(bp, s, d), xs.reshape(bs, ts, d), *outs_p, *outs_s)
```

```python
import functools
import math

import numpy as np
import jax
import jax.numpy as jnp
from jax import lax
from jax.experimental import pallas as pl
from jax.experimental.pallas import tpu as pltpu

F32 = jnp.float32
BF16 = jnp.bfloat16

CHUNK = 64
H_A = 8
HD_A = 64
HE_A = 2 * HD_A
H_M = 4
DH_M = 256
N_BUCKETS = 32
CONV_W = 3
NEG_INF = -1e30
EPS = 1e-6
FAR_BUCKET = N_BUCKETS // 2 - 1
LOG2E = math.log2(math.e)

V7X_VMEM_LIMIT = 56 * 1024 * 1024
ROW_TILE = 512
ATTN_TILE = 512
ATTN_FAR_BLOCKS = 4
BF16_ROWS = 16
MLSTM_TILE = 256

NT_DIMS = (((1,), (1,)), ((), ()))


def _cparams(sem):
    return pltpu.CompilerParams(dimension_semantics=sem, vmem_limit_bytes=V7X_VMEM_LIMIT)


def _dot(a, b):
    return jnp.dot(a, b, preferred_element_type=F32)


def _rms(x, g):
    return x * lax.rsqrt(jnp.mean(x * x, axis=-1, keepdims=True) + EPS) * g


def _log_sigmoid(x):
    return jnp.minimum(x, 0.0) - jnp.log1p(jnp.exp(-jnp.abs(x)))


def _mod_spec(per_row, tm, d, col, row_index):
    if per_row:
        return pl.BlockSpec((tm, d), lambda *g: (row_index(*g), col))
    return pl.BlockSpec((1, d), lambda *g: (0, col))


def _ada_kernel(c_ref, w_ref, b_ref, o_ref):
    a = jax.nn.silu(c_ref[...]).astype(BF16)
    o_ref[...] = _dot(a, w_ref[...].astype(BF16)) + b_ref[...]


def _ada(c_all, ada_w, ada_b):
    depth, d, n = ada_w.shape
    rc = c_all.shape[0]
    tn = 1536
    return pl.pallas_call(
        _ada_kernel,
        grid=(depth, n // tn),
        in_specs=[pl.BlockSpec((rc, d), lambda l, j: (0, 0)),
                  pl.BlockSpec((None, d, tn), lambda l, j: (l, 0, j)),
                  pl.BlockSpec((None, 1, tn), lambda l, j: (l, 0, j))],
        out_specs=pl.BlockSpec((None, rc, tn), lambda l, j: (l, 0, j)),
        out_shape=jax.ShapeDtypeStruct((depth, rc, n), F32),
        compiler_params=_cparams(("arbitrary", "arbitrary")),
        name="ada_mod",
    )(c_all, ada_w, ada_b.reshape(depth, 1, n))


def _in_attn_kernel(x_ref, g_ref, sc_ref, sh_ref, w_ref, *refs, d, transposed):
    q_ref, k32_ref, kbf_ref, v32_ref, vbf_ref = refs[-5:]
    xn = (_rms(x_ref[...], g_ref[...]) * (1.0 + sc_ref[...]) + sh_ref[...]).astype(BF16)
    q = _dot(xn, w_ref[:, 0:d]) * (HD_A ** -0.5 * (LOG2E if transposed else 1.0))
    k = _dot(xn, w_ref[:, d:2 * d])
    v = _dot(xn, w_ref[:, 2 * d:3 * d])
    k32_ref[...] = k
    kbf_ref[...] = k.astype(BF16)
    v32_ref[...] = v
    if transposed:
        q_ref[...] = q.T.astype(BF16)
        vbf_ref[...] = v.T.astype(BF16)
    else:
        q_ref[...] = q.astype(BF16)
        vbf_ref[...] = v.astype(BF16)


def _in_attn(x, mod, per_row, g, w_qkv, transposed, layer, depth, kv_states):
    r, d = x.shape
    tm = min(r, ROW_TILE)
    row = lambda i: i
    nat = pl.BlockSpec((tm, d), lambda i: (i, 0))
    tr = pl.BlockSpec((d, tm), lambda i: (0, i))
    state = pl.BlockSpec((None, tm, d), lambda i: (layer, i, 0))
    state_shape = jax.ShapeDtypeStruct((depth, r, d), F32)
    bf_shape = jax.ShapeDtypeStruct((d, r) if transposed else (r, d), BF16)
    prev = () if kv_states is None else tuple(kv_states)
    return pl.pallas_call(
        functools.partial(_in_attn_kernel, d=d, transposed=transposed),
        grid=(r // tm,),
        in_specs=[nat, pl.BlockSpec((1, d), lambda i: (0, 0)),
                  _mod_spec(per_row, tm, d, 1, row), _mod_spec(per_row, tm, d, 0, row),
                  pl.BlockSpec((d, 3 * d), lambda i: (0, 0))] + [pl.BlockSpec(memory_space=pl.ANY)] * len(prev),
        out_specs=[tr if transposed else nat, state, nat, state, tr if transposed else nat],
        out_shape=[bf_shape, state_shape, jax.ShapeDtypeStruct((r, d), BF16), state_shape, bf_shape],
        input_output_aliases={5: 1, 6: 3} if prev else {},
        compiler_params=_cparams(("arbitrary",)),
        name="in_attn",
    )(x, g, mod, mod, w_qkv, *prev)


def _in_mlstm_kernel(x_ref, g_ref, sc_ref, sh_ref, w_ref, wif_ref, bif_ref,
                     qm_ref, km_ref, vm_ref, om_ref, gts_ref, gate_ref, *, d):
    xn = (_rms(x_ref[...], g_ref[...]) * (1.0 + sc_ref[...]) + sh_ref[...]).astype(BF16)
    qm_ref[...] = _dot(xn, w_ref[:, 0:d]).astype(BF16)
    km_ref[...] = (_dot(xn, w_ref[:, d:2 * d]) * (DH_M ** -0.5)).astype(BF16)
    vm_ref[...] = _dot(xn, w_ref[:, 2 * d:3 * d]).astype(BF16)
    om_ref[...] = _dot(xn, w_ref[:, 3 * d:4 * d])
    gts_ref[...] = _dot(xn, w_ref[:, 4 * d:6 * d])
    pre = _dot(xn, wif_ref[...]) + bif_ref[...]
    gate_ref[...] = pre[:, 0:2 * H_M]


def _in_mlstm(x, mod, per_row, g, w_m, w_if, b_if):
    r, d = x.shape
    tm = min(r, ROW_TILE // 2)
    row = lambda i: i
    nat = pl.BlockSpec((tm, d), lambda i: (i, 0))
    const = lambda shape: pl.BlockSpec(shape, lambda i: (0, 0))
    return pl.pallas_call(
        functools.partial(_in_mlstm_kernel, d=d),
        grid=(r // tm,),
        in_specs=[nat, const((1, d)), _mod_spec(per_row, tm, d, 1, row), _mod_spec(per_row, tm, d, 0, row),
                  const(w_m.shape), const(w_if.shape), const(b_if.shape)],
        out_specs=[nat, nat, nat, nat, pl.BlockSpec((tm, 2 * d), lambda i: (i, 0)),
                   pl.BlockSpec((tm, 2 * H_M), lambda i: (i, 0))],
        out_shape=[jax.ShapeDtypeStruct((r, d), BF16)] * 3
        + [jax.ShapeDtypeStruct((r, d), F32), jax.ShapeDtypeStruct((r, 2 * d), F32),
           jax.ShapeDtypeStruct((r, 2 * H_M), F32)],
        compiler_params=_cparams(("arbitrary",)),
        name="in_mlstm",
    )(x, g, mod, mod, w_m, w_if, b_if)


def _bucket_np(rel):
    half = N_BUCKETS // 2
    max_exact = half // 2
    n = np.abs(rel)
    thresholds = np.ceil(max_exact * 2.0 ** (np.arange(1, half - max_exact) / 2.0)).astype(np.int64)
    large = max_exact + (n[..., None] >= thresholds).sum(-1)
    return np.where(rel > 0, half, 0) + np.where(n < max_exact, n, np.minimum(large, half - 1))


def _lam(lp):
    s1 = jnp.sum(lp[0:1] * lp[1:2], axis=-1, keepdims=True)
    s2 = jnp.sum(lp[2:3] * lp[3:4], axis=-1, keepdims=True)
    return jnp.exp(s1) - jnp.exp(s2)


def _bias_steps(lo, hi):
    rel = np.arange(lo, hi + 1)
    b = _bucket_np(rel)
    starts = np.concatenate([[0], np.nonzero(b[1:] != b[:-1])[0] + 1])
    return [(int(rel[k]), int(b[k])) for k in starts]


def _rel_bias_tile(rb_ref, h, rel, lo, hi, shift, scale):
    steps = _bias_steps(lo, hi)
    v = jnp.full(rel.shape, (rb_ref[steps[0][1], h] - shift) * scale, F32)
    for start, bucket in steps[1:]:
        v = jnp.where(rel >= start, (rb_ref[bucket, h] - shift) * scale, v)
    return v


def _attn_prompt_kernel(rb_ref, lamp_ref, qT_ref, k_ref, vT_ref, subg_ref, o_ref,
                        q_s, bias_s, m_s, acc_s, *, t, far_blocks, lam_init):
    h = pl.program_id(0)
    i = pl.program_id(1)
    rows = 64
    chunk_shift = CHUNK.bit_length() - 1

    @pl.when(i == 0)
    def _():
        far = rb_ref[FAR_BUCKET, h]
        for tile, shift in enumerate((t, 0)):

            def fill(r, carry, tile=tile, shift=shift):
                r0 = pl.multiple_of(r * rows, rows)
                kk = lax.broadcasted_iota(jnp.int32, (rows, t), 0) + r0
                qq = lax.broadcasted_iota(jnp.int32, (rows, t), 1)
                v = _rel_bias_tile(rb_ref, h, kk - qq - shift, -shift - (t - 1), -shift + (t - 1), far, LOG2E)
                if tile == 1:
                    visible = jnp.right_shift(kk, chunk_shift) <= jnp.right_shift(qq, chunk_shift)
                    v = jnp.where(visible, v, NEG_INF)
                bias_s[tile, pl.ds(r0, rows), :] = v
                return carry

            lax.fori_loop(0, t // rows, fill, 0)

    lam = _lam(lamp_ref[...]) + lam_init
    qT = qT_ref[...]
    rowi = lax.broadcasted_iota(jnp.int32, qT.shape, 0)
    zero = jnp.zeros_like(qT)
    q_s[0] = jnp.where(rowi < HD_A, qT, zero)
    q_s[1] = jnp.where(rowi >= HD_A, qT, zero)

    def keys(j0, n):
        off = pl.multiple_of(j0 * t, t)
        tail = (lax.broadcasted_iota(jnp.int32, (BF16_ROWS, n), 0) == 0).astype(BF16)
        return k_ref[pl.ds(off, n), :], jnp.concatenate([vT_ref[:, pl.ds(off, n)], tail], axis=0)

    def scores(kb, c, tile):
        s = _dot(kb, q_s[c])
        return s if tile is None else s + bias_s[tile]

    def fixed_shift_block(j0, n, tile):
        kb, vb = keys(j0, n)
        for c in range(2):
            e = jnp.exp2(scores(kb, c, tile) - m_s[c])
            acc_s[c] += _dot(vb, e.astype(BF16))

    def running_max_block(j0, tile):
        kb, vb = keys(j0, t)
        for c in range(2):
            s = scores(kb, c, tile)
            m_old = m_s[c]
            m_new = jnp.maximum(m_old, jnp.max(s, axis=0, keepdims=True))
            acc_s[c] = jnp.exp2(m_old - m_new) * acc_s[c] + _dot(vb, jnp.exp2(s - m_new).astype(BF16))
            m_s[c] = m_new

    def finish():
        a0, a1 = acc_s[0], acc_s[1]
        l0, l1 = a0[HE_A:HE_A + 1], a1[HE_A:HE_A + 1]
        oT = a0[0:HE_A] * (1.0 / l0) - a1[0:HE_A] * (lam / l1)
        inv = lax.rsqrt(jnp.mean(oT * oT, axis=0, keepdims=True) + EPS)
        o_ref[...] = ((oT * inv).T * (subg_ref[...] * (1.0 - lam_init))).astype(BF16)
        big = float(np.finfo(np.float32).max)
        bad = jnp.max(jnp.where(jnp.abs(oT) <= big, 0.0, 1.0), axis=0, keepdims=True)
        return jnp.max(jnp.maximum(bad, jnp.where((l0 <= big) & (l1 <= big), 0.0, 1.0)))

    def first_step(r):
        nb = max(r, 0) + (2 if r >= 0 else 1)
        kb, vb = keys(i - (nb - 1), nb * t)
        for c in range(2):
            s = _dot(kb, q_s[c])
            parts = [s[0:r * t]] if r > 0 else []
            if r >= 0:
                parts.append(s[(nb - 2) * t:(nb - 1) * t] + bias_s[0])
            parts.append(s[(nb - 1) * t:nb * t] + bias_s[1])
            e = [jnp.exp2(p - m_s[c]).astype(BF16) for p in parts]
            acc_s[c] = _dot(vb, e[0] if len(e) == 1 else jnp.concatenate(e, axis=0))

    k0 = k_ref[pl.ds(pl.multiple_of(i * t, t), CHUNK), :]
    for c in range(2):
        m_s[c] = jnp.max(_dot(k0, q_s[c]) + bias_s[1, 0:CHUNK, :], axis=0, keepdims=True)

    nfar = jnp.maximum(i - 1, 0)
    folded = jnp.bitwise_and(nfar, far_blocks - 1)

    @pl.when(i == 0)
    def _():
        first_step(-1)

    for r in range(far_blocks):
        @pl.when((i >= 1) & (folded == r))
        def _(r=r):
            first_step(r)

    def big_step(p, carry):
        fixed_shift_block(p * far_blocks, far_blocks * t, None)
        return carry

    lax.fori_loop(0, lax.div(nfar, far_blocks), big_step, 0)

    @pl.when(finish() > 0.0)
    def _():
        m_s[...] = jnp.full(m_s.shape, NEG_INF, F32)
        acc_s[...] = jnp.zeros(acc_s.shape, F32)

        def step(j, carry):
            running_max_block(j, None)
            return carry

        lax.fori_loop(0, nfar, step, 0)

        @pl.when(i >= 1)
        def _():
            running_max_block(i - 1, 0)

        running_max_block(i, 1)
        finish()


def _attn_prompt(qT, kbf, vT, rel_bias, lam_p, sub_g, lam_init, t):
    d, s = qT.shape
    assert s % t == 0 and t % 64 == 0 and t % CHUNK == 0 and CHUNK & (CHUNK - 1) == 0
    assert _bucket_np(np.array([-(t + 1)]))[0] == FAR_BUCKET
    assert ATTN_FAR_BLOCKS & (ATTN_FAR_BLOCKS - 1) == 0
    nq = s // t
    return pl.pallas_call(
        functools.partial(_attn_prompt_kernel, t=t, far_blocks=ATTN_FAR_BLOCKS, lam_init=lam_init),
        grid=(H_A, nq),
        in_specs=[pl.BlockSpec(memory_space=pltpu.SMEM),
                  pl.BlockSpec((4, HD_A), lambda h, i: (0, 0)),
                  pl.BlockSpec((HE_A, t), lambda h, i: (h, i)),
                  pl.BlockSpec((s, HE_A), lambda h, i: (0, h)),
                  pl.BlockSpec((HE_A, s), lambda h, i: (h, 0)),
                  pl.BlockSpec((1, HE_A), lambda h, i: (0, 0))],
        out_specs=pl.BlockSpec((t, HE_A), lambda h, i: (i, h)),
        out_shape=jax.ShapeDtypeStruct((s, d), BF16),
        scratch_shapes=[pltpu.VMEM((2, HE_A, t), BF16), pltpu.VMEM((2, t, t), F32),
                        pltpu.VMEM((2, 1, t), F32), pltpu.VMEM((2, HE_A + BF16_ROWS, t), F32)],
        compiler_params=_cparams(("arbitrary", "arbitrary")),
        name="attn_prompt",
    )(rel_bias.astype(F32), lam_p, qT, kbf, vT, sub_g)


def _attn_sample_kernel(rb_ref, lamp_ref, q_ref, kc_ref, vc_ref, kn_ref, vn_ref, subg_ref, o_ref,
                        *, past, lam_init):
    t = q_ref.shape[0]
    chunk_shift = CHUNK.bit_length() - 1
    lam = _lam(lamp_ref[...]) + lam_init

    def positions(n_keys, first_key):
        qpos = past + lax.broadcasted_iota(jnp.int32, (t, n_keys), 0)
        kpos = first_key + lax.broadcasted_iota(jnp.int32, (t, n_keys), 1)
        return kpos - qpos, jnp.right_shift(kpos, chunk_shift) <= jnp.right_shift(qpos, chunk_shift)

    rel_c, vis_c = positions(past, 0)
    rel_n, vis_n = positions(t, past)
    for h in range(H_A):
        sl = slice(h * HE_A, (h + 1) * HE_A)
        bias_c = jnp.where(vis_c, _rel_bias_tile(rb_ref, h, rel_c, -(past + t - 1), -1, 0.0, 1.0), NEG_INF)
        bias_n = jnp.where(vis_n, _rel_bias_tile(rb_ref, h, rel_n, -(t - 1), t - 1, 0.0, 1.0), NEG_INF)
        q = q_ref[:, sl]
        lane = lax.broadcasted_iota(jnp.int32, q.shape, 1)
        zero = jnp.zeros_like(q)
        q2 = jnp.concatenate([jnp.where(lane < HD_A, q, zero), jnp.where(lane >= HD_A, q, zero)], axis=0)
        kc = kc_ref[pl.ds(h, past, stride=H_A), :].astype(BF16)
        sc = lax.dot_general(q2, kc, NT_DIMS, preferred_element_type=F32) + jnp.concatenate([bias_c, bias_c], 0)
        sn = (lax.dot_general(q2, kn_ref[:, sl], NT_DIMS, preferred_element_type=F32)
              + jnp.concatenate([bias_n, bias_n], 0))
        m = jnp.maximum(jnp.max(sc, axis=-1, keepdims=True), jnp.max(sn, axis=-1, keepdims=True))
        ec = jnp.exp(sc - m)
        en = jnp.exp(sn - m)
        l = jnp.sum(ec, axis=-1, keepdims=True) + jnp.sum(en, axis=-1, keepdims=True)
        pc = ec / l
        pn = en / l
        ac = pc[0:t] - lam * pc[t:2 * t]
        an = pn[0:t] - lam * pn[t:2 * t]
        vc = vc_ref[pl.ds(h, past, stride=H_A), :].astype(BF16)
        o = _dot(ac.astype(BF16), vc) + _dot(an.astype(BF16), vn_ref[:, sl])
        o_ref[:, sl] = (_rms(o, subg_ref[...]) * (1.0 - lam_init)).astype(BF16)


def _attn_sample(q, kbf, vbf, cache_k, cache_v, layer, rel_bias, lam_p, sub_g, lam_init, b, t):
    r, d = q.shape
    past = cache_k.shape[2]
    rows = pl.BlockSpec((t, d), lambda i: (i, 0))
    cache_k = cache_k.reshape(-1, HE_A)
    cache_v = cache_v.reshape(-1, HE_A)
    cache = pl.BlockSpec((past * H_A, HE_A), lambda i: (layer * b + i, 0))
    return pl.pallas_call(
        functools.partial(_attn_sample_kernel, past=past, lam_init=lam_init),
        grid=(b,),
        in_specs=[pl.BlockSpec(memory_space=pltpu.SMEM),
                  pl.BlockSpec((4, HD_A), lambda i: (0, 0)),
                  rows, cache, cache, rows, rows,
                  pl.BlockSpec((1, HE_A), lambda i: (0, 0))],
        out_specs=rows,
        out_shape=jax.ShapeDtypeStruct((r, d), BF16),
        compiler_params=_cparams(("arbitrary",)),
        name="attn_sample",
    )(rel_bias.astype(F32), lam_p, q, cache_k, cache_v, kbf, vbf, sub_g)


def _mlstm_kernel(q_ref, k_ref, v_ref, om_ref, g_ref, gT_ref, mg_ref, c0_ref, n0_ref, m0_ref, *refs, tl):
    h_ref, c_ref, n_ref, m_ref, c_s, n_s, m_s = refs[-7:]
    step = pl.program_id(1)

    @pl.when(step == 0)
    def _():
        c_s[...] = c0_ref[...]
        n_s[...] = n0_ref[...]
        m_s[...] = m0_ref[...]

    ti = lax.broadcasted_iota(jnp.int32, (tl, tl), 0)
    si = lax.broadcasted_iota(jnp.int32, (tl, tl), 1)
    causal = si <= ti
    gates_c = g_ref[...]
    gates_r = gT_ref[...]
    lf_cols = _log_sigmoid(gates_c)
    lf_rows = _log_sigmoid(gates_r)
    for h in range(H_M):
        sl = slice(h * DH_M, (h + 1) * DH_M)
        q = q_ref[:, sl]
        k = k_ref[:, sl]
        v = v_ref[:, sl]
        ig_c = gates_c[:, h:h + 1]
        lf_c = lf_cols[:, H_M + h:H_M + h + 1]
        ig_r = gates_r[h:h + 1, :]
        lf_r = lf_rows[H_M + h:H_M + h + 1, :]
        m = m_s[h]
        n_row = n_s[h]
        b_c = jnp.sum(jnp.where(causal, lf_r, 0.0), axis=1, keepdims=True)
        b_r = jnp.sum(jnp.where(ti <= si, lf_c, 0.0), axis=0, keepdims=True)
        dmat = jnp.where(causal, b_c - b_r + ig_r, -jnp.inf)
        inter = b_c + m
        m_t = jnp.maximum(inter, jnp.max(dmat, axis=1, keepdims=True))
        w_intra = jnp.exp(dmat - m_t)
        w_inter = jnp.exp(inter - m_t)
        a = w_intra * lax.dot_general(q, k, NT_DIMS, preferred_element_type=F32)
        num = w_inter * _dot(q, c_s[h].astype(BF16)) + _dot(a.astype(BF16), v)
        den = (w_inter * jnp.sum(q.astype(F32) * n_row, axis=1, keepdims=True)
               + jnp.sum(a, axis=1, keepdims=True))
        hh = num / jnp.maximum(jnp.abs(den), jnp.exp(-m_t))
        b_last = b_r[:, tl - 1:tl]
        g_c = b_last - b_c + ig_c
        m_new = jnp.maximum(b_last + m, jnp.max(g_c, axis=0, keepdims=True))
        decay = jnp.exp(b_last + m - m_new)
        kw = jnp.exp(g_c - m_new) * k.astype(F32)
        c_s[h] = decay * c_s[h] + _dot(kw.T.astype(BF16), v)
        n_s[h] = decay * n_row + jnp.sum(kw, axis=0, keepdims=True)
        m_s[h] = m_new
        hn = _rms(hh, mg_ref[:, sl])
        h_ref[:, sl] = (hn * jax.nn.sigmoid(om_ref[:, sl])).astype(BF16)

    @pl.when(step == pl.num_programs(1) - 1)
    def _():
        c_ref[...] = c_s[...]
        n_ref[...] = n_s[...]
        m_ref[...] = m_s[...]


def _mlstm(qm, km, vm, om, gates, m_g, c0, c0_layer, n0, m0, b, t, tl, layer, depth, c_states):
    r, d = qm.shape
    nc = t // tl
    gates_t = jnp.transpose(gates.reshape(b * nc, tl, 2 * H_M), (0, 2, 1))
    rows = pl.BlockSpec((tl, d), lambda i, c: (i * nc + c, 0))
    st_n = pl.BlockSpec((None, H_M, 1, DH_M), lambda i, c: (i, 0, 0, 0))
    st_m = pl.BlockSpec((None, H_M, 1, 1), lambda i, c: (i, 0, 0, 0))
    prev = () if c_states is None else (c_states,)
    h, c_states, n_new, m_new = pl.pallas_call(
        functools.partial(_mlstm_kernel, tl=tl),
        grid=(b, nc),
        in_specs=[rows, rows, rows, rows,
                  pl.BlockSpec((tl, 2 * H_M), lambda i, c: (i * nc + c, 0)),
                  pl.BlockSpec((None, 2 * H_M, tl), lambda i, c: (i * nc + c, 0, 0)),
                  pl.BlockSpec((1, d), lambda i, c: (0, 0)),
                  pl.BlockSpec((None, None, H_M, DH_M, DH_M), lambda i, c: (c0_layer, i, 0, 0, 0)),
                  st_n, st_m] + [pl.BlockSpec(memory_space=pl.ANY)] * len(prev),
        out_specs=[rows, pl.BlockSpec((None, None, H_M, DH_M, DH_M), lambda i, c: (layer, i, 0, 0, 0)),
                   st_n, st_m],
        out_shape=[jax.ShapeDtypeStruct((r, d), BF16),
                   jax.ShapeDtypeStruct((depth, b, H_M, DH_M, DH_M), F32),
                   jax.ShapeDtypeStruct((b, H_M, 1, DH_M), F32),
                   jax.ShapeDtypeStruct((b, H_M, 1, 1), F32)],
        input_output_aliases={10: 1} if prev else {},
        scratch_shapes=[pltpu.VMEM((H_M, DH_M, DH_M), F32), pltpu.VMEM((H_M, 1, DH_M), F32),
                        pltpu.VMEM((H_M, 1, 1), F32)],
        compiler_params=_cparams(("arbitrary", "arbitrary")),
        name="mlstm",
    )(qm, km, vm, om, gates, gates_t, m_g, c0, n0.reshape(b, H_M, 1, DH_M), m0.reshape(b, H_M, 1, 1), *prev)
    return h, c_states, n_new.reshape(b, H_M, DH_M), m_new.reshape(b, H_M)


def _merge_kernel(oa_ref, hm_ref, gts_ref, x_ref, g1_ref, wa_ref, wm_ref, wo_ref, o_ref, *, d):
    ya = _dot(oa_ref[...], wa_ref[...])
    ym = _dot(hm_ref[...], wm_ref[...])
    mix = jax.nn.sigmoid(gts_ref[:, 0:d]) * ya + jax.nn.sigmoid(gts_ref[:, d:2 * d]) * ym
    o_ref[...] = x_ref[...] + g1_ref[...] * _dot(mix.astype(BF16), wo_ref[...])


def _merge(oa, hm, gts, x, mod, per_row, wa, wm, wo):
    r, d = x.shape
    tm = min(r, ROW_TILE)
    nat = pl.BlockSpec((tm, d), lambda i: (i, 0))
    wspec = pl.BlockSpec((d, d), lambda i: (0, 0))
    return pl.pallas_call(
        functools.partial(_merge_kernel, d=d),
        grid=(r // tm,),
        in_specs=[nat, nat, pl.BlockSpec((tm, 2 * d), lambda i: (i, 0)), nat,
                  _mod_spec(per_row, tm, d, 2, lambda i: i), wspec, wspec, wspec],
        out_specs=nat,
        out_shape=jax.ShapeDtypeStruct((r, d), F32),
        compiler_params=_cparams(("arbitrary",)),
        name="merge_out",
    )(oa, hm, gts, x, mod, wa, wm, wo)


def _ffn_kernel(x_ref, g_ref, sc_ref, sh_ref, g2_ref, wu_ref, wg_ref, cw_ref, cb_ref, wd_ref, buf_ref, fg_ref,
                o_ref, nb_ref, xn_s, acc_s, tail_s, *, tm, final):
    ti = pl.program_id(1)
    c = pl.program_id(2)

    @pl.when(c == 0)
    def _():
        xn_s[...] = (_rms(x_ref[...], g_ref[...]) * (1.0 + sc_ref[...]) + sh_ref[...]).astype(BF16)
        acc_s[...] = jnp.zeros(acc_s.shape, F32)

    xn = xn_s[...]
    u = _dot(xn, wu_ref[...])
    gate = _dot(xn, wg_ref[...])
    first = ti == 0
    prev2 = jnp.where(first, buf_ref[0:1, :], tail_s[c, 6:7, :])
    prev1 = jnp.where(first, buf_ref[1:2, :], tail_s[c, 7:8, :])
    rowi = lax.broadcasted_iota(jnp.int32, u.shape, 0)
    u1 = jnp.where(rowi == 0, prev1, pltpu.roll(u, 1, 0))
    u2 = jnp.where(rowi == 0, prev2, jnp.where(rowi == 1, prev1, pltpu.roll(u, 2, 0)))
    conv = cb_ref[...] + cw_ref[0:1, :] * u2 + cw_ref[1:2, :] * u1 + cw_ref[2:3, :] * u
    f = jax.nn.gelu(conv) * gate
    acc_s[...] += _dot(f.astype(BF16), wd_ref[...])
    tail_s[c] = u[tm - 8:tm, :]
    nb_ref[...] = u[tm - (CONV_W - 1):tm, :]

    @pl.when(c == pl.num_programs(2) - 1)
    def _():
        y = x_ref[...] + g2_ref[...] * acc_s[...]
        if final:
            y = _rms(y, fg_ref[...])
        o_ref[...] = y


def _ffn(x, mod, per_row, g, w_up, conv_w, conv_b, w_down, conv_buf, final_g, b, t, final):
    r, d = x.shape
    dff = w_down.shape[0]
    tm = min(t, ROW_TILE)
    nt = t // tm
    nc = 2 if tm > 64 else 1
    ck = dff // nc
    assert ck % 128 == 0 and tm >= 8
    row = lambda i, j, c: i * nt + j
    nat = pl.BlockSpec((tm, d), lambda i, j, c: (i * nt + j, 0))
    vec = pl.BlockSpec((1, d), lambda i, j, c: (0, 0))
    y, tails = pl.pallas_call(
        functools.partial(_ffn_kernel, tm=tm, final=final),
        grid=(b, nt, nc),
        in_specs=[nat, vec, _mod_spec(per_row, tm, d, 4, row), _mod_spec(per_row, tm, d, 3, row),
                  _mod_spec(per_row, tm, d, 5, row),
                  pl.BlockSpec((d, ck), lambda i, j, c: (0, c)),
                  pl.BlockSpec((d, ck), lambda i, j, c: (0, nc + c)),
                  pl.BlockSpec((CONV_W, ck), lambda i, j, c: (0, c)),
                  pl.BlockSpec((1, ck), lambda i, j, c: (0, c)),
                  pl.BlockSpec((ck, d), lambda i, j, c: (c, 0)),
                  pl.BlockSpec((None, CONV_W - 1, ck), lambda i, j, c: (i, 0, c)),
                  vec],
        out_specs=[nat, pl.BlockSpec((None, None, CONV_W - 1, ck), lambda i, j, c: (i, j, 0, c))],
        out_shape=[jax.ShapeDtypeStruct((r, d), F32), jax.ShapeDtypeStruct((b, nt, CONV_W - 1, dff), F32)],
        scratch_shapes=[pltpu.VMEM((tm, d), BF16), pltpu.VMEM((tm, d), F32), pltpu.VMEM((nc, 8, ck), F32)],
        compiler_params=_cparams(("arbitrary", "arbitrary", "arbitrary")),
        name="conv_ffn",
    )(x, g, mod, mod, mod, w_up, w_up, conv_w, conv_b, w_down, conv_buf, final_g)
    return y, tails[:, nt - 1]


def kernel(x_prompt, x_sample, c_prompt, c_sample, cache_k, cache_v, state_C, state_n, state_m, state_conv,
           rel_bias, ada_w, ada_b, norm1_g, norm2_g, w_in, b_if, lam_p, attn_sub_g, mlstm_g, w_br_a, w_br_m,
           w_out, w_up, conv_w, conv_b, w_down, final_g):
    bp, s, d = x_prompt.shape
    bs, ts, _ = x_sample.shape
    depth = w_in.shape[0]
    dff = w_down.shape[1]
    assert bp == 1 and d == H_A * HE_A == H_M * DH_M

    c_all = jnp.concatenate([c_prompt, c_sample, jnp.zeros((16 - bp - bs, d), F32)], axis=0)
    mod = _ada(c_all, ada_w, ada_b)

    xp = x_prompt.reshape(bp * s, d)
    xs = x_sample.reshape(bs * ts, d)
    tl_p = min(MLSTM_TILE, s)
    tl_s = min(CHUNK, ts)
    zero_c = jnp.zeros((1, bp, H_M, DH_M, DH_M), F32)
    zero_n = jnp.zeros((bp, H_M, DH_M), F32)
    zero_m = jnp.zeros((bp, H_M), F32)
    zero_buf = jnp.zeros((bp, CONV_W - 1, dff), F32)
    st_p, st_s = [], []
    kv_p = kv_s = c_p = c_s = None
    for l in range(depth):
        lam_init = 0.8 - 0.6 * math.exp(-0.3 * l)
        last = l == depth - 1
        w_qkv = w_in[l, :, 0:3 * d].astype(BF16)
        w_m = w_in[l, :, 3 * d:9 * d].astype(BF16)
        w_if = jnp.pad(w_in[l, :, 9 * d:], ((0, 0), (0, 128 - 2 * H_M))).astype(BF16)
        bif = jnp.pad(b_if[l], (0, 128 - 2 * H_M)).reshape(1, 128)
        wa, wm, wo = w_br_a[l].astype(BF16), w_br_m[l].astype(BF16), w_out[l].astype(BF16)
        wu, wd = w_up[l].astype(BF16), w_down[l].astype(BF16)
        g1, g2 = norm1_g[l].reshape(1, d), norm2_g[l].reshape(1, d)
        sub_g, m_g = attn_sub_g[l].reshape(1, HE_A), mlstm_g[l].reshape(1, d)
        cb = conv_b[l].reshape(1, dff)
        fg = final_g.reshape(1, d)

        mod_p = mod[l, 0:bp]
        qT, k_all, kbf, v_all, vT = _in_attn(xp, mod_p, False, g1, w_qkv, True, l, depth, kv_p)
        kv_p = (k_all, v_all)
        qm, km, vm, om, gts, gates = _in_mlstm(xp, mod_p, False, g1, w_m, w_if, bif)
        oa = _attn_prompt(qT, kbf, vT, rel_bias, lam_p[l], sub_g, lam_init, min(ATTN_TILE, s))
        hm, c_p, n_new, m_new = _mlstm(qm, km, vm, om, gates, m_g, zero_c, 0, zero_n, zero_m, bp, s, tl_p,
                                       l, depth, c_p)
        xp = _merge(oa, hm, gts, xp, mod_p, False, wa, wm, wo)
        xp, buf_new = _ffn(xp, mod_p, False, g2, wu, conv_w[l], cb, wd, zero_buf, fg, bp, s, last)
        st_p.append((n_new, m_new, buf_new))

        mod_s = jnp.repeat(mod[l, bp:bp + bs], ts, axis=0)
        q, k_all, kbf, v_all, vbf = _in_attn(xs, mod_s, True, g1, w_qkv, False, l, depth, kv_s)
        kv_s = (k_all, v_all)
        qm, km, vm, om, gts, gates = _in_mlstm(xs, mod_s, True, g1, w_m, w_if, bif)
        oa = _attn_sample(q, kbf, vbf, cache_k, cache_v, l, rel_bias, lam_p[l], sub_g, lam_init, bs, ts)
        hm, c_s, n_new, m_new = _mlstm(qm, km, vm, om, gates, m_g, state_C, l, state_n[l], state_m[l], bs, ts, tl_s,
                                       l, depth, c_s)
        xs = _merge(oa, hm, gts, xs, mod_s, True, wa, wm, wo)
        xs, buf_new = _ffn(xs, mod_s, True, g2, wu, conv_w[l], cb, wd, state_conv[l], fg, bs, ts, last)
        st_s.append((n_new, m_new, buf_new))

    outs_p = [a.reshape(depth, bp, s, H_A, HE_A) for a in kv_p] + [c_p] + [jnp.stack(a) for a in zip(*st_p)]
    outs_s = [a.reshape(depth, bs, ts, H_A, HE_A) for a in kv_s] + [c_s] + [jnp.stack(a) for a in zip(*st_s)]
    return (xp.reshape(bp, s, d), xs.reshape(bs, ts, d), *outs_p, *outs_s)
```

```python
import functools
import math

import numpy as np
import jax
import jax.numpy as jnp
from jax import lax
from jax.experimental import pallas as pl
from jax.experimental.pallas import tpu as pltpu

F32 = jnp.float32
BF16 = jnp.bfloat16

CHUNK = 64
H_A = 8
HD_A = 64
HE_A = 2 * HD_A
H_M = 4
DH_M = 256
N_BUCKETS = 32
CONV_W = 3
NEG_INF = -1e30
EPS = 1e-6
FAR_BUCKET = N_BUCKETS // 2 - 1
LOG2E = math.log2(math.e)

V7X_VMEM_LIMIT = 56 * 1024 * 1024
ROW_TILE = 512
ATTN_TILE = 512
ATTN_FAR_BLOCKS = 4
BF16_ROWS = 16
MLSTM_TILE = 256

NT_DIMS = (((1,), (1,)), ((), ()))


def _cparams(sem):
    return pltpu.CompilerParams(dimension_semantics=sem, vmem_limit_bytes=V7X_VMEM_LIMIT)


def _dot(a, b):
    return jnp.dot(a, b, preferred_element_type=F32)


def _rms(x, g):
    return x * lax.rsqrt(jnp.mean(x * x, axis=-1, keepdims=True) + EPS) * g


def _log_sigmoid(x):
    return jnp.minimum(x, 0.0) - jnp.log1p(jnp.exp(-jnp.abs(x)))


def _mod_spec(per_row, tm, d, col, row_index):
    if per_row:
        return pl.BlockSpec((tm, d), lambda *g: (row_index(*g), col))
    return pl.BlockSpec((1, d), lambda *g: (0, col))


def _ada_kernel(c_ref, w_ref, b_ref, o_ref):
    a = jax.nn.silu(c_ref[...]).astype(BF16)
    o_ref[...] = _dot(a, w_ref[...].astype(BF16)) + b_ref[...]


def _ada(c_all, ada_w, ada_b):
    depth, d, n = ada_w.shape
    rc = c_all.shape[0]
    tn = 1536
    return pl.pallas_call(
        _ada_kernel,
        grid=(depth, n // tn),
        in_specs=[pl.BlockSpec((rc, d), lambda l, j: (0, 0)),
                  pl.BlockSpec((None, d, tn), lambda l, j: (l, 0, j)),
                  pl.BlockSpec((None, 1, tn), lambda l, j: (l, 0, j))],
        out_specs=pl.BlockSpec((None, rc, tn), lambda l, j: (l, 0, j)),
        out_shape=jax.ShapeDtypeStruct((depth, rc, n), F32),
        compiler_params=_cparams(("arbitrary", "arbitrary")),
        name="ada_mod",
    )(c_all, ada_w, ada_b.reshape(depth, 1, n))


def _in_attn_kernel(x_ref, g_ref, sc_ref, sh_ref, w_ref, *refs, d, transposed):
    q_ref, k32_ref, kbf_ref, v32_ref, vbf_ref = refs[-5:]
    xn = (_rms(x_ref[...], g_ref[...]) * (1.0 + sc_ref[...]) + sh_ref[...]).astype(BF16)
    q = _dot(xn, w_ref[:, 0:d]) * (HD_A ** -0.5 * (LOG2E if transposed else 1.0))
    k = _dot(xn, w_ref[:, d:2 * d])
    v = _dot(xn, w_ref[:, 2 * d:3 * d])
    k32_ref[...] = k
    kbf_ref[...] = k.astype(BF16)
    v32_ref[...] = v
    if transposed:
        q_ref[...] = q.T.astype(BF16)
        vbf_ref[...] = v.T.astype(BF16)
    else:
        q_ref[...] = q.astype(BF16)
        vbf_ref[...] = v.astype(BF16)


def _in_attn(x, mod, per_row, g, w_qkv, transposed, layer, depth, kv_states):
    r, d = x.shape
    tm = min(r, ROW_TILE)
    row = lambda i: i
    nat = pl.BlockSpec((tm, d), lambda i: (i, 0))
    tr = pl.BlockSpec((d, tm), lambda i: (0, i))
    state = pl.BlockSpec((None, tm, d), lambda i: (layer, i, 0))
    state_shape = jax.ShapeDtypeStruct((depth, r, d), F32)
    bf_shape = jax.ShapeDtypeStruct((d, r) if transposed else (r, d), BF16)
    prev = () if kv_states is None else tuple(kv_states)
    return pl.pallas_call(
        functools.partial(_in_attn_kernel, d=d, transposed=transposed),
        grid=(r // tm,),
        in_specs=[nat, pl.BlockSpec((1, d), lambda i: (0, 0)),
                  _mod_spec(per_row, tm, d, 1, row), _mod_spec(per_row, tm, d, 0, row),
                  pl.BlockSpec((d, 3 * d), lambda i: (0, 0))] + [pl.BlockSpec(memory_space=pl.ANY)] * len(prev),
        out_specs=[tr if transposed else nat, state, nat, state, tr if transposed else nat],
        out_shape=[bf_shape, state_shape, jax.ShapeDtypeStruct((r, d), BF16), state_shape, bf_shape],
        input_output_aliases={5: 1, 6: 3} if prev else {},
        compiler_params=_cparams(("arbitrary",)),
        name="in_attn",
    )(x, g, mod, mod, w_qkv, *prev)


def _in_mlstm_kernel(x_ref, g_ref, sc_ref, sh_ref, w_ref, wif_ref, bif_ref,
                     qm_ref, km_ref, vm_ref, om_ref, gts_ref, gate_ref, *, d):
    xn = (_rms(x_ref[...], g_ref[...]) * (1.0 + sc_ref[...]) + sh_ref[...]).astype(BF16)
    qm_ref[...] = _dot(xn, w_ref[:, 0:d]).astype(BF16)
    km_ref[...] = (_dot(xn, w_ref[:, d:2 * d]) * (DH_M ** -0.5)).astype(BF16)
    vm_ref[...] = _dot(xn, w_ref[:, 2 * d:3 * d]).astype(BF16)
    om_ref[...] = _dot(xn, w_ref[:, 3 * d:4 * d])
    gts_ref[...] = _dot(xn, w_ref[:, 4 * d:6 * d])
    pre = _dot(xn, wif_ref[...]) + bif_ref[...]
    gate_ref[...] = pre[:, 0:2 * H_M]


def _in_mlstm(x, mod, per_row, g, w_m, w_if, b_if):
    r, d = x.shape
    tm = min(r, ROW_TILE)
    row = lambda i: i
    nat = pl.BlockSpec((tm, d), lambda i: (i, 0))
    const = lambda shape: pl.BlockSpec(shape, lambda i: (0, 0))
    return pl.pallas_call(
        functools.partial(_in_mlstm_kernel, d=d),
        grid=(r // tm,),
        in_specs=[nat, const((1, d)), _mod_spec(per_row, tm, d, 1, row), _mod_spec(per_row, tm, d, 0, row),
                  const(w_m.shape), const(w_if.shape), const(b_if.shape)],
        out_specs=[nat, nat, nat, nat, pl.BlockSpec((tm, 2 * d), lambda i: (i, 0)),
                   pl.BlockSpec((tm, 2 * H_M), lambda i: (i, 0))],
        out_shape=[jax.ShapeDtypeStruct((r, d), BF16)] * 3
        + [jax.ShapeDtypeStruct((r, d), F32), jax.ShapeDtypeStruct((r, 2 * d), F32),
           jax.ShapeDtypeStruct((r, 2 * H_M), F32)],
        compiler_params=_cparams(("arbitrary",)),
        name="in_mlstm",
    )(x, g, mod, mod, w_m, w_if, b_if)


def _bucket_np(rel):
    half = N_BUCKETS // 2
    max_exact = half // 2
    n = np.abs(rel)
    thresholds = np.ceil(max_exact * 2.0 ** (np.arange(1, half - max_exact) / 2.0)).astype(np.int64)
    large = max_exact + (n[..., None] >= thresholds).sum(-1)
    return np.where(rel > 0, half, 0) + np.where(n < max_exact, n, np.minimum(large, half - 1))


def _lam(lp):
    s1 = jnp.sum(lp[0:1] * lp[1:2], axis=-1, keepdims=True)
    s2 = jnp.sum(lp[2:3] * lp[3:4], axis=-1, keepdims=True)
    return jnp.exp(s1) - jnp.exp(s2)


def _bias_steps(lo, hi):
    rel = np.arange(lo, hi + 1)
    b = _bucket_np(rel)
    starts = np.concatenate([[0], np.nonzero(b[1:] != b[:-1])[0] + 1])
    return [(int(rel[k]), int(b[k])) for k in starts]


def _rel_bias_tile(rb_ref, h, rel, lo, hi, shift, scale):
    steps = _bias_steps(lo, hi)
    v = jnp.full(rel.shape, (rb_ref[steps[0][1], h] - shift) * scale, F32)
    for start, bucket in steps[1:]:
        v = jnp.where(rel >= start, (rb_ref[bucket, h] - shift) * scale, v)
    return v


def _attn_prompt_kernel(rb_ref, lamp_ref, qT_ref, k_ref, vT_ref, subg_ref, o_ref,
                        q_s, bias_s, m_s, acc_s, *, t, far_blocks, lam_init):
    h = pl.program_id(0)
    i = pl.program_id(1)
    rows = 64
    chunk_shift = CHUNK.bit_length() - 1

    @pl.when(i == 0)
    def _():
        far = rb_ref[FAR_BUCKET, h]
        for tile, shift in enumerate((t, 0)):

            def fill(r, carry, tile=tile, shift=shift):
                r0 = pl.multiple_of(r * rows, rows)
                kk = lax.broadcasted_iota(jnp.int32, (rows, t), 0) + r0
                qq = lax.broadcasted_iota(jnp.int32, (rows, t), 1)
                v = _rel_bias_tile(rb_ref, h, kk - qq - shift, -shift - (t - 1), -shift + (t - 1), far, LOG2E)
                if tile == 1:
                    visible = jnp.right_shift(kk, chunk_shift) <= jnp.right_shift(qq, chunk_shift)
                    v = jnp.where(visible, v, NEG_INF)
                bias_s[tile, pl.ds(r0, rows), :] = v
                return carry

            lax.fori_loop(0, t // rows, fill, 0)

    lam = _lam(lamp_ref[...]) + lam_init
    qT = qT_ref[...]
    rowi = lax.broadcasted_iota(jnp.int32, qT.shape, 0)
    zero = jnp.zeros_like(qT)
    q_s[0] = jnp.where(rowi < HD_A, qT, zero)
    q_s[1] = jnp.where(rowi >= HD_A, qT, zero)

    def keys(j0, n):
        off = pl.multiple_of(j0 * t, t)
        tail = (lax.broadcasted_iota(jnp.int32, (BF16_ROWS, n), 0) == 0).astype(BF16)
        return k_ref[pl.ds(off, n), :], jnp.concatenate([vT_ref[:, pl.ds(off, n)], tail], axis=0)

    def scores(kb, c, tile):
        s = _dot(kb, q_s[c])
        return s if tile is None else s + bias_s[tile]

    def fixed_shift_block(j0, n, tile):
        kb, vb = keys(j0, n)
        for c in range(2):
            e = jnp.exp2(scores(kb, c, tile) - m_s[c])
            acc_s[c] += _dot(vb, e.astype(BF16))

    def running_max_block(j0, tile):
        kb, vb = keys(j0, t)
        for c in range(2):
            s = scores(kb, c, tile)
            m_old = m_s[c]
            m_new = jnp.maximum(m_old, jnp.max(s, axis=0, keepdims=True))
            acc_s[c] = jnp.exp2(m_old - m_new) * acc_s[c] + _dot(vb, jnp.exp2(s - m_new).astype(BF16))
            m_s[c] = m_new

    def finish():
        a0, a1 = acc_s[0], acc_s[1]
        l0, l1 = a0[HE_A:HE_A + 1], a1[HE_A:HE_A + 1]
        oT = a0[0:HE_A] * (1.0 / l0) - a1[0:HE_A] * (lam / l1)
        inv = lax.rsqrt(jnp.mean(oT * oT, axis=0, keepdims=True) + EPS)
        o_ref[...] = ((oT * inv).T * (subg_ref[...] * (1.0 - lam_init))).astype(BF16)
        big = float(np.finfo(np.float32).max)
        bad = jnp.max(jnp.where(jnp.abs(oT) <= big, 0.0, 1.0), axis=0, keepdims=True)
        return jnp.max(jnp.maximum(bad, jnp.where((l0 <= big) & (l1 <= big), 0.0, 1.0)))

    def first_step(r):
        nb = max(r, 0) + (2 if r >= 0 else 1)
        kb, vb = keys(i - (nb - 1), nb * t)
        for c in range(2):
            s = _dot(kb, q_s[c])
            parts = [s[0:r * t]] if r > 0 else []
            if r >= 0:
                parts.append(s[(nb - 2) * t:(nb - 1) * t] + bias_s[0])
            parts.append(s[(nb - 1) * t:nb * t] + bias_s[1])
            e = [jnp.exp2(p - m_s[c]).astype(BF16) for p in parts]
            acc_s[c] = _dot(vb, e[0] if len(e) == 1 else jnp.concatenate(e, axis=0))

    k0 = k_ref[pl.ds(pl.multiple_of(i * t, t), CHUNK), :]
    for c in range(2):
        m_s[c] = jnp.max(_dot(k0, q_s[c]) + bias_s[1, 0:CHUNK, :], axis=0, keepdims=True)

    nfar = jnp.maximum(i - 1, 0)
    folded = jnp.bitwise_and(nfar, far_blocks - 1)

    @pl.when(i == 0)
    def _():
        first_step(-1)

    for r in range(far_blocks):
        @pl.when((i >= 1) & (folded == r))
        def _(r=r):
            first_step(r)

    groups = lax.div(nfar, far_blocks)
    odd = jnp.bitwise_and(groups, 1)

    @pl.when(odd == 1)
    def _():
        fixed_shift_block(0, far_blocks * t, None)

    def big_step(p, carry):
        fixed_shift_block((odd + 2 * p) * far_blocks, 2 * far_blocks * t, None)
        return carry

    lax.fori_loop(0, lax.shift_right_logical(groups, 1), big_step, 0)

    @pl.when(finish() > 0.0)
    def _():
        m_s[...] = jnp.full(m_s.shape, NEG_INF, F32)
        acc_s[...] = jnp.zeros(acc_s.shape, F32)

        def step(j, carry):
            running_max_block(j, None)
            return carry

        lax.fori_loop(0, nfar, step, 0)

        @pl.when(i >= 1)
        def _():
            running_max_block(i - 1, 0)

        running_max_block(i, 1)
        finish()


def _attn_prompt(qT, kbf, vT, rel_bias, lam_p, sub_g, lam_init, t):
    d, s = qT.shape
    assert s % t == 0 and t % 64 == 0 and t % CHUNK == 0 and CHUNK & (CHUNK - 1) == 0
    assert _bucket_np(np.array([-(t + 1)]))[0] == FAR_BUCKET
    assert ATTN_FAR_BLOCKS & (ATTN_FAR_BLOCKS - 1) == 0
    nq = s // t
    return pl.pallas_call(
        functools.partial(_attn_prompt_kernel, t=t, far_blocks=ATTN_FAR_BLOCKS, lam_init=lam_init),
        grid=(H_A, nq),
        in_specs=[pl.BlockSpec(memory_space=pltpu.SMEM),
                  pl.BlockSpec((4, HD_A), lambda h, i: (0, 0)),
                  pl.BlockSpec((HE_A, t), lambda h, i: (h, i)),
                  pl.BlockSpec((s, HE_A), lambda h, i: (0, h)),
                  pl.BlockSpec((HE_A, s), lambda h, i: (h, 0)),
                  pl.BlockSpec((1, HE_A), lambda h, i: (0, 0))],
        out_specs=pl.BlockSpec((t, HE_A), lambda h, i: (i, h)),
        out_shape=jax.ShapeDtypeStruct((s, d), BF16),
        scratch_shapes=[pltpu.VMEM((2, HE_A, t), BF16), pltpu.VMEM((2, t, t), F32),
                        pltpu.VMEM((2, 1, t), F32), pltpu.VMEM((2, HE_A + BF16_ROWS, t), F32)],
        compiler_params=_cparams(("arbitrary", "arbitrary")),
        name="attn_prompt",
    )(rel_bias.astype(F32), lam_p, qT, kbf, vT, sub_g)


def _attn_sample_kernel(rb_ref, lamp_ref, q_ref, kc_ref, vc_ref, kn_ref, vn_ref, subg_ref, o_ref,
                        *, past, lam_init):
    t = q_ref.shape[0]
    chunk_shift = CHUNK.bit_length() - 1
    lam = _lam(lamp_ref[...]) + lam_init

    def positions(n_keys, first_key):
        qpos = past + lax.broadcasted_iota(jnp.int32, (t, n_keys), 0)
        kpos = first_key + lax.broadcasted_iota(jnp.int32, (t, n_keys), 1)
        return kpos - qpos, jnp.right_shift(kpos, chunk_shift) <= jnp.right_shift(qpos, chunk_shift)

    rel_c, vis_c = positions(past, 0)
    rel_n, vis_n = positions(t, past)
    for h in range(H_A):
        sl = slice(h * HE_A, (h + 1) * HE_A)
        bias_c = jnp.where(vis_c, _rel_bias_tile(rb_ref, h, rel_c, -(past + t - 1), -1, 0.0, 1.0), NEG_INF)
        bias_n = jnp.where(vis_n, _rel_bias_tile(rb_ref, h, rel_n, -(t - 1), t - 1, 0.0, 1.0), NEG_INF)
        q = q_ref[:, sl]
        lane = lax.broadcasted_iota(jnp.int32, q.shape, 1)
        zero = jnp.zeros_like(q)
        q2 = jnp.concatenate([jnp.where(lane < HD_A, q, zero), jnp.where(lane >= HD_A, q, zero)], axis=0)
        kc = kc_ref[pl.ds(h, past, stride=H_A), :].astype(BF16)
        sc = lax.dot_general(q2, kc, NT_DIMS, preferred_element_type=F32) + jnp.concatenate([bias_c, bias_c], 0)
        sn = (lax.dot_general(q2, kn_ref[:, sl], NT_DIMS, preferred_element_type=F32)
              + jnp.concatenate([bias_n, bias_n], 0))
        m = jnp.maximum(jnp.max(sc, axis=-1, keepdims=True), jnp.max(sn, axis=-1, keepdims=True))
        ec = jnp.exp(sc - m)
        en = jnp.exp(sn - m)
        l = jnp.sum(ec, axis=-1, keepdims=True) + jnp.sum(en, axis=-1, keepdims=True)
        pc = ec / l
        pn = en / l
        ac = pc[0:t] - lam * pc[t:2 * t]
        an = pn[0:t] - lam * pn[t:2 * t]
        vc = vc_ref[pl.ds(h, past, stride=H_A), :].astype(BF16)
        o = _dot(ac.astype(BF16), vc) + _dot(an.astype(BF16), vn_ref[:, sl])
        o_ref[:, sl] = (_rms(o, subg_ref[...]) * (1.0 - lam_init)).astype(BF16)


def _attn_sample(q, kbf, vbf, cache_k, cache_v, layer, rel_bias, lam_p, sub_g, lam_init, b, t):
    r, d = q.shape
    past = cache_k.shape[2]
    rows = pl.BlockSpec((t, d), lambda i: (i, 0))
    cache_k = cache_k.reshape(-1, HE_A)
    cache_v = cache_v.reshape(-1, HE_A)
    cache = pl.BlockSpec((past * H_A, HE_A), lambda i: (layer * b + i, 0))
    return pl.pallas_call(
        functools.partial(_attn_sample_kernel, past=past, lam_init=lam_init),
        grid=(b,),
        in_specs=[pl.BlockSpec(memory_space=pltpu.SMEM),
                  pl.BlockSpec((4, HD_A), lambda i: (0, 0)),
                  rows, cache, cache, rows, rows,
                  pl.BlockSpec((1, HE_A), lambda i: (0, 0))],
        out_specs=rows,
        out_shape=jax.ShapeDtypeStruct((r, d), BF16),
        compiler_params=_cparams(("arbitrary",)),
        name="attn_sample",
    )(rel_bias.astype(F32), lam_p, q, cache_k, cache_v, kbf, vbf, sub_g)


def _mlstm_kernel(q_ref, k_ref, v_ref, om_ref, g_ref, gT_ref, mg_ref, c0_ref, n0_ref, m0_ref, *refs, tl):
    h_ref, c_ref, n_ref, m_ref, c_s, n_s, m_s = refs[-7:]
    step = pl.program_id(1)

    @pl.when(step == 0)
    def _():
        c_s[...] = c0_ref[...]
        n_s[...] = n0_ref[...]
        m_s[...] = m0_ref[...]

    ti = lax.broadcasted_iota(jnp.int32, (tl, tl), 0)
    si = lax.broadcasted_iota(jnp.int32, (tl, tl), 1)
    causal = si <= ti
    gates_c = g_ref[...]
    gates_r = gT_ref[...]
    lf_cols = _log_sigmoid(gates_c)
    lf_rows = _log_sigmoid(gates_r)
    for h in range(H_M):
        sl = slice(h * DH_M, (h + 1) * DH_M)
        q = q_ref[:, sl]
        k = k_ref[:, sl]
        v = v_ref[:, sl]
        ig_c = gates_c[:, h:h + 1]
        lf_c = lf_cols[:, H_M + h:H_M + h + 1]
        ig_r = gates_r[h:h + 1, :]
        lf_r = lf_rows[H_M + h:H_M + h + 1, :]
        m = m_s[h]
        n_row = n_s[h]
        b_c = jnp.sum(jnp.where(causal, lf_r, 0.0), axis=1, keepdims=True)
        b_r = jnp.sum(jnp.where(ti <= si, lf_c, 0.0), axis=0, keepdims=True)
        dmat = jnp.where(causal, b_c - b_r + ig_r, -jnp.inf)
        inter = b_c + m
        m_t = jnp.maximum(inter, jnp.max(dmat, axis=1, keepdims=True))
        w_intra = jnp.exp(dmat - m_t)
        w_inter = jnp.exp(inter - m_t)
        a = w_intra * lax.dot_general(q, k, NT_DIMS, preferred_element_type=F32)
        num = w_inter * _dot(q, c_s[h].astype(BF16)) + _dot(a.astype(BF16), v)
        den = (w_inter * jnp.sum(q.astype(F32) * n_row, axis=1, keepdims=True)
               + jnp.sum(a, axis=1, keepdims=True))
        hh = num / jnp.maximum(jnp.abs(den), jnp.exp(-m_t))
        b_last = b_r[:, tl - 1:tl]
        g_c = b_last - b_c + ig_c
        m_new = jnp.maximum(b_last + m, jnp.max(g_c, axis=0, keepdims=True))
        decay = jnp.exp(b_last + m - m_new)
        kw = jnp.exp(g_c - m_new) * k.astype(F32)
        c_s[h] = decay * c_s[h] + _dot(kw.T.astype(BF16), v)
        n_s[h] = decay * n_row + jnp.sum(kw, axis=0, keepdims=True)
        m_s[h] = m_new
        hn = _rms(hh, mg_ref[:, sl])
        h_ref[:, sl] = (hn * jax.nn.sigmoid(om_ref[:, sl])).astype(BF16)

    @pl.when(step == pl.num_programs(1) - 1)
    def _():
        c_ref[...] = c_s[...]
        n_ref[...] = n_s[...]
        m_ref[...] = m_s[...]


def _mlstm(qm, km, vm, om, gates, m_g, c0, c0_layer, n0, m0, b, t, tl, layer, depth, c_states):
    r, d = qm.shape
    nc = t // tl
    gates_t = jnp.transpose(gates.reshape(b * nc, tl, 2 * H_M), (0, 2, 1))
    rows = pl.BlockSpec((tl, d), lambda i, c: (i * nc + c, 0))
    st_n = pl.BlockSpec((None, H_M, 1, DH_M), lambda i, c: (i, 0, 0, 0))
    st_m = pl.BlockSpec((None, H_M, 1, 1), lambda i, c: (i, 0, 0, 0))
    prev = () if c_states is None else (c_states,)
    h, c_states, n_new, m_new = pl.pallas_call(
        functools.partial(_mlstm_kernel, tl=tl),
        grid=(b, nc),
        in_specs=[rows, rows, rows, rows,
                  pl.BlockSpec((tl, 2 * H_M), lambda i, c: (i * nc + c, 0)),
                  pl.BlockSpec((None, 2 * H_M, tl), lambda i, c: (i * nc + c, 0, 0)),
                  pl.BlockSpec((1, d), lambda i, c: (0, 0)),
                  pl.BlockSpec((None, None, H_M, DH_M, DH_M), lambda i, c: (c0_layer, i, 0, 0, 0)),
                  st_n, st_m] + [pl.BlockSpec(memory_space=pl.ANY)] * len(prev),
        out_specs=[rows, pl.BlockSpec((None, None, H_M, DH_M, DH_M), lambda i, c: (layer, i, 0, 0, 0)),
                   st_n, st_m],
        out_shape=[jax.ShapeDtypeStruct((r, d), BF16),
                   jax.ShapeDtypeStruct((depth, b, H_M, DH_M, DH_M), F32),
                   jax.ShapeDtypeStruct((b, H_M, 1, DH_M), F32),
                   jax.ShapeDtypeStruct((b, H_M, 1, 1), F32)],
        input_output_aliases={10: 1} if prev else {},
        scratch_shapes=[pltpu.VMEM((H_M, DH_M, DH_M), F32), pltpu.VMEM((H_M, 1, DH_M), F32),
                        pltpu.VMEM((H_M, 1, 1), F32)],
        compiler_params=_cparams(("arbitrary", "arbitrary")),
        name="mlstm",
    )(qm, km, vm, om, gates, gates_t, m_g, c0, n0.reshape(b, H_M, 1, DH_M), m0.reshape(b, H_M, 1, 1), *prev)
    return h, c_states, n_new.reshape(b, H_M, DH_M), m_new.reshape(b, H_M)


def _merge_kernel(oa_ref, hm_ref, gts_ref, x_ref, g1_ref, wa_ref, wm_ref, wo_ref, o_ref, *, d):
    ya = _dot(oa_ref[...], wa_ref[...])
    ym = _dot(hm_ref[...], wm_ref[...])
    mix = jax.nn.sigmoid(gts_ref[:, 0:d]) * ya + jax.nn.sigmoid(gts_ref[:, d:2 * d]) * ym
    o_ref[...] = x_ref[...] + g1_ref[...] * _dot(mix.astype(BF16), wo_ref[...])


def _merge(oa, hm, gts, x, mod, per_row, wa, wm, wo):
    r, d = x.shape
    tm = min(r, ROW_TILE)
    nat = pl.BlockSpec((tm, d), lambda i: (i, 0))
    wspec = pl.BlockSpec((d, d), lambda i: (0, 0))
    return pl.pallas_call(
        functools.partial(_merge_kernel, d=d),
        grid=(r // tm,),
        in_specs=[nat, nat, pl.BlockSpec((tm, 2 * d), lambda i: (i, 0)), nat,
                  _mod_spec(per_row, tm, d, 2, lambda i: i), wspec, wspec, wspec],
        out_specs=nat,
        out_shape=jax.ShapeDtypeStruct((r, d), F32),
        compiler_params=_cparams(("arbitrary",)),
        name="merge_out",
    )(oa, hm, gts, x, mod, wa, wm, wo)


def _ffn_kernel(x_ref, g_ref, sc_ref, sh_ref, g2_ref, wu_ref, cw_ref, cb_ref, wd_ref, buf_ref, fg_ref,
                o_ref, nb_ref, tail_s, *, tm, dff, nc, final):
    first = pl.program_id(1) == 0
    ck = dff // nc
    x = x_ref[...]
    xn = (_rms(x, g_ref[...]) * (1.0 + sc_ref[...]) + sh_ref[...]).astype(BF16)
    rowi = lax.broadcasted_iota(jnp.int32, (tm, ck), 0)
    acc = None
    for c in range(nc):
        cs = slice(c * ck, (c + 1) * ck)
        u = _dot(xn, wu_ref[:, cs])
        gate = _dot(xn, wu_ref[:, dff + c * ck:dff + (c + 1) * ck])
        prev2 = jnp.where(first, buf_ref[0:1, cs], tail_s[6:7, cs])
        prev1 = jnp.where(first, buf_ref[1:2, cs], tail_s[7:8, cs])
        u1 = jnp.where(rowi == 0, prev1, pltpu.roll(u, 1, 0))
        u2 = jnp.where(rowi == 0, prev2, jnp.where(rowi == 1, prev1, pltpu.roll(u, 2, 0)))
        conv = cb_ref[:, cs] + cw_ref[0:1, cs] * u2 + cw_ref[1:2, cs] * u1 + cw_ref[2:3, cs] * u
        part = _dot((jax.nn.gelu(conv) * gate).astype(BF16), wd_ref[cs, :])
        acc = part if acc is None else acc + part
        tail_s[:, cs] = u[tm - 8:tm, :]
        nb_ref[:, cs] = u[tm - (CONV_W - 1):tm, :]
    y = x + g2_ref[...] * acc
    if final:
        y = _rms(y, fg_ref[...])
    o_ref[...] = y


def _ffn(x, mod, per_row, g, w_up, conv_w, conv_b, w_down, conv_buf, final_g, b, t, final):
    r, d = x.shape
    dff = w_down.shape[0]
    tm = min(t, ROW_TILE)
    nt = t // tm
    nc = 2
    assert dff % (nc * 128) == 0 and tm >= 8
    row = lambda i, j: i * nt + j
    nat = pl.BlockSpec((tm, d), lambda i, j: (i * nt + j, 0))
    whole = lambda a: pl.BlockSpec(a.shape, lambda i, j: (0,) * a.ndim)
    y, tails = pl.pallas_call(
        functools.partial(_ffn_kernel, tm=tm, dff=dff, nc=nc, final=final),
        grid=(b, nt),
        in_specs=[nat, whole(g), _mod_spec(per_row, tm, d, 4, row), _mod_spec(per_row, tm, d, 3, row),
                  _mod_spec(per_row, tm, d, 5, row), whole(w_up), whole(conv_w), whole(conv_b), whole(w_down),
                  pl.BlockSpec((None, CONV_W - 1, dff), lambda i, j: (i, 0, 0)), whole(final_g)],
        out_specs=[nat, pl.BlockSpec((None, None, CONV_W - 1, dff), lambda i, j: (i, j, 0, 0))],
        out_shape=[jax.ShapeDtypeStruct((r, d), F32), jax.ShapeDtypeStruct((b, nt, CONV_W - 1, dff), F32)],
        scratch_shapes=[pltpu.VMEM((8, dff), F32)],
        compiler_params=_cparams(("arbitrary", "arbitrary")),
        name="conv_ffn",
    )(x, g, mod, mod, mod, w_up, conv_w, conv_b, w_down, conv_buf, final_g)
    return y, tails[:, nt - 1]


def kernel(x_prompt, x_sample, c_prompt, c_sample, cache_k, cache_v, state_C, state_n, state_m, state_conv,
           rel_bias, ada_w, ada_b, norm1_g, norm2_g, w_in, b_if, lam_p, attn_sub_g, mlstm_g, w_br_a, w_br_m,
           w_out, w_up, conv_w, conv_b, w_down, final_g):
    bp, s, d = x_prompt.shape
    bs, ts, _ = x_sample.shape
    depth = w_in.shape[0]
    dff = w_down.shape[1]
    assert bp == 1 and d == H_A * HE_A == H_M * DH_M

    c_all = jnp.concatenate([c_prompt, c_sample, jnp.zeros((16 - bp - bs, d), F32)], axis=0)
    mod = _ada(c_all, ada_w, ada_b)

    xp = x_prompt.reshape(bp * s, d)
    xs = x_sample.reshape(bs * ts, d)
    tl_p = min(MLSTM_TILE, s)
    tl_s = min(CHUNK, ts)
    zero_c = jnp.zeros((1, bp, H_M, DH_M, DH_M), F32)
    zero_n = jnp.zeros((bp, H_M, DH_M), F32)
    zero_m = jnp.zeros((bp, H_M), F32)
    zero_buf = jnp.zeros((bp, CONV_W - 1, dff), F32)
    st_p, st_s = [], []
    kv_p = kv_s = c_p = c_s = None
    for l in range(depth):
        lam_init = 0.8 - 0.6 * math.exp(-0.3 * l)
        last = l == depth - 1
        w_qkv = w_in[l, :, 0:3 * d].astype(BF16)
        w_m = w_in[l, :, 3 * d:9 * d].astype(BF16)
        w_if = jnp.pad(w_in[l, :, 9 * d:], ((0, 0), (0, 128 - 2 * H_M))).astype(BF16)
        bif = jnp.pad(b_if[l], (0, 128 - 2 * H_M)).reshape(1, 128)
        wa, wm, wo = w_br_a[l].astype(BF16), w_br_m[l].astype(BF16), w_out[l].astype(BF16)
        wu, wd = w_up[l].astype(BF16), w_down[l].astype(BF16)
        g1, g2 = norm1_g[l].reshape(1, d), norm2_g[l].reshape(1, d)
        sub_g, m_g = attn_sub_g[l].reshape(1, HE_A), mlstm_g[l].reshape(1, d)
        cb = conv_b[l].reshape(1, dff)
        fg = final_g.reshape(1, d)

        mod_p = mod[l, 0:bp]
        qT, k_all, kbf, v_all, vT = _in_attn(xp, mod_p, False, g1, w_qkv, True, l, depth, kv_p)
        kv_p = (k_all, v_all)
        qm, km, vm, om, gts, gates = _in_mlstm(xp, mod_p, False, g1, w_m, w_if, bif)
        oa = _attn_prompt(qT, kbf, vT, rel_bias, lam_p[l], sub_g, lam_init, min(ATTN_TILE, s))
        hm, c_p, n_new, m_new = _mlstm(qm, km, vm, om, gates, m_g, zero_c, 0, zero_n, zero_m, bp, s, tl_p,
                                       l, depth, c_p)
        xp = _merge(oa, hm, gts, xp, mod_p, False, wa, wm, wo)
        xp, buf_new = _ffn(xp, mod_p, False, g2, wu, conv_w[l], cb, wd, zero_buf, fg, bp, s, last)
        st_p.append((n_new, m_new, buf_new))

        mod_s = jnp.repeat(mod[l, bp:bp + bs], ts, axis=0)
        q, k_all, kbf, v_all, vbf = _in_attn(xs, mod_s, True, g1, w_qkv, False, l, depth, kv_s)
        kv_s = (k_all, v_all)
        qm, km, vm, om, gts, gates = _in_mlstm(xs, mod_s, True, g1, w_m, w_if, bif)
        oa = _attn_sample(q, kbf, vbf, cache_k, cache_v, l, rel_bias, lam_p[l], sub_g, lam_init, bs, ts)
        hm, c_s, n_new, m_new = _mlstm(qm, km, vm, om, gates, m_g, state_C, l, state_n[l], state_m[l], bs, ts, tl_s,
                                       l, depth, c_s)
        xs = _merge(oa, hm, gts, xs, mod_s, True, wa, wm, wo)
        xs, buf_new = _ffn(xs, mod_s, True, g2, wu, conv_w[l], cb, wd, state_conv[l], fg, bs, ts, last)
        st_s.append((n_new, m_new, buf_new))

    outs_p = [a.reshape(depth, bp, s, H_A, HE_A) for a in kv_p] + [c_p] + [jnp.stack(a) for a in zip(*st_p)]
    outs_s = [a.reshape(depth, bs, ts, H_A, HE_A) for a in kv_s] + [c_s] + [jnp.stack(a) for a in zip(*st_s)]
    return (xp.reshape(bp, s, d), xs.reshape(bs, ts, d), *outs_p, *outs_s)
```

```python
import functools
import math

import numpy as np
import jax
import jax.numpy as jnp
from jax import lax
from jax.experimental import pallas as pl
from jax.experimental.pallas import tpu as pltpu

F32 = jnp.float32
BF16 = jnp.bfloat16

CHUNK = 64
H_A = 8
HD_A = 64
HE_A = 2 * HD_A
H_M = 4
DH_M = 256
N_BUCKETS = 32
CONV_W = 3
NEG_INF = -1e30
EPS = 1e-6
FAR_BUCKET = N_BUCKETS // 2 - 1
LOG2E = math.log2(math.e)

V7X_VMEM_LIMIT = 56 * 1024 * 1024
ROW_TILE = 512
ATTN_TILE = 512
ATTN_FAR_BLOCKS = 4
BF16_ROWS = 16
MLSTM_TILE = 256

NT_DIMS = (((1,), (1,)), ((), ()))


def _cparams(sem):
    return pltpu.CompilerParams(dimension_semantics=sem, vmem_limit_bytes=V7X_VMEM_LIMIT)


def _dot(a, b):
    return jnp.dot(a, b, preferred_element_type=F32)


def _rms(x, g):
    return x * lax.rsqrt(jnp.mean(x * x, axis=-1, keepdims=True) + EPS) * g


def _log_sigmoid(x):
    return jnp.minimum(x, 0.0) - jnp.log1p(jnp.exp(-jnp.abs(x)))


def _mod_spec(per_row, tm, d, col, row_index):
    if per_row:
        return pl.BlockSpec((tm, d), lambda *g: (row_index(*g), col))
    return pl.BlockSpec((1, d), lambda *g: (0, col))


def _ada_kernel(c_ref, w_ref, b_ref, o_ref):
    a = jax.nn.silu(c_ref[...]).astype(BF16)
    o_ref[...] = _dot(a, w_ref[...].astype(BF16)) + b_ref[...]


def _ada(c_all, ada_w, ada_b):
    depth, d, n = ada_w.shape
    rc = c_all.shape[0]
    tn = 1536
    return pl.pallas_call(
        _ada_kernel,
        grid=(depth, n // tn),
        in_specs=[pl.BlockSpec((rc, d), lambda l, j: (0, 0)),
                  pl.BlockSpec((None, d, tn), lambda l, j: (l, 0, j)),
                  pl.BlockSpec((None, 1, tn), lambda l, j: (l, 0, j))],
        out_specs=pl.BlockSpec((None, rc, tn), lambda l, j: (l, 0, j)),
        out_shape=jax.ShapeDtypeStruct((depth, rc, n), F32),
        compiler_params=_cparams(("arbitrary", "arbitrary")),
        name="ada_mod",
    )(c_all, ada_w, ada_b.reshape(depth, 1, n))


def _in_attn_kernel(x_ref, g_ref, sc_ref, sh_ref, w_ref, *refs, d, transposed, layer, creates):
    q_ref, k32_ref, kbf_ref, v32_ref, vbf_ref = refs[-5:]
    xn = (_rms(x_ref[...], g_ref[...]) * (1.0 + sc_ref[...]) + sh_ref[...]).astype(BF16)
    q = _dot(xn, w_ref[:, 0:d]) * (HD_A ** -0.5 * (LOG2E if transposed else 1.0))
    k = _dot(xn, w_ref[:, d:2 * d])
    v = _dot(xn, w_ref[:, 2 * d:3 * d])
    if creates:
        for l in range(k32_ref.shape[0]):
            k32_ref[l] = k if l == layer else jnp.zeros_like(k)
            v32_ref[l] = v if l == layer else jnp.zeros_like(v)
    else:
        k32_ref[...] = k
        v32_ref[...] = v
    kbf_ref[...] = k.astype(BF16)
    if transposed:
        q_ref[...] = q.T.astype(BF16)
        vbf_ref[...] = v.T.astype(BF16)
    else:
        q_ref[...] = q.astype(BF16)
        vbf_ref[...] = v.astype(BF16)


def _in_attn(x, mod, per_row, g, w_qkv, transposed, layer, depth, kv_states):
    r, d = x.shape
    tm = min(r, ROW_TILE)
    row = lambda i: i
    nat = pl.BlockSpec((tm, d), lambda i: (i, 0))
    tr = pl.BlockSpec((d, tm), lambda i: (0, i))
    prev = () if kv_states is None else tuple(kv_states)
    state = (pl.BlockSpec((None, tm, d), lambda i: (layer, i, 0)) if prev else
             pl.BlockSpec((depth, tm, d), lambda i: (0, i, 0)))
    state_shape = jax.ShapeDtypeStruct((depth, r, d), F32)
    bf_shape = jax.ShapeDtypeStruct((d, r) if transposed else (r, d), BF16)
    return pl.pallas_call(
        functools.partial(_in_attn_kernel, d=d, transposed=transposed, layer=layer, creates=not prev),
        grid=(r // tm,),
        in_specs=[nat, pl.BlockSpec((1, d), lambda i: (0, 0)),
                  _mod_spec(per_row, tm, d, 1, row), _mod_spec(per_row, tm, d, 0, row),
                  pl.BlockSpec((d, 3 * d), lambda i: (0, 0))] + [pl.BlockSpec(memory_space=pl.ANY)] * len(prev),
        out_specs=[tr if transposed else nat, state, nat, state, tr if transposed else nat],
        out_shape=[bf_shape, state_shape, jax.ShapeDtypeStruct((r, d), BF16), state_shape, bf_shape],
        input_output_aliases={5: 1, 6: 3} if prev else {},
        compiler_params=_cparams(("arbitrary",)),
        name="in_attn",
    )(x, g, mod, mod, w_qkv, *prev)


def _in_mlstm_kernel(x_ref, g_ref, sc_ref, sh_ref, w_ref, wif_ref, bif_ref,
                     qm_ref, km_ref, vm_ref, om_ref, gts_ref, gate_ref, *, d):
    xn = (_rms(x_ref[...], g_ref[...]) * (1.0 + sc_ref[...]) + sh_ref[...]).astype(BF16)
    qm_ref[...] = _dot(xn, w_ref[:, 0:d]).astype(BF16)
    km_ref[...] = (_dot(xn, w_ref[:, d:2 * d]) * (DH_M ** -0.5)).astype(BF16)
    vm_ref[...] = _dot(xn, w_ref[:, 2 * d:3 * d]).astype(BF16)
    om_ref[...] = _dot(xn, w_ref[:, 3 * d:4 * d])
    gts_ref[...] = _dot(xn, w_ref[:, 4 * d:6 * d])
    pre = _dot(xn, wif_ref[...]) + bif_ref[...]
    gate_ref[...] = pre[:, 0:2 * H_M]


def _in_mlstm(x, mod, per_row, g, w_m, w_if, b_if):
    r, d = x.shape
    tm = min(r, ROW_TILE)
    row = lambda i: i
    nat = pl.BlockSpec((tm, d), lambda i: (i, 0))
    const = lambda shape: pl.BlockSpec(shape, lambda i: (0, 0))
    return pl.pallas_call(
        functools.partial(_in_mlstm_kernel, d=d),
        grid=(r // tm,),
        in_specs=[nat, const((1, d)), _mod_spec(per_row, tm, d, 1, row), _mod_spec(per_row, tm, d, 0, row),
                  const(w_m.shape), const(w_if.shape), const(b_if.shape)],
        out_specs=[nat, nat, nat, nat, pl.BlockSpec((tm, 2 * d), lambda i: (i, 0)),
                   pl.BlockSpec((tm, 2 * H_M), lambda i: (i, 0))],
        out_shape=[jax.ShapeDtypeStruct((r, d), BF16)] * 3
        + [jax.ShapeDtypeStruct((r, d), F32), jax.ShapeDtypeStruct((r, 2 * d), F32),
           jax.ShapeDtypeStruct((r, 2 * H_M), F32)],
        compiler_params=_cparams(("arbitrary",)),
        name="in_mlstm",
    )(x, g, mod, mod, w_m, w_if, b_if)


def _bucket_np(rel):
    half = N_BUCKETS // 2
    max_exact = half // 2
    n = np.abs(rel)
    thresholds = np.ceil(max_exact * 2.0 ** (np.arange(1, half - max_exact) / 2.0)).astype(np.int64)
    large = max_exact + (n[..., None] >= thresholds).sum(-1)
    return np.where(rel > 0, half, 0) + np.where(n < max_exact, n, np.minimum(large, half - 1))


def _lam(lp):
    s1 = jnp.sum(lp[0:1] * lp[1:2], axis=-1, keepdims=True)
    s2 = jnp.sum(lp[2:3] * lp[3:4], axis=-1, keepdims=True)
    return jnp.exp(s1) - jnp.exp(s2)


def _bias_steps(lo, hi):
    rel = np.arange(lo, hi + 1)
    b = _bucket_np(rel)
    starts = np.concatenate([[0], np.nonzero(b[1:] != b[:-1])[0] + 1])
    return [(int(rel[k]), int(b[k])) for k in starts]


def _rel_bias_tile(rb_ref, h, rel, lo, hi, shift, scale):
    steps = _bias_steps(lo, hi)
    v = jnp.full(rel.shape, (rb_ref[steps[0][1], h] - shift) * scale, F32)
    for start, bucket in steps[1:]:
        v = jnp.where(rel >= start, (rb_ref[bucket, h] - shift) * scale, v)
    return v


def _attn_prompt_kernel(rb_ref, lamp_ref, qT_ref, k_ref, vT_ref, subg_ref, o_ref,
                        q_s, bias_s, m_s, l_s, acc_s, *, t, far_blocks, lam_init):
    h = pl.program_id(0)
    i = pl.program_id(1)
    rows = 64
    chunk_shift = CHUNK.bit_length() - 1

    @pl.when(i == 0)
    def _():
        far = rb_ref[FAR_BUCKET, h]
        for tile, shift in enumerate((t, 0)):

            def fill(r, carry, tile=tile, shift=shift):
                r0 = pl.multiple_of(r * rows, rows)
                kk = lax.broadcasted_iota(jnp.int32, (rows, t), 0) + r0
                qq = lax.broadcasted_iota(jnp.int32, (rows, t), 1)
                v = _rel_bias_tile(rb_ref, h, kk - qq - shift, -shift - (t - 1), -shift + (t - 1), far, LOG2E)
                if tile == 1:
                    visible = jnp.right_shift(kk, chunk_shift) <= jnp.right_shift(qq, chunk_shift)
                    v = jnp.where(visible, v, NEG_INF)
                bias_s[tile, pl.ds(r0, rows), :] = v
                return carry

            lax.fori_loop(0, t // rows, fill, 0)

    lam = _lam(lamp_ref[...]) + lam_init
    qT = qT_ref[...]
    rowi = lax.broadcasted_iota(jnp.int32, qT.shape, 0)
    zero = jnp.zeros_like(qT)
    q_s[0] = jnp.where(rowi < HD_A, qT, zero)
    q_s[1] = jnp.where(rowi >= HD_A, qT, zero)

    def keys(j0, n):
        off = pl.multiple_of(j0 * t, t)
        return k_ref[pl.ds(off, n), :], vT_ref[:, pl.ds(off, n)]

    def scores(kb, c, tile):
        s = _dot(kb, q_s[c])
        return s if tile is None else s + bias_s[tile]

    def fixed_shift_block(j0, n, tile):
        kb, vb = keys(j0, n)
        for c in range(2):
            e = jnp.exp2(scores(kb, c, tile) - m_s[c])
            l_s[c] += jnp.sum(e, axis=0, keepdims=True)
            acc_s[c] += _dot(vb, e.astype(BF16))

    def running_max_block(j0, tile):
        kb, vb = keys(j0, t)
        for c in range(2):
            s = scores(kb, c, tile)
            m_old = m_s[c]
            m_new = jnp.maximum(m_old, jnp.max(s, axis=0, keepdims=True))
            alpha = jnp.exp2(m_old - m_new)
            e = jnp.exp2(s - m_new)
            l_s[c] = alpha * l_s[c] + jnp.sum(e, axis=0, keepdims=True)
            acc_s[c] = alpha * acc_s[c] + _dot(vb, e.astype(BF16))
            m_s[c] = m_new

    def finish():
        l0, l1 = l_s[0], l_s[1]
        oT = acc_s[0] * (1.0 / l0) - acc_s[1] * (lam / l1)
        inv = lax.rsqrt(jnp.mean(oT * oT, axis=0, keepdims=True) + EPS)
        o_ref[...] = ((oT * inv).T * (subg_ref[...] * (1.0 - lam_init))).astype(BF16)
        big = float(np.finfo(np.float32).max)
        bad = jnp.max(jnp.where(jnp.abs(oT) <= big, 0.0, 1.0), axis=0, keepdims=True)
        return jnp.max(jnp.maximum(bad, jnp.where((l0 <= big) & (l1 <= big), 0.0, 1.0)))

    def first_step(r):
        nb = max(r, 0) + (2 if r >= 0 else 1)
        kb, vb = keys(i - (nb - 1), nb * t)
        for c in range(2):
            s = _dot(kb, q_s[c])
            parts = [s[0:r * t]] if r > 0 else []
            if r >= 0:
                parts.append(s[(nb - 2) * t:(nb - 1) * t] + bias_s[0])
            parts.append(s[(nb - 1) * t:nb * t] + bias_s[1])
            e = [jnp.exp2(p - m_s[c]) for p in parts]
            l_s[c] = functools.reduce(jnp.add, [jnp.sum(p, axis=0, keepdims=True) for p in e])
            e = [p.astype(BF16) for p in e]
            acc_s[c] = _dot(vb, e[0] if len(e) == 1 else jnp.concatenate(e, axis=0))

    k0 = k_ref[pl.ds(pl.multiple_of(i * t, t), CHUNK), :]
    for c in range(2):
        m_s[c] = jnp.max(_dot(k0, q_s[c]) + bias_s[1, 0:CHUNK, :], axis=0, keepdims=True)

    nfar = jnp.maximum(i - 1, 0)
    folded = jnp.bitwise_and(nfar, far_blocks - 1)

    @pl.when(i == 0)
    def _():
        first_step(-1)

    for r in range(far_blocks):
        @pl.when((i >= 1) & (folded == r))
        def _(r=r):
            first_step(r)

    groups = lax.div(nfar, far_blocks)
    odd = jnp.bitwise_and(groups, 1)

    @pl.when(odd == 1)
    def _():
        fixed_shift_block(0, far_blocks * t, None)

    def big_step(p, carry):
        fixed_shift_block((odd + 2 * p) * far_blocks, 2 * far_blocks * t, None)
        return carry

    lax.fori_loop(0, lax.shift_right_logical(groups, 1), big_step, 0)

    @pl.when(finish() > 0.0)
    def _():
        m_s[...] = jnp.full(m_s.shape, NEG_INF, F32)
        l_s[...] = jnp.zeros(l_s.shape, F32)
        acc_s[...] = jnp.zeros(acc_s.shape, F32)

        def step(j, carry):
            running_max_block(j, None)
            return carry

        lax.fori_loop(0, nfar, step, 0)

        @pl.when(i >= 1)
        def _():
            running_max_block(i - 1, 0)

        running_max_block(i, 1)
        finish()


def _attn_prompt(qT, kbf, vT, rel_bias, lam_p, sub_g, lam_init, t):
    d, s = qT.shape
    assert s % t == 0 and t % 64 == 0 and t % CHUNK == 0 and CHUNK & (CHUNK - 1) == 0
    assert _bucket_np(np.array([-(t + 1)]))[0] == FAR_BUCKET
    assert ATTN_FAR_BLOCKS & (ATTN_FAR_BLOCKS - 1) == 0
    nq = s // t
    return pl.pallas_call(
        functools.partial(_attn_prompt_kernel, t=t, far_blocks=ATTN_FAR_BLOCKS, lam_init=lam_init),
        grid=(H_A, nq),
        in_specs=[pl.BlockSpec(memory_space=pltpu.SMEM),
                  pl.BlockSpec((4, HD_A), lambda h, i: (0, 0)),
                  pl.BlockSpec((HE_A, t), lambda h, i: (h, i)),
                  pl.BlockSpec((s, HE_A), lambda h, i: (0, h)),
                  pl.BlockSpec((HE_A, s), lambda h, i: (h, 0)),
                  pl.BlockSpec((1, HE_A), lambda h, i: (0, 0))],
        out_specs=pl.BlockSpec((t, HE_A), lambda h, i: (i, h)),
        out_shape=jax.ShapeDtypeStruct((s, d), BF16),
        scratch_shapes=[pltpu.VMEM((2, HE_A, t), BF16), pltpu.VMEM((2, t, t), F32),
                        pltpu.VMEM((2, 1, t), F32), pltpu.VMEM((2, 1, t), F32), pltpu.VMEM((2, HE_A, t), F32)],
        compiler_params=_cparams(("arbitrary", "arbitrary")),
        name="attn_prompt",
    )(rel_bias.astype(F32), lam_p, qT, kbf, vT, sub_g)


def _attn_sample_kernel(rb_ref, lamp_ref, q_ref, kc_ref, vc_ref, kn_ref, vn_ref, subg_ref, o_ref,
                        *, past, lam_init):
    t = q_ref.shape[0]
    chunk_shift = CHUNK.bit_length() - 1
    lam = _lam(lamp_ref[...]) + lam_init

    def positions(n_keys, first_key):
        qpos = past + lax.broadcasted_iota(jnp.int32, (t, n_keys), 0)
        kpos = first_key + lax.broadcasted_iota(jnp.int32, (t, n_keys), 1)
        return kpos - qpos, jnp.right_shift(kpos, chunk_shift) <= jnp.right_shift(qpos, chunk_shift)

    rel_c, vis_c = positions(past, 0)
    rel_n, vis_n = positions(t, past)
    for h in range(H_A):
        sl = slice(h * HE_A, (h + 1) * HE_A)
        bias_c = jnp.where(vis_c, _rel_bias_tile(rb_ref, h, rel_c, -(past + t - 1), -1, 0.0, 1.0), NEG_INF)
        bias_n = jnp.where(vis_n, _rel_bias_tile(rb_ref, h, rel_n, -(t - 1), t - 1, 0.0, 1.0), NEG_INF)
        q = q_ref[:, sl]
        lane = lax.broadcasted_iota(jnp.int32, q.shape, 1)
        zero = jnp.zeros_like(q)
        q2 = jnp.concatenate([jnp.where(lane < HD_A, q, zero), jnp.where(lane >= HD_A, q, zero)], axis=0)
        kc = kc_ref[pl.ds(h, past, stride=H_A), :].astype(BF16)
        sc = lax.dot_general(q2, kc, NT_DIMS, preferred_element_type=F32) + jnp.concatenate([bias_c, bias_c], 0)
        sn = (lax.dot_general(q2, kn_ref[:, sl], NT_DIMS, preferred_element_type=F32)
              + jnp.concatenate([bias_n, bias_n], 0))
        m = jnp.maximum(jnp.max(sc, axis=-1, keepdims=True), jnp.max(sn, axis=-1, keepdims=True))
        ec = jnp.exp(sc - m)
        en = jnp.exp(sn - m)
        l = jnp.sum(ec, axis=-1, keepdims=True) + jnp.sum(en, axis=-1, keepdims=True)
        pc = ec / l
        pn = en / l
        ac = pc[0:t] - lam * pc[t:2 * t]
        an = pn[0:t] - lam * pn[t:2 * t]
        vc = vc_ref[pl.ds(h, past, stride=H_A), :].astype(BF16)
        o = _dot(ac.astype(BF16), vc) + _dot(an.astype(BF16), vn_ref[:, sl])
        o_ref[:, sl] = (_rms(o, subg_ref[...]) * (1.0 - lam_init)).astype(BF16)


def _attn_sample(q, kbf, vbf, cache_k, cache_v, layer, rel_bias, lam_p, sub_g, lam_init, b, t):
    r, d = q.shape
    past = cache_k.shape[2]
    rows = pl.BlockSpec((t, d), lambda i: (i, 0))
    cache_k = cache_k.reshape(-1, HE_A)
    cache_v = cache_v.reshape(-1, HE_A)
    cache = pl.BlockSpec((past * H_A, HE_A), lambda i: (layer * b + i, 0))
    return pl.pallas_call(
        functools.partial(_attn_sample_kernel, past=past, lam_init=lam_init),
        grid=(b,),
        in_specs=[pl.BlockSpec(memory_space=pltpu.SMEM),
                  pl.BlockSpec((4, HD_A), lambda i: (0, 0)),
                  rows, cache, cache, rows, rows,
                  pl.BlockSpec((1, HE_A), lambda i: (0, 0))],
        out_specs=rows,
        out_shape=jax.ShapeDtypeStruct((r, d), BF16),
        compiler_params=_cparams(("arbitrary",)),
        name="attn_sample",
    )(rel_bias.astype(F32), lam_p, q, cache_k, cache_v, kbf, vbf, sub_g)


def _mlstm_kernel(q_ref, k_ref, v_ref, om_ref, g_ref, gT_ref, mg_ref, c0_ref, n0_ref, m0_ref, *refs,
                  tl, layer, creates):
    h_ref, c_ref, n_ref, m_ref, c_s, n_s, m_s = refs[-7:]
    step = pl.program_id(1)

    @pl.when(step == 0)
    def _():
        c_s[...] = c0_ref[...]
        n_s[...] = n0_ref[...]
        m_s[...] = m0_ref[...]

    ti = lax.broadcasted_iota(jnp.int32, (tl, tl), 0)
    si = lax.broadcasted_iota(jnp.int32, (tl, tl), 1)
    causal = si <= ti
    gates_c = g_ref[...]
    gates_r = gT_ref[...]
    lf_cols = _log_sigmoid(gates_c)
    lf_rows = _log_sigmoid(gates_r)
    for h in range(H_M):
        sl = slice(h * DH_M, (h + 1) * DH_M)
        q = q_ref[:, sl]
        k = k_ref[:, sl]
        v = v_ref[:, sl]
        ig_c = gates_c[:, h:h + 1]
        lf_c = lf_cols[:, H_M + h:H_M + h + 1]
        ig_r = gates_r[h:h + 1, :]
        lf_r = lf_rows[H_M + h:H_M + h + 1, :]
        m = m_s[h]
        n_row = n_s[h]
        b_c = jnp.sum(jnp.where(causal, lf_r, 0.0), axis=1, keepdims=True)
        b_r = jnp.sum(jnp.where(ti <= si, lf_c, 0.0), axis=0, keepdims=True)
        dmat = jnp.where(causal, b_c - b_r + ig_r, -jnp.inf)
        inter = b_c + m
        m_t = jnp.maximum(inter, jnp.max(dmat, axis=1, keepdims=True))
        w_intra = jnp.exp(dmat - m_t)
        w_inter = jnp.exp(inter - m_t)
        a = w_intra * lax.dot_general(q, k, NT_DIMS, preferred_element_type=F32)
        num = w_inter * _dot(q, c_s[h].astype(BF16)) + _dot(a.astype(BF16), v)
        den = (w_inter * jnp.sum(q.astype(F32) * n_row, axis=1, keepdims=True)
               + jnp.sum(a, axis=1, keepdims=True))
        hh = num / jnp.maximum(jnp.abs(den), jnp.exp(-m_t))
        b_last = b_r[:, tl - 1:tl]
        g_c = b_last - b_c + ig_c
        m_new = jnp.maximum(b_last + m, jnp.max(g_c, axis=0, keepdims=True))
        decay = jnp.exp(b_last + m - m_new)
        kw = jnp.exp(g_c - m_new) * k.astype(F32)
        c_s[h] = decay * c_s[h] + _dot(kw.T.astype(BF16), v)
        n_s[h] = decay * n_row + jnp.sum(kw, axis=0, keepdims=True)
        m_s[h] = m_new
        hn = _rms(hh, mg_ref[:, sl])
        h_ref[:, sl] = (hn * jax.nn.sigmoid(om_ref[:, sl])).astype(BF16)

    @pl.when(step == pl.num_programs(1) - 1)
    def _():
        if creates:
            for l in range(c_ref.shape[0]):
                c_ref[l] = c_s[...] if l == layer else jnp.zeros(c_s.shape, F32)
        else:
            c_ref[...] = c_s[...]
        n_ref[...] = n_s[...]
        m_ref[...] = m_s[...]


def _mlstm(qm, km, vm, om, gates, m_g, c0, c0_layer, n0, m0, b, t, tl, layer, depth, c_states):
    r, d = qm.shape
    nc = t // tl
    gates_t = jnp.transpose(gates.reshape(b * nc, tl, 2 * H_M), (0, 2, 1))
    rows = pl.BlockSpec((tl, d), lambda i, c: (i * nc + c, 0))
    st_n = pl.BlockSpec((None, H_M, 1, DH_M), lambda i, c: (i, 0, 0, 0))
    st_m = pl.BlockSpec((None, H_M, 1, 1), lambda i, c: (i, 0, 0, 0))
    prev = () if c_states is None else (c_states,)
    c_out = (pl.BlockSpec((None, None, H_M, DH_M, DH_M), lambda i, c: (layer, i, 0, 0, 0)) if prev else
             pl.BlockSpec((depth, None, H_M, DH_M, DH_M), lambda i, c: (0, i, 0, 0, 0)))
    h, c_states, n_new, m_new = pl.pallas_call(
        functools.partial(_mlstm_kernel, tl=tl, layer=layer, creates=not prev),
        grid=(b, nc),
        in_specs=[rows, rows, rows, rows,
                  pl.BlockSpec((tl, 2 * H_M), lambda i, c: (i * nc + c, 0)),
                  pl.BlockSpec((None, 2 * H_M, tl), lambda i, c: (i * nc + c, 0, 0)),
                  pl.BlockSpec((1, d), lambda i, c: (0, 0)),
                  pl.BlockSpec((None, None, H_M, DH_M, DH_M), lambda i, c: (c0_layer, i, 0, 0, 0)),
                  st_n, st_m] + [pl.BlockSpec(memory_space=pl.ANY)] * len(prev),
        out_specs=[rows, c_out, st_n, st_m],
        out_shape=[jax.ShapeDtypeStruct((r, d), BF16),
                   jax.ShapeDtypeStruct((depth, b, H_M, DH_M, DH_M), F32),
                   jax.ShapeDtypeStruct((b, H_M, 1, DH_M), F32),
                   jax.ShapeDtypeStruct((b, H_M, 1, 1), F32)],
        input_output_aliases={10: 1} if prev else {},
        scratch_shapes=[pltpu.VMEM((H_M, DH_M, DH_M), F32), pltpu.VMEM((H_M, 1, DH_M), F32),
                        pltpu.VMEM((H_M, 1, 1), F32)],
        compiler_params=_cparams(("arbitrary", "arbitrary")),
        name="mlstm",
    )(qm, km, vm, om, gates, gates_t, m_g, c0, n0.reshape(b, H_M, 1, DH_M), m0.reshape(b, H_M, 1, 1), *prev)
    return h, c_states, n_new.reshape(b, H_M, DH_M), m_new.reshape(b, H_M)


def _merge_kernel(oa_ref, hm_ref, gts_ref, x_ref, g1_ref, wa_ref, wm_ref, wo_ref, o_ref, *, d):
    ya = _dot(oa_ref[...], wa_ref[...])
    ym = _dot(hm_ref[...], wm_ref[...])
    mix = jax.nn.sigmoid(gts_ref[:, 0:d]) * ya + jax.nn.sigmoid(gts_ref[:, d:2 * d]) * ym
    o_ref[...] = x_ref[...] + g1_ref[...] * _dot(mix.astype(BF16), wo_ref[...])


def _merge(oa, hm, gts, x, mod, per_row, wa, wm, wo):
    r, d = x.shape
    tm = min(r, ROW_TILE)
    nat = pl.BlockSpec((tm, d), lambda i: (i, 0))
    wspec = pl.BlockSpec((d, d), lambda i: (0, 0))
    return pl.pallas_call(
        functools.partial(_merge_kernel, d=d),
        grid=(r // tm,),
        in_specs=[nat, nat, pl.BlockSpec((tm, 2 * d), lambda i: (i, 0)), nat,
                  _mod_spec(per_row, tm, d, 2, lambda i: i), wspec, wspec, wspec],
        out_specs=nat,
        out_shape=jax.ShapeDtypeStruct((r, d), F32),
        compiler_params=_cparams(("arbitrary",)),
        name="merge_out",
    )(oa, hm, gts, x, mod, wa, wm, wo)


def _ffn_kernel(x_ref, g_ref, sc_ref, sh_ref, g2_ref, wu_ref, cw_ref, cb_ref, wd_ref, buf_ref, fg_ref,
                o_ref, nb_ref, tail_s, *, tm, dff, nc, final):
    first = pl.program_id(1) == 0
    ck = dff // nc
    x = x_ref[...]
    xn = (_rms(x, g_ref[...]) * (1.0 + sc_ref[...]) + sh_ref[...]).astype(BF16)
    rowi = lax.broadcasted_iota(jnp.int32, (tm, ck), 0)
    acc = None
    for c in range(nc):
        cs = slice(c * ck, (c + 1) * ck)
        u = _dot(xn, wu_ref[:, cs])
        gate = _dot(xn, wu_ref[:, dff + c * ck:dff + (c + 1) * ck])
        prev2 = jnp.where(first, buf_ref[0:1, cs], tail_s[6:7, cs])
        prev1 = jnp.where(first, buf_ref[1:2, cs], tail_s[7:8, cs])
        u1 = jnp.where(rowi == 0, prev1, pltpu.roll(u, 1, 0))
        u2 = jnp.where(rowi == 0, prev2, jnp.where(rowi == 1, prev1, pltpu.roll(u, 2, 0)))
        conv = cb_ref[:, cs] + cw_ref[0:1, cs] * u2 + cw_ref[1:2, cs] * u1 + cw_ref[2:3, cs] * u
        part = _dot((jax.nn.gelu(conv) * gate).astype(BF16), wd_ref[cs, :])
        acc = part if acc is None else acc + part
        tail_s[:, cs] = u[tm - 8:tm, :]
        nb_ref[:, cs] = u[tm - (CONV_W - 1):tm, :]
    y = x + g2_ref[...] * acc
    if final:
        y = _rms(y, fg_ref[...])
    o_ref[...] = y


def _ffn(x, mod, per_row, g, w_up, conv_w, conv_b, w_down, conv_buf, final_g, b, t, final):
    r, d = x.shape
    dff = w_down.shape[0]
    tm = min(t, ROW_TILE)
    nt = t // tm
    nc = 2
    assert dff % (nc * 128) == 0 and tm >= 8
    row = lambda i, j: i * nt + j
    nat = pl.BlockSpec((tm, d), lambda i, j: (i * nt + j, 0))
    whole = lambda a: pl.BlockSpec(a.shape, lambda i, j: (0,) * a.ndim)
    y, tails = pl.pallas_call(
        functools.partial(_ffn_kernel, tm=tm, dff=dff, nc=nc, final=final),
        grid=(b, nt),
        in_specs=[nat, whole(g), _mod_spec(per_row, tm, d, 4, row), _mod_spec(per_row, tm, d, 3, row),
                  _mod_spec(per_row, tm, d, 5, row), whole(w_up), whole(conv_w), whole(conv_b), whole(w_down),
                  pl.BlockSpec((None, CONV_W - 1, dff), lambda i, j: (i, 0, 0)), whole(final_g)],
        out_specs=[nat, pl.BlockSpec((None, None, CONV_W - 1, dff), lambda i, j: (i, j, 0, 0))],
        out_shape=[jax.ShapeDtypeStruct((r, d), F32), jax.ShapeDtypeStruct((b, nt, CONV_W - 1, dff), F32)],
        scratch_shapes=[pltpu.VMEM((8, dff), F32)],
        compiler_params=_cparams(("arbitrary", "arbitrary")),
        name="conv_ffn",
    )(x, g, mod, mod, mod, w_up, conv_w, conv_b, w_down, conv_buf, final_g)
    return y, tails[:, nt - 1]


def kernel(x_prompt, x_sample, c_prompt, c_sample, cache_k, cache_v, state_C, state_n, state_m, state_conv,
           rel_bias, ada_w, ada_b, norm1_g, norm2_g, w_in, b_if, lam_p, attn_sub_g, mlstm_g, w_br_a, w_br_m,
           w_out, w_up, conv_w, conv_b, w_down, final_g):
    bp, s, d = x_prompt.shape
    bs, ts, _ = x_sample.shape
    depth = w_in.shape[0]
    dff = w_down.shape[1]
    assert bp == 1 and d == H_A * HE_A == H_M * DH_M

    c_all = jnp.concatenate([c_prompt, c_sample, jnp.zeros((16 - bp - bs, d), F32)], axis=0)
    mod = _ada(c_all, ada_w, ada_b)

    xp = x_prompt.reshape(bp * s, d)
    xs = x_sample.reshape(bs * ts, d)
    tl_p = min(MLSTM_TILE, s)
    tl_s = min(CHUNK, ts)
    zero_c = jnp.zeros((1, bp, H_M, DH_M, DH_M), F32)
    zero_n = jnp.zeros((bp, H_M, DH_M), F32)
    zero_m = jnp.zeros((bp, H_M), F32)
    zero_buf = jnp.zeros((bp, CONV_W - 1, dff), F32)
    st_p, st_s = [], []
    kv_p = kv_s = c_p = c_s = None
    for l in range(depth):
        lam_init = 0.8 - 0.6 * math.exp(-0.3 * l)
        last = l == depth - 1
        w_qkv = w_in[l, :, 0:3 * d].astype(BF16)
        w_m = w_in[l, :, 3 * d:9 * d].astype(BF16)
        w_if = jnp.pad(w_in[l, :, 9 * d:], ((0, 0), (0, 128 - 2 * H_M))).astype(BF16)
        bif = jnp.pad(b_if[l], (0, 128 - 2 * H_M)).reshape(1, 128)
        wa, wm, wo = w_br_a[l].astype(BF16), w_br_m[l].astype(BF16), w_out[l].astype(BF16)
        wu, wd = w_up[l].astype(BF16), w_down[l].astype(BF16)
        g1, g2 = norm1_g[l].reshape(1, d), norm2_g[l].reshape(1, d)
        sub_g, m_g = attn_sub_g[l].reshape(1, HE_A), mlstm_g[l].reshape(1, d)
        cb = conv_b[l].reshape(1, dff)
        fg = final_g.reshape(1, d)

        mod_p = mod[l, 0:bp]
        qT, k_all, kbf, v_all, vT = _in_attn(xp, mod_p, False, g1, w_qkv, True, l, depth, kv_p)
        kv_p = (k_all, v_all)
        qm, km, vm, om, gts, gates = _in_mlstm(xp, mod_p, False, g1, w_m, w_if, bif)
        oa = _attn_prompt(qT, kbf, vT, rel_bias, lam_p[l], sub_g, lam_init, min(ATTN_TILE, s))
        hm, c_p, n_new, m_new = _mlstm(qm, km, vm, om, gates, m_g, zero_c, 0, zero_n, zero_m, bp, s, tl_p,
                                       l, depth, c_p)
        xp = _merge(oa, hm, gts, xp, mod_p, False, wa, wm, wo)
        xp, buf_new = _ffn(xp, mod_p, False, g2, wu, conv_w[l], cb, wd, zero_buf, fg, bp, s, last)
        st_p.append((n_new, m_new, buf_new))

        mod_s = jnp.repeat(mod[l, bp:bp + bs], ts, axis=0)
        q, k_all, kbf, v_all, vbf = _in_attn(xs, mod_s, True, g1, w_qkv, False, l, depth, kv_s)
        kv_s = (k_all, v_all)
        qm, km, vm, om, gts, gates = _in_mlstm(xs, mod_s, True, g1, w_m, w_if, bif)
        oa = _attn_sample(q, kbf, vbf, cache_k, cache_v, l, rel_bias, lam_p[l], sub_g, lam_init, bs, ts)
        hm, c_s, n_new, m_new = _mlstm(qm, km, vm, om, gates, m_g, state_C, l, state_n[l], state_m[l], bs, ts, tl_s,
                                       l, depth, c_s)
        xs = _merge(oa, hm, gts, xs, mod_s, True, wa, wm, wo)
        xs, buf_new = _ffn(xs, mod_s, True, g2, wu, conv_w[l], cb, wd, state_conv[l], fg, bs, ts, last)
        st_s.append((n_new, m_new, buf_new))

    outs_p = [a.reshape(depth, bp, s, H_A, HE_A) for a in kv_p] + [c_p] + [jnp.stack(a) for a in zip(*st_p)]
    outs_s = [a.reshape(depth, bs, ts, H_A, HE_A) for a in kv_s] + [c_s] + [jnp.stack(a) for a in zip(*st_s)]
    return (xp.reshape(bp, s, d), xs.reshape(bs, ts, d), *outs_p, *outs_s)
```

```python
import functools
import math

import numpy as np
import jax
import jax.numpy as jnp
from jax import lax
from jax.experimental import pallas as pl
from jax.experimental.pallas import tpu as pltpu

F32 = jnp.float32
BF16 = jnp.bfloat16

CHUNK = 64
H_A = 8
HD_A = 64
HE_A = 2 * HD_A
H_M = 4
DH_M = 256
N_BUCKETS = 32
CONV_W = 3
NEG_INF = -1e30
EPS = 1e-6
FAR_BUCKET = N_BUCKETS // 2 - 1
LOG2E = math.log2(math.e)

V7X_VMEM_LIMIT = 56 * 1024 * 1024
ROW_TILE = 512
ATTN_TILE = 512
ATTN_FAR_BLOCKS = 4
ATTN_TILES_PER_STEP = 8
BF16_ROWS = 16
MLSTM_TILE = 256

NT_DIMS = (((1,), (1,)), ((), ()))


def _cparams(sem):
    return pltpu.CompilerParams(dimension_semantics=sem, vmem_limit_bytes=V7X_VMEM_LIMIT)


def _dot(a, b):
    return jnp.dot(a, b, preferred_element_type=F32)


def _rms(x, g):
    return x * lax.rsqrt(jnp.mean(x * x, axis=-1, keepdims=True) + EPS) * g


def _log_sigmoid(x):
    return jnp.minimum(x, 0.0) - jnp.log1p(jnp.exp(-jnp.abs(x)))


def _mod_spec(per_row, tm, d, col, row_index):
    if per_row:
        return pl.BlockSpec((tm, d), lambda *g: (row_index(*g), col))
    return pl.BlockSpec((1, d), lambda *g: (0, col))


def _ada_kernel(c_ref, w_ref, b_ref, o_ref):
    a = jax.nn.silu(c_ref[...]).astype(BF16)
    o_ref[...] = _dot(a, w_ref[...].astype(BF16)) + b_ref[...]


def _ada(c_all, ada_w, ada_b):
    depth, d, n = ada_w.shape
    rc = c_all.shape[0]
    tn = 1536
    return pl.pallas_call(
        _ada_kernel,
        grid=(depth, n // tn),
        in_specs=[pl.BlockSpec((rc, d), lambda l, j: (0, 0)),
                  pl.BlockSpec((None, d, tn), lambda l, j: (l, 0, j)),
                  pl.BlockSpec((None, 1, tn), lambda l, j: (l, 0, j))],
        out_specs=pl.BlockSpec((None, rc, tn), lambda l, j: (l, 0, j)),
        out_shape=jax.ShapeDtypeStruct((depth, rc, n), F32),
        compiler_params=_cparams(("arbitrary", "arbitrary")),
        name="ada_mod",
    )(c_all, ada_w, ada_b.reshape(depth, 1, n))


def _in_attn_kernel(x_ref, g_ref, sc_ref, sh_ref, w_ref, *refs, d, transposed, layer, creates):
    q_ref, k32_ref, kbf_ref, v32_ref, vbf_ref = refs[-5:]
    xn = (_rms(x_ref[...], g_ref[...]) * (1.0 + sc_ref[...]) + sh_ref[...]).astype(BF16)
    q = _dot(xn, w_ref[:, 0:d]) * (HD_A ** -0.5 * (LOG2E if transposed else 1.0))
    k = _dot(xn, w_ref[:, d:2 * d])
    v = _dot(xn, w_ref[:, 2 * d:3 * d])
    if creates:
        for l in range(k32_ref.shape[0]):
            k32_ref[l] = k if l == layer else jnp.zeros_like(k)
            v32_ref[l] = v if l == layer else jnp.zeros_like(v)
    else:
        k32_ref[...] = k
        v32_ref[...] = v
    kbf_ref[...] = k.astype(BF16)
    if transposed:
        q_ref[...] = q.T.astype(BF16)
        vbf_ref[...] = v.T.astype(BF16)
    else:
        q_ref[...] = q.astype(BF16)
        vbf_ref[...] = v.astype(BF16)


def _in_attn(x, mod, per_row, g, w_qkv, transposed, layer, depth, kv_states):
    r, d = x.shape
    tm = min(r, ROW_TILE)
    row = lambda i: i
    nat = pl.BlockSpec((tm, d), lambda i: (i, 0))
    tr = pl.BlockSpec((d, tm), lambda i: (0, i))
    prev = () if kv_states is None else tuple(kv_states)
    state = (pl.BlockSpec((None, tm, d), lambda i: (layer, i, 0)) if prev else
             pl.BlockSpec((depth, tm, d), lambda i: (0, i, 0)))
    state_shape = jax.ShapeDtypeStruct((depth, r, d), F32)
    bf_shape = jax.ShapeDtypeStruct((d, r) if transposed else (r, d), BF16)
    return pl.pallas_call(
        functools.partial(_in_attn_kernel, d=d, transposed=transposed, layer=layer, creates=not prev),
        grid=(r // tm,),
        in_specs=[nat, pl.BlockSpec((1, d), lambda i: (0, 0)),
                  _mod_spec(per_row, tm, d, 1, row), _mod_spec(per_row, tm, d, 0, row),
                  pl.BlockSpec((d, 3 * d), lambda i: (0, 0))] + [pl.BlockSpec(memory_space=pl.ANY)] * len(prev),
        out_specs=[tr if transposed else nat, state, nat, state, tr if transposed else nat],
        out_shape=[bf_shape, state_shape, jax.ShapeDtypeStruct((r, d), BF16), state_shape, bf_shape],
        input_output_aliases={5: 1, 6: 3} if prev else {},
        compiler_params=_cparams(("arbitrary",)),
        name="in_attn",
    )(x, g, mod, mod, w_qkv, *prev)


def _in_mlstm_kernel(x_ref, g_ref, sc_ref, sh_ref, w_ref, wif_ref, bif_ref,
                     qm_ref, km_ref, vm_ref, om_ref, gts_ref, gate_ref, *, d):
    xn = (_rms(x_ref[...], g_ref[...]) * (1.0 + sc_ref[...]) + sh_ref[...]).astype(BF16)
    qm_ref[...] = _dot(xn, w_ref[:, 0:d]).astype(BF16)
    km_ref[...] = (_dot(xn, w_ref[:, d:2 * d]) * (DH_M ** -0.5)).astype(BF16)
    vm_ref[...] = _dot(xn, w_ref[:, 2 * d:3 * d]).astype(BF16)
    om_ref[...] = _dot(xn, w_ref[:, 3 * d:4 * d])
    gts_ref[...] = _dot(xn, w_ref[:, 4 * d:6 * d])
    pre = _dot(xn, wif_ref[...]) + bif_ref[...]
    gate_ref[...] = pre[:, 0:2 * H_M]


def _in_mlstm(x, mod, per_row, g, w_m, w_if, b_if):
    r, d = x.shape
    tm = min(r, ROW_TILE)
    row = lambda i: i
    nat = pl.BlockSpec((tm, d), lambda i: (i, 0))
    const = lambda shape: pl.BlockSpec(shape, lambda i: (0, 0))
    return pl.pallas_call(
        functools.partial(_in_mlstm_kernel, d=d),
        grid=(r // tm,),
        in_specs=[nat, const((1, d)), _mod_spec(per_row, tm, d, 1, row), _mod_spec(per_row, tm, d, 0, row),
                  const(w_m.shape), const(w_if.shape), const(b_if.shape)],
        out_specs=[nat, nat, nat, nat, pl.BlockSpec((tm, 2 * d), lambda i: (i, 0)),
                   pl.BlockSpec((tm, 2 * H_M), lambda i: (i, 0))],
        out_shape=[jax.ShapeDtypeStruct((r, d), BF16)] * 3
        + [jax.ShapeDtypeStruct((r, d), F32), jax.ShapeDtypeStruct((r, 2 * d), F32),
           jax.ShapeDtypeStruct((r, 2 * H_M), F32)],
        compiler_params=_cparams(("arbitrary",)),
        name="in_mlstm",
    )(x, g, mod, mod, w_m, w_if, b_if)


def _bucket_np(rel):
    half = N_BUCKETS // 2
    max_exact = half // 2
    n = np.abs(rel)
    thresholds = np.ceil(max_exact * 2.0 ** (np.arange(1, half - max_exact) / 2.0)).astype(np.int64)
    large = max_exact + (n[..., None] >= thresholds).sum(-1)
    return np.where(rel > 0, half, 0) + np.where(n < max_exact, n, np.minimum(large, half - 1))


def _lam(lp):
    s1 = jnp.sum(lp[0:1] * lp[1:2], axis=-1, keepdims=True)
    s2 = jnp.sum(lp[2:3] * lp[3:4], axis=-1, keepdims=True)
    return jnp.exp(s1) - jnp.exp(s2)


def _bias_steps(lo, hi):
    rel = np.arange(lo, hi + 1)
    b = _bucket_np(rel)
    starts = np.concatenate([[0], np.nonzero(b[1:] != b[:-1])[0] + 1])
    return [(int(rel[k]), int(b[k])) for k in starts]


def _rel_bias_tile(rb_ref, h, rel, lo, hi, shift, scale):
    steps = _bias_steps(lo, hi)
    v = jnp.full(rel.shape, (rb_ref[steps[0][1], h] - shift) * scale, F32)
    for start, bucket in steps[1:]:
        v = jnp.where(rel >= start, (rb_ref[bucket, h] - shift) * scale, v)
    return v


def _attn_prompt_tile(rb_ref, lamp_ref, qT_ref, k_ref, vT_ref, subg_ref, o_ref,
                      q_s, bias_s, m_s, l_s, acc_s, *, i, qsl, t, far_blocks, lam_init):
    h = pl.program_id(0)
    rows = 64
    chunk_shift = CHUNK.bit_length() - 1

    @pl.when(i == 0)
    def _():
        far = rb_ref[FAR_BUCKET, h]
        for tile, shift in enumerate((t, 0)):

            def fill(r, carry, tile=tile, shift=shift):
                r0 = pl.multiple_of(r * rows, rows)
                kk = lax.broadcasted_iota(jnp.int32, (rows, t), 0) + r0
                qq = lax.broadcasted_iota(jnp.int32, (rows, t), 1)
                v = _rel_bias_tile(rb_ref, h, kk - qq - shift, -shift - (t - 1), -shift + (t - 1), far, LOG2E)
                if tile == 1:
                    visible = jnp.right_shift(kk, chunk_shift) <= jnp.right_shift(qq, chunk_shift)
                    v = jnp.where(visible, v, NEG_INF)
                bias_s[tile, pl.ds(r0, rows), :] = v
                return carry

            lax.fori_loop(0, t // rows, fill, 0)

    lam = _lam(lamp_ref[...]) + lam_init
    qT = qT_ref[:, qsl]
    rowi = lax.broadcasted_iota(jnp.int32, qT.shape, 0)
    zero = jnp.zeros_like(qT)
    q_s[0] = jnp.where(rowi < HD_A, qT, zero)
    q_s[1] = jnp.where(rowi >= HD_A, qT, zero)

    def keys(j0, n):
        off = pl.multiple_of(j0 * t, t)
        return k_ref[pl.ds(off, n), :], vT_ref[:, pl.ds(off, n)]

    def scores(kb, c, tile):
        s = _dot(kb, q_s[c])
        return s if tile is None else s + bias_s[tile]

    def fixed_shift_block(j0, n, tile):
        kb, vb = keys(j0, n)
        for c in range(2):
            e = jnp.exp2(scores(kb, c, tile) - m_s[c])
            l_s[c] += jnp.sum(e, axis=0, keepdims=True)
            acc_s[c] += _dot(vb, e.astype(BF16))

    def running_max_block(j0, tile):
        kb, vb = keys(j0, t)
        for c in range(2):
            s = scores(kb, c, tile)
            m_old = m_s[c]
            m_new = jnp.maximum(m_old, jnp.max(s, axis=0, keepdims=True))
            alpha = jnp.exp2(m_old - m_new)
            e = jnp.exp2(s - m_new)
            l_s[c] = alpha * l_s[c] + jnp.sum(e, axis=0, keepdims=True)
            acc_s[c] = alpha * acc_s[c] + _dot(vb, e.astype(BF16))
            m_s[c] = m_new

    def finish():
        l0, l1 = l_s[0], l_s[1]
        oT = acc_s[0] * (1.0 / l0) - acc_s[1] * (lam / l1)
        inv = lax.rsqrt(jnp.mean(oT * oT, axis=0, keepdims=True) + EPS)
        o_ref[qsl, :] = ((oT * inv).T * (subg_ref[...] * (1.0 - lam_init))).astype(BF16)
        big = float(np.finfo(np.float32).max)
        bad = jnp.max(jnp.where(jnp.abs(oT) <= big, 0.0, 1.0), axis=0, keepdims=True)
        return jnp.max(jnp.maximum(bad, jnp.where((l0 <= big) & (l1 <= big), 0.0, 1.0)))

    def first_step(r):
        nb = max(r, 0) + (2 if r >= 0 else 1)
        kb, vb = keys(i - (nb - 1), nb * t)
        for c in range(2):
            s = _dot(kb, q_s[c])
            parts = [s[0:r * t]] if r > 0 else []
            if r >= 0:
                parts.append(s[(nb - 2) * t:(nb - 1) * t] + bias_s[0])
            parts.append(s[(nb - 1) * t:nb * t] + bias_s[1])
            e = [jnp.exp2(p - m_s[c]) for p in parts]
            l_s[c] = functools.reduce(jnp.add, [jnp.sum(p, axis=0, keepdims=True) for p in e])
            e = [p.astype(BF16) for p in e]
            acc_s[c] = _dot(vb, e[0] if len(e) == 1 else jnp.concatenate(e, axis=0))

    k0 = k_ref[pl.ds(pl.multiple_of(i * t, t), CHUNK), :]
    for c in range(2):
        m_s[c] = jnp.max(_dot(k0, q_s[c]) + bias_s[1, 0:CHUNK, :], axis=0, keepdims=True)

    nfar = jnp.maximum(i - 1, 0)
    folded = jnp.bitwise_and(nfar, far_blocks - 1)

    @pl.when(i == 0)
    def _():
        first_step(-1)

    for r in range(far_blocks):
        @pl.when((i >= 1) & (folded == r))
        def _(r=r):
            first_step(r)

    groups = lax.div(nfar, far_blocks)
    odd = jnp.bitwise_and(groups, 1)

    @pl.when(odd == 1)
    def _():
        fixed_shift_block(0, far_blocks * t, None)

    def big_step(p, carry):
        fixed_shift_block((odd + 2 * p) * far_blocks, 2 * far_blocks * t, None)
        return carry

    lax.fori_loop(0, lax.shift_right_logical(groups, 1), big_step, 0)

    @pl.when(finish() > 0.0)
    def _():
        m_s[...] = jnp.full(m_s.shape, NEG_INF, F32)
        l_s[...] = jnp.zeros(l_s.shape, F32)
        acc_s[...] = jnp.zeros(acc_s.shape, F32)

        def step(j, carry):
            running_max_block(j, None)
            return carry

        lax.fori_loop(0, nfar, step, 0)

        @pl.when(i >= 1)
        def _():
            running_max_block(i - 1, 0)

        running_max_block(i, 1)
        finish()


def _attn_prompt_kernel(*refs, t, tiles, far_blocks, lam_init):
    def one_tile(n, carry):
        _attn_prompt_tile(*refs, i=pl.program_id(1) * tiles + n, qsl=pl.ds(pl.multiple_of(n * t, t), t),
                          t=t, far_blocks=far_blocks, lam_init=lam_init)
        return carry

    lax.fori_loop(0, tiles, one_tile, 0)


def _attn_prompt(qT, kbf, vT, rel_bias, lam_p, sub_g, lam_init, t):
    d, s = qT.shape
    assert s % t == 0 and t % 64 == 0 and t % CHUNK == 0 and CHUNK & (CHUNK - 1) == 0
    assert _bucket_np(np.array([-(t + 1)]))[0] == FAR_BUCKET
    assert ATTN_FAR_BLOCKS & (ATTN_FAR_BLOCKS - 1) == 0
    nq = s // t
    tiles = math.gcd(nq, ATTN_TILES_PER_STEP)
    return pl.pallas_call(
        functools.partial(_attn_prompt_kernel, t=t, tiles=tiles, far_blocks=ATTN_FAR_BLOCKS, lam_init=lam_init),
        grid=(H_A, nq // tiles),
        in_specs=[pl.BlockSpec(memory_space=pltpu.SMEM),
                  pl.BlockSpec((4, HD_A), lambda h, i: (0, 0)),
                  pl.BlockSpec((HE_A, tiles * t), lambda h, i: (h, i)),
                  pl.BlockSpec((s, HE_A), lambda h, i: (0, h)),
                  pl.BlockSpec((HE_A, s), lambda h, i: (h, 0)),
                  pl.BlockSpec((1, HE_A), lambda h, i: (0, 0))],
        out_specs=pl.BlockSpec((tiles * t, HE_A), lambda h, i: (i, h)),
        out_shape=jax.ShapeDtypeStruct((s, d), BF16),
        scratch_shapes=[pltpu.VMEM((2, HE_A, t), BF16), pltpu.VMEM((2, t, t), F32),
                        pltpu.VMEM((2, 1, t), F32), pltpu.VMEM((2, 1, t), F32), pltpu.VMEM((2, HE_A, t), F32)],
        compiler_params=_cparams(("arbitrary", "arbitrary")),
        name="attn_prompt",
    )(rel_bias.astype(F32), lam_p, qT, kbf, vT, sub_g)


def _attn_sample_kernel(rb_ref, lamp_ref, q_ref, kc_ref, vc_ref, kn_ref, vn_ref, subg_ref, o_ref,
                        *, past, lam_init):
    t = q_ref.shape[0]
    chunk_shift = CHUNK.bit_length() - 1
    lam = _lam(lamp_ref[...]) + lam_init

    def positions(n_keys, first_key):
        qpos = past + lax.broadcasted_iota(jnp.int32, (t, n_keys), 0)
        kpos = first_key + lax.broadcasted_iota(jnp.int32, (t, n_keys), 1)
        return kpos - qpos, jnp.right_shift(kpos, chunk_shift) <= jnp.right_shift(qpos, chunk_shift)

    rel_c, vis_c = positions(past, 0)
    rel_n, vis_n = positions(t, past)
    for h in range(H_A):
        sl = slice(h * HE_A, (h + 1) * HE_A)
        bias_c = jnp.where(vis_c, _rel_bias_tile(rb_ref, h, rel_c, -(past + t - 1), -1, 0.0, 1.0), NEG_INF)
        bias_n = jnp.where(vis_n, _rel_bias_tile(rb_ref, h, rel_n, -(t - 1), t - 1, 0.0, 1.0), NEG_INF)
        q = q_ref[:, sl]
        lane = lax.broadcasted_iota(jnp.int32, q.shape, 1)
        zero = jnp.zeros_like(q)
        q2 = jnp.concatenate([jnp.where(lane < HD_A, q, zero), jnp.where(lane >= HD_A, q, zero)], axis=0)
        kc = kc_ref[pl.ds(h, past, stride=H_A), :].astype(BF16)
        sc = lax.dot_general(q2, kc, NT_DIMS, preferred_element_type=F32) + jnp.concatenate([bias_c, bias_c], 0)
        sn = (lax.dot_general(q2, kn_ref[:, sl], NT_DIMS, preferred_element_type=F32)
              + jnp.concatenate([bias_n, bias_n], 0))
        m = jnp.maximum(jnp.max(sc, axis=-1, keepdims=True), jnp.max(sn, axis=-1, keepdims=True))
        ec = jnp.exp(sc - m)
        en = jnp.exp(sn - m)
        l = jnp.sum(ec, axis=-1, keepdims=True) + jnp.sum(en, axis=-1, keepdims=True)
        pc = ec / l
        pn = en / l
        ac = pc[0:t] - lam * pc[t:2 * t]
        an = pn[0:t] - lam * pn[t:2 * t]
        vc = vc_ref[pl.ds(h, past, stride=H_A), :].astype(BF16)
        o = _dot(ac.astype(BF16), vc) + _dot(an.astype(BF16), vn_ref[:, sl])
        o_ref[:, sl] = (_rms(o, subg_ref[...]) * (1.0 - lam_init)).astype(BF16)


def _attn_sample(q, kbf, vbf, cache_k, cache_v, layer, rel_bias, lam_p, sub_g, lam_init, b, t):
    r, d = q.shape
    past = cache_k.shape[2]
    rows = pl.BlockSpec((t, d), lambda i: (i, 0))
    cache_k = cache_k.reshape(-1, HE_A)
    cache_v = cache_v.reshape(-1, HE_A)
    cache = pl.BlockSpec((past * H_A, HE_A), lambda i: (layer * b + i, 0))
    return pl.pallas_call(
        functools.partial(_attn_sample_kernel, past=past, lam_init=lam_init),
        grid=(b,),
        in_specs=[pl.BlockSpec(memory_space=pltpu.SMEM),
                  pl.BlockSpec((4, HD_A), lambda i: (0, 0)),
                  rows, cache, cache, rows, rows,
                  pl.BlockSpec((1, HE_A), lambda i: (0, 0))],
        out_specs=rows,
        out_shape=jax.ShapeDtypeStruct((r, d), BF16),
        compiler_params=_cparams(("arbitrary",)),
        name="attn_sample",
    )(rel_bias.astype(F32), lam_p, q, cache_k, cache_v, kbf, vbf, sub_g)


def _mlstm_kernel(q_ref, k_ref, v_ref, om_ref, g_ref, gT_ref, mg_ref, c0_ref, n0_ref, m0_ref, *refs,
                  tl, layer, creates):
    h_ref, c_ref, n_ref, m_ref, c_s, n_s, m_s = refs[-7:]
    step = pl.program_id(1)

    @pl.when(step == 0)
    def _():
        c_s[...] = c0_ref[...]
        n_s[...] = n0_ref[...]
        m_s[...] = m0_ref[...]

    ti = lax.broadcasted_iota(jnp.int32, (tl, tl), 0)
    si = lax.broadcasted_iota(jnp.int32, (tl, tl), 1)
    causal = si <= ti
    gates_c = g_ref[...]
    gates_r = gT_ref[...]
    lf_cols = _log_sigmoid(gates_c)
    lf_rows = _log_sigmoid(gates_r)
    for h in range(H_M):
        sl = slice(h * DH_M, (h + 1) * DH_M)
        q = q_ref[:, sl]
        k = k_ref[:, sl]
        v = v_ref[:, sl]
        ig_c = gates_c[:, h:h + 1]
        lf_c = lf_cols[:, H_M + h:H_M + h + 1]
        ig_r = gates_r[h:h + 1, :]
        lf_r = lf_rows[H_M + h:H_M + h + 1, :]
        m = m_s[h]
        n_row = n_s[h]
        b_c = jnp.sum(jnp.where(causal, lf_r, 0.0), axis=1, keepdims=True)
        b_r = jnp.sum(jnp.where(ti <= si, lf_c, 0.0), axis=0, keepdims=True)
        dmat = jnp.where(causal, b_c - b_r + ig_r, -jnp.inf)
        inter = b_c + m
        m_t = jnp.maximum(inter, jnp.max(dmat, axis=1, keepdims=True))
        w_intra = jnp.exp(dmat - m_t)
        w_inter = jnp.exp(inter - m_t)
        a = w_intra * lax.dot_general(q, k, NT_DIMS, preferred_element_type=F32)
        num = w_inter * _dot(q, c_s[h].astype(BF16)) + _dot(a.astype(BF16), v)
        den = (w_inter * jnp.sum(q.astype(F32) * n_row, axis=1, keepdims=True)
               + jnp.sum(a, axis=1, keepdims=True))
        hh = num / jnp.maximum(jnp.abs(den), jnp.exp(-m_t))
        b_last = b_r[:, tl - 1:tl]
        g_c = b_last - b_c + ig_c
        m_new = jnp.maximum(b_last + m, jnp.max(g_c, axis=0, keepdims=True))
        decay = jnp.exp(b_last + m - m_new)
        kw = jnp.exp(g_c - m_new) * k.astype(F32)
        c_s[h] = decay * c_s[h] + _dot(kw.T.astype(BF16), v)
        n_s[h] = decay * n_row + jnp.sum(kw, axis=0, keepdims=True)
        m_s[h] = m_new
        hn = _rms(hh, mg_ref[:, sl])
        h_ref[:, sl] = (hn * jax.nn.sigmoid(om_ref[:, sl])).astype(BF16)

    @pl.when(step == pl.num_programs(1) - 1)
    def _():
        if creates:
            for l in range(c_ref.shape[0]):
                c_ref[l] = c_s[...] if l == layer else jnp.zeros(c_s.shape, F32)
        else:
            c_ref[...] = c_s[...]
        n_ref[...] = n_s[...]
        m_ref[...] = m_s[...]


def _mlstm(qm, km, vm, om, gates, m_g, c0, c0_layer, n0, m0, b, t, tl, layer, depth, c_states):
    r, d = qm.shape
    nc = t // tl
    gates_t = jnp.transpose(gates.reshape(b * nc, tl, 2 * H_M), (0, 2, 1))
    rows = pl.BlockSpec((tl, d), lambda i, c: (i * nc + c, 0))
    st_n = pl.BlockSpec((None, H_M, 1, DH_M), lambda i, c: (i, 0, 0, 0))
    st_m = pl.BlockSpec((None, H_M, 1, 1), lambda i, c: (i, 0, 0, 0))
    prev = () if c_states is None else (c_states,)
    c_out = (pl.BlockSpec((None, None, H_M, DH_M, DH_M), lambda i, c: (layer, i, 0, 0, 0)) if prev else
             pl.BlockSpec((depth, None, H_M, DH_M, DH_M), lambda i, c: (0, i, 0, 0, 0)))
    h, c_states, n_new, m_new = pl.pallas_call(
        functools.partial(_mlstm_kernel, tl=tl, layer=layer, creates=not prev),
        grid=(b, nc),
        in_specs=[rows, rows, rows, rows,
                  pl.BlockSpec((tl, 2 * H_M), lambda i, c: (i * nc + c, 0)),
                  pl.BlockSpec((None, 2 * H_M, tl), lambda i, c: (i * nc + c, 0, 0)),
                  pl.BlockSpec((1, d), lambda i, c: (0, 0)),
                  pl.BlockSpec((None, None, H_M, DH_M, DH_M), lambda i, c: (c0_layer, i, 0, 0, 0)),
                  st_n, st_m] + [pl.BlockSpec(memory_space=pl.ANY)] * len(prev),
        out_specs=[rows, c_out, st_n, st_m],
        out_shape=[jax.ShapeDtypeStruct((r, d), BF16),
                   jax.ShapeDtypeStruct((depth, b, H_M, DH_M, DH_M), F32),
                   jax.ShapeDtypeStruct((b, H_M, 1, DH_M), F32),
                   jax.ShapeDtypeStruct((b, H_M, 1, 1), F32)],
        input_output_aliases={10: 1} if prev else {},
        scratch_shapes=[pltpu.VMEM((H_M, DH_M, DH_M), F32), pltpu.VMEM((H_M, 1, DH_M), F32),
                        pltpu.VMEM((H_M, 1, 1), F32)],
        compiler_params=_cparams(("arbitrary", "arbitrary")),
        name="mlstm",
    )(qm, km, vm, om, gates, gates_t, m_g, c0, n0.reshape(b, H_M, 1, DH_M), m0.reshape(b, H_M, 1, 1), *prev)
    return h, c_states, n_new.reshape(b, H_M, DH_M), m_new.reshape(b, H_M)


def _merge_kernel(oa_ref, hm_ref, gts_ref, x_ref, g1_ref, wa_ref, wm_ref, wo_ref, o_ref, *, d):
    ya = _dot(oa_ref[...], wa_ref[...])
    ym = _dot(hm_ref[...], wm_ref[...])
    mix = jax.nn.sigmoid(gts_ref[:, 0:d]) * ya + jax.nn.sigmoid(gts_ref[:, d:2 * d]) * ym
    o_ref[...] = x_ref[...] + g1_ref[...] * _dot(mix.astype(BF16), wo_ref[...])


def _merge(oa, hm, gts, x, mod, per_row, wa, wm, wo):
    r, d = x.shape
    tm = min(r, ROW_TILE)
    nat = pl.BlockSpec((tm, d), lambda i: (i, 0))
    wspec = pl.BlockSpec((d, d), lambda i: (0, 0))
    return pl.pallas_call(
        functools.partial(_merge_kernel, d=d),
        grid=(r // tm,),
        in_specs=[nat, nat, pl.BlockSpec((tm, 2 * d), lambda i: (i, 0)), nat,
                  _mod_spec(per_row, tm, d, 2, lambda i: i), wspec, wspec, wspec],
        out_specs=nat,
        out_shape=jax.ShapeDtypeStruct((r, d), F32),
        compiler_params=_cparams(("arbitrary",)),
        name="merge_out",
    )(oa, hm, gts, x, mod, wa, wm, wo)


def _ffn_kernel(x_ref, g_ref, sc_ref, sh_ref, g2_ref, wu_ref, cw_ref, cb_ref, wd_ref, buf_ref, fg_ref,
                o_ref, nb_ref, tail_s, *, tm, dff, nc, final):
    first = pl.program_id(1) == 0
    ck = dff // nc
    x = x_ref[...]
    xn = (_rms(x, g_ref[...]) * (1.0 + sc_ref[...]) + sh_ref[...]).astype(BF16)
    rowi = lax.broadcasted_iota(jnp.int32, (tm, ck), 0)
    acc = None
    for c in range(nc):
        cs = slice(c * ck, (c + 1) * ck)
        u = _dot(xn, wu_ref[:, cs])
        gate = _dot(xn, wu_ref[:, dff + c * ck:dff + (c + 1) * ck])
        prev2 = jnp.where(first, buf_ref[0:1, cs], tail_s[6:7, cs])
        prev1 = jnp.where(first, buf_ref[1:2, cs], tail_s[7:8, cs])
        u1 = jnp.where(rowi == 0, prev1, pltpu.roll(u, 1, 0))
        u2 = jnp.where(rowi == 0, prev2, jnp.where(rowi == 1, prev1, pltpu.roll(u, 2, 0)))
        conv = cb_ref[:, cs] + cw_ref[0:1, cs] * u2 + cw_ref[1:2, cs] * u1 + cw_ref[2:3, cs] * u
        part = _dot((jax.nn.gelu(conv) * gate).astype(BF16), wd_ref[cs, :])
        acc = part if acc is None else acc + part
        tail_s[:, cs] = u[tm - 8:tm, :]
        nb_ref[:, cs] = u[tm - (CONV_W - 1):tm, :]
    y = x + g2_ref[...] * acc
    if final:
        y = _rms(y, fg_ref[...])
    o_ref[...] = y


def _ffn(x, mod, per_row, g, w_up, conv_w, conv_b, w_down, conv_buf, final_g, b, t, final):
    r, d = x.shape
    dff = w_down.shape[0]
    tm = min(t, ROW_TILE)
    nt = t // tm
    nc = 2
    assert dff % (nc * 128) == 0 and tm >= 8
    row = lambda i, j: i * nt + j
    nat = pl.BlockSpec((tm, d), lambda i, j: (i * nt + j, 0))
    whole = lambda a: pl.BlockSpec(a.shape, lambda i, j: (0,) * a.ndim)
    y, tails = pl.pallas_call(
        functools.partial(_ffn_kernel, tm=tm, dff=dff, nc=nc, final=final),
        grid=(b, nt),
        in_specs=[nat, whole(g), _mod_spec(per_row, tm, d, 4, row), _mod_spec(per_row, tm, d, 3, row),
                  _mod_spec(per_row, tm, d, 5, row), whole(w_up), whole(conv_w), whole(conv_b), whole(w_down),
                  pl.BlockSpec((None, CONV_W - 1, dff), lambda i, j: (i, 0, 0)), whole(final_g)],
        out_specs=[nat, pl.BlockSpec((None, None, CONV_W - 1, dff), lambda i, j: (i, j, 0, 0))],
        out_shape=[jax.ShapeDtypeStruct((r, d), F32), jax.ShapeDtypeStruct((b, nt, CONV_W - 1, dff), F32)],
        scratch_shapes=[pltpu.VMEM((8, dff), F32)],
        compiler_params=_cparams(("arbitrary", "arbitrary")),
        name="conv_ffn",
    )(x, g, mod, mod, mod, w_up, conv_w, conv_b, w_down, conv_buf, final_g)
    return y, tails[:, nt - 1]


def kernel(x_prompt, x_sample, c_prompt, c_sample, cache_k, cache_v, state_C, state_n, state_m, state_conv,
           rel_bias, ada_w, ada_b, norm1_g, norm2_g, w_in, b_if, lam_p, attn_sub_g, mlstm_g, w_br_a, w_br_m,
           w_out, w_up, conv_w, conv_b, w_down, final_g):
    bp, s, d = x_prompt.shape
    bs, ts, _ = x_sample.shape
    depth = w_in.shape[0]
    dff = w_down.shape[1]
    assert bp == 1 and d == H_A * HE_A == H_M * DH_M

    c_all = jnp.concatenate([c_prompt, c_sample, jnp.zeros((16 - bp - bs, d), F32)], axis=0)
    mod = _ada(c_all, ada_w, ada_b)

    xp = x_prompt.reshape(bp * s, d)
    xs = x_sample.reshape(bs * ts, d)
    tl_p = min(MLSTM_TILE, s)
    tl_s = min(CHUNK, ts)
    zero_c = jnp.zeros((1, bp, H_M, DH_M, DH_M), F32)
    zero_n = jnp.zeros((bp, H_M, DH_M), F32)
    zero_m = jnp.zeros((bp, H_M), F32)
    zero_buf = jnp.zeros((bp, CONV_W - 1, dff), F32)
    st_p, st_s = [], []
    kv_p = kv_s = c_p = c_s = None
    for l in range(depth):
        lam_init = 0.8 - 0.6 * math.exp(-0.3 * l)
        last = l == depth - 1
        w_qkv = w_in[l, :, 0:3 * d].astype(BF16)
        w_m = w_in[l, :, 3 * d:9 * d].astype(BF16)
        w_if = jnp.pad(w_in[l, :, 9 * d:], ((0, 0), (0, 128 - 2 * H_M))).astype(BF16)
        bif = jnp.pad(b_if[l], (0, 128 - 2 * H_M)).reshape(1, 128)
        wa, wm, wo = w_br_a[l].astype(BF16), w_br_m[l].astype(BF16), w_out[l].astype(BF16)
        wu, wd = w_up[l].astype(BF16), w_down[l].astype(BF16)
        g1, g2 = norm1_g[l].reshape(1, d), norm2_g[l].reshape(1, d)
        sub_g, m_g = attn_sub_g[l].reshape(1, HE_A), mlstm_g[l].reshape(1, d)
        cb = conv_b[l].reshape(1, dff)
        fg = final_g.reshape(1, d)

        mod_p = mod[l, 0:bp]
        qT, k_all, kbf, v_all, vT = _in_attn(xp, mod_p, False, g1, w_qkv, True, l, depth, kv_p)
        kv_p = (k_all, v_all)
        qm, km, vm, om, gts, gates = _in_mlstm(xp, mod_p, False, g1, w_m, w_if, bif)
        oa = _attn_prompt(qT, kbf, vT, rel_bias, lam_p[l], sub_g, lam_init, min(ATTN_TILE, s))
        hm, c_p, n_new, m_new = _mlstm(qm, km, vm, om, gates, m_g, zero_c, 0, zero_n, zero_m, bp, s, tl_p,
                                       l, depth, c_p)
        xp = _merge(oa, hm, gts, xp, mod_p, False, wa, wm, wo)
        xp, buf_new = _ffn(xp, mod_p, False, g2, wu, conv_w[l], cb, wd, zero_buf, fg, bp, s, last)
        st_p.append((n_new, m_new, buf_new))

        mod_s = jnp.repeat(mod[l, bp:bp + bs], ts, axis=0)
        q, k_all, kbf, v_all, vbf = _in_attn(xs, mod_s, True, g1, w_qkv, False, l, depth, kv_s)
        kv_s = (k_all, v_all)
        qm, km, vm, om, gts, gates = _in_mlstm(xs, mod_s, True, g1, w_m, w_if, bif)
        oa = _attn_sample(q, kbf, vbf, cache_k, cache_v, l, rel_bias, lam_p[l], sub_g, lam_init, bs, ts)
        hm, c_s, n_new, m_new = _mlstm(qm, km, vm, om, gates, m_g, state_C, l, state_n[l], state_m[l], bs, ts, tl_s,
                                       l, depth, c_s)
        xs = _merge(oa, hm, gts, xs, mod_s, True, wa, wm, wo)
        xs, buf_new = _ffn(xs, mod_s, True, g2, wu, conv_w[l], cb, wd, state_conv[l], fg, bs, ts, last)
        st_s.append((n_new, m_new, buf_new))

    outs_p = [a.reshape(depth, bp, s, H_A, HE_A) for a in kv_p] + [c_p] + [jnp.stack(a) for a in zip(*st_p)]
    outs_s = [a.reshape(depth, bs, ts, H_A, HE_A) for a in kv_s] + [c_s] + [jnp.stack(a) for a in zip(*st_s)]
    return (xp.reshape(bp, s, d), xs.reshape(bs, ts, d), *outs_p, *outs_s)
```

```python
import functools
import math

import numpy as np
import jax
import jax.numpy as jnp
from jax import lax
from jax.experimental import pallas as pl
from jax.experimental.pallas import tpu as pltpu

F32 = jnp.float32
BF16 = jnp.bfloat16

CHUNK = 64
H_A = 8
HD_A = 64
HE_A = 2 * HD_A
H_M = 4
DH_M = 256
N_BUCKETS = 32
CONV_W = 3
NEG_INF = -1e30
EPS = 1e-6
FAR_BUCKET = N_BUCKETS // 2 - 1
LOG2E = math.log2(math.e)

V7X_VMEM_LIMIT = 56 * 1024 * 1024
ROW_TILE = 512
ATTN_TILE = 512
ATTN_FAR_BLOCKS = 4
ATTN_TILES_PER_STEP = 8
BF16_ROWS = 16
MLSTM_TILE = 256

NT_DIMS = (((1,), (1,)), ((), ()))


def _cparams(sem):
    return pltpu.CompilerParams(dimension_semantics=sem, vmem_limit_bytes=V7X_VMEM_LIMIT)


def _dot(a, b):
    return jnp.dot(a, b, preferred_element_type=F32)


def _rms(x, g):
    return x * lax.rsqrt(jnp.mean(x * x, axis=-1, keepdims=True) + EPS) * g


def _log_sigmoid(x):
    return jnp.minimum(x, 0.0) - jnp.log1p(jnp.exp(-jnp.abs(x)))


def _mod_spec(per_row, tm, d, col, row_index):
    if per_row:
        return pl.BlockSpec((tm, d), lambda *g: (row_index(*g), col))
    return pl.BlockSpec((1, d), lambda *g: (0, col))


def _ada_kernel(c_ref, w_ref, b_ref, o_ref):
    a = jax.nn.silu(c_ref[...]).astype(BF16)
    o_ref[...] = _dot(a, w_ref[...].astype(BF16)) + b_ref[...]


def _ada(c_all, ada_w, ada_b):
    depth, d, n = ada_w.shape
    rc = c_all.shape[0]
    tn = 1536
    return pl.pallas_call(
        _ada_kernel,
        grid=(depth, n // tn),
        in_specs=[pl.BlockSpec((rc, d), lambda l, j: (0, 0)),
                  pl.BlockSpec((None, d, tn), lambda l, j: (l, 0, j)),
                  pl.BlockSpec((None, 1, tn), lambda l, j: (l, 0, j))],
        out_specs=pl.BlockSpec((None, rc, tn), lambda l, j: (l, 0, j)),
        out_shape=jax.ShapeDtypeStruct((depth, rc, n), F32),
        compiler_params=_cparams(("arbitrary", "arbitrary")),
        name="ada_mod",
    )(c_all, ada_w, ada_b.reshape(depth, 1, n))


def _in_attn_kernel(x_ref, g_ref, sc_ref, sh_ref, w_ref, *refs, d, transposed, layer, creates):
    q_ref, k32_ref, kbf_ref, v32_ref, vbf_ref = refs[-5:]
    xn = (_rms(x_ref[...], g_ref[...]) * (1.0 + sc_ref[...]) + sh_ref[...]).astype(BF16)
    q = _dot(xn, w_ref[:, 0:d]) * (HD_A ** -0.5 * (LOG2E if transposed else 1.0))
    k = _dot(xn, w_ref[:, d:2 * d])
    v = _dot(xn, w_ref[:, 2 * d:3 * d])
    if creates:
        for l in range(k32_ref.shape[0]):
            k32_ref[l] = k if l == layer else jnp.zeros_like(k)
            v32_ref[l] = v if l == layer else jnp.zeros_like(v)
    else:
        k32_ref[...] = k
        v32_ref[...] = v
    kbf_ref[...] = k.astype(BF16)
    if transposed:
        q_ref[...] = q.T.astype(BF16)
        vbf_ref[...] = v.T.astype(BF16)
    else:
        q_ref[...] = q.astype(BF16)
        vbf_ref[...] = v.astype(BF16)


def _in_attn(x, mod, per_row, g, w_in, transposed, layer, depth, kv_states):
    r, d = x.shape
    tm = min(r, ROW_TILE)
    row = lambda i: i
    nat = pl.BlockSpec((tm, d), lambda i: (i, 0))
    tr = pl.BlockSpec((d, tm), lambda i: (0, i))
    prev = () if kv_states is None else tuple(kv_states)
    state = (pl.BlockSpec((None, tm, d), lambda i: (layer, i, 0)) if prev else
             pl.BlockSpec((depth, tm, d), lambda i: (0, i, 0)))
    state_shape = jax.ShapeDtypeStruct((depth, r, d), F32)
    bf_shape = jax.ShapeDtypeStruct((d, r) if transposed else (r, d), BF16)
    return pl.pallas_call(
        functools.partial(_in_attn_kernel, d=d, transposed=transposed, layer=layer, creates=not prev),
        grid=(r // tm,),
        in_specs=[nat, pl.BlockSpec((1, d), lambda i: (0, 0)),
                  _mod_spec(per_row, tm, d, 1, row), _mod_spec(per_row, tm, d, 0, row),
                  pl.BlockSpec((None, d, 3 * d), lambda i: (layer, 0, 0))]
        + [pl.BlockSpec(memory_space=pl.ANY)] * len(prev),
        out_specs=[tr if transposed else nat, state, nat, state, tr if transposed else nat],
        out_shape=[bf_shape, state_shape, jax.ShapeDtypeStruct((r, d), BF16), state_shape, bf_shape],
        input_output_aliases={5: 1, 6: 3} if prev else {},
        compiler_params=_cparams(("arbitrary",)),
        name="in_attn",
    )(x, g, mod, mod, w_in, *prev)


def _in_mlstm_kernel(x_ref, g_ref, sc_ref, sh_ref, w1_ref, w2_ref, wif_ref, bif_ref,
                     qm_ref, km_ref, vm_ref, om_ref, gts_ref, gate_ref, *, d):
    xn = (_rms(x_ref[...], g_ref[...]) * (1.0 + sc_ref[...]) + sh_ref[...]).astype(BF16)
    qm_ref[...] = _dot(xn, w1_ref[:, 0:d]).astype(BF16)
    km_ref[...] = (_dot(xn, w1_ref[:, d:2 * d]) * (DH_M ** -0.5)).astype(BF16)
    vm_ref[...] = _dot(xn, w1_ref[:, 2 * d:3 * d]).astype(BF16)
    om_ref[...] = _dot(xn, w2_ref[:, 0:d]).astype(BF16)
    gts_ref[...] = _dot(xn, w2_ref[:, d:3 * d]).astype(BF16)
    pre = _dot(xn, wif_ref[...]) + bif_ref[...]
    gate_ref[...] = pre[:, 0:2 * H_M]


def _in_mlstm(x, mod, per_row, g, w_in, layer, w_if, b_if):
    r, d = x.shape
    tm = min(r, ROW_TILE)
    row = lambda i: i
    nat = pl.BlockSpec((tm, d), lambda i: (i, 0))
    const = lambda shape: pl.BlockSpec(shape, lambda i: (0, 0))
    return pl.pallas_call(
        functools.partial(_in_mlstm_kernel, d=d),
        grid=(r // tm,),
        in_specs=[nat, const((1, d)), _mod_spec(per_row, tm, d, 1, row), _mod_spec(per_row, tm, d, 0, row),
                  pl.BlockSpec((None, d, 3 * d), lambda i: (layer, 0, 1)),
                  pl.BlockSpec((None, d, 3 * d), lambda i: (layer, 0, 2)),
                  const(w_if.shape), const(b_if.shape)],
        out_specs=[nat, nat, nat, nat, pl.BlockSpec((tm, 2 * d), lambda i: (i, 0)),
                   pl.BlockSpec((tm, 2 * H_M), lambda i: (i, 0))],
        out_shape=[jax.ShapeDtypeStruct((r, d), BF16)] * 4
        + [jax.ShapeDtypeStruct((r, 2 * d), BF16), jax.ShapeDtypeStruct((r, 2 * H_M), F32)],
        compiler_params=_cparams(("arbitrary",)),
        name="in_mlstm",
    )(x, g, mod, mod, w_in, w_in, w_if, b_if)


def _bucket_np(rel):
    half = N_BUCKETS // 2
    max_exact = half // 2
    n = np.abs(rel)
    thresholds = np.ceil(max_exact * 2.0 ** (np.arange(1, half - max_exact) / 2.0)).astype(np.int64)
    large = max_exact + (n[..., None] >= thresholds).sum(-1)
    return np.where(rel > 0, half, 0) + np.where(n < max_exact, n, np.minimum(large, half - 1))


def _lam(lp):
    s1 = jnp.sum(lp[0:1] * lp[1:2], axis=-1, keepdims=True)
    s2 = jnp.sum(lp[2:3] * lp[3:4], axis=-1, keepdims=True)
    return jnp.exp(s1) - jnp.exp(s2)


def _bias_steps(lo, hi):
    rel = np.arange(lo, hi + 1)
    b = _bucket_np(rel)
    starts = np.concatenate([[0], np.nonzero(b[1:] != b[:-1])[0] + 1])
    return [(int(rel[k]), int(b[k])) for k in starts]


def _rel_bias_tile(rb_ref, h, rel, lo, hi, shift, scale):
    steps = _bias_steps(lo, hi)
    v = jnp.full(rel.shape, (rb_ref[steps[0][1], h] - shift) * scale, F32)
    for start, bucket in steps[1:]:
        v = jnp.where(rel >= start, (rb_ref[bucket, h] - shift) * scale, v)
    return v


def _attn_prompt_tile(rb_ref, lamp_ref, qT_ref, k_ref, vT_ref, subg_ref, o_ref,
                      q_s, bias_s, m_s, l_s, acc_s, *, i, qsl, t, far_blocks, lam_init):
    h = pl.program_id(0)
    rows = 64
    chunk_shift = CHUNK.bit_length() - 1

    @pl.when(i == 0)
    def _():
        far = rb_ref[FAR_BUCKET, h]
        for tile, shift in enumerate((t, 0)):

            def fill(r, carry, tile=tile, shift=shift):
                r0 = pl.multiple_of(r * rows, rows)
                kk = lax.broadcasted_iota(jnp.int32, (rows, t), 0) + r0
                qq = lax.broadcasted_iota(jnp.int32, (rows, t), 1)
                v = _rel_bias_tile(rb_ref, h, kk - qq - shift, -shift - (t - 1), -shift + (t - 1), far, LOG2E)
                if tile == 1:
                    visible = jnp.right_shift(kk, chunk_shift) <= jnp.right_shift(qq, chunk_shift)
                    v = jnp.where(visible, v, NEG_INF)
                bias_s[tile, pl.ds(r0, rows), :] = v
                return carry

            lax.fori_loop(0, t // rows, fill, 0)

    lam = _lam(lamp_ref[...]) + lam_init
    qT = qT_ref[:, qsl]
    rowi = lax.broadcasted_iota(jnp.int32, qT.shape, 0)
    zero = jnp.zeros_like(qT)
    q_s[0] = jnp.where(rowi < HD_A, qT, zero)
    q_s[1] = jnp.where(rowi >= HD_A, qT, zero)

    def keys(j0, n):
        off = pl.multiple_of(j0 * t, t)
        return k_ref[pl.ds(off, n), :], vT_ref[:, pl.ds(off, n)]

    def scores(kb, c, tile):
        s = _dot(kb, q_s[c])
        return s if tile is None else s + bias_s[tile]

    def fixed_shift_block(j0, n, tile):
        kb, vb = keys(j0, n)
        for c in range(2):
            e = jnp.exp2(scores(kb, c, tile) - m_s[c])
            l_s[c] += jnp.sum(e, axis=0, keepdims=True)
            acc_s[c] += _dot(vb, e.astype(BF16))

    def running_max_block(j0, tile):
        kb, vb = keys(j0, t)
        for c in range(2):
            s = scores(kb, c, tile)
            m_old = m_s[c]
            m_new = jnp.maximum(m_old, jnp.max(s, axis=0, keepdims=True))
            alpha = jnp.exp2(m_old - m_new)
            e = jnp.exp2(s - m_new)
            l_s[c] = alpha * l_s[c] + jnp.sum(e, axis=0, keepdims=True)
            acc_s[c] = alpha * acc_s[c] + _dot(vb, e.astype(BF16))
            m_s[c] = m_new

    def finish():
        l0, l1 = l_s[0], l_s[1]
        oT = acc_s[0] * (1.0 / l0) - acc_s[1] * (lam / l1)
        inv = lax.rsqrt(jnp.mean(oT * oT, axis=0, keepdims=True) + EPS)
        o_ref[qsl, :] = ((oT * inv).T * (subg_ref[...] * (1.0 - lam_init))).astype(BF16)
        big = float(np.finfo(np.float32).max)
        bad = jnp.max(jnp.where(jnp.abs(oT) <= big, 0.0, 1.0), axis=0, keepdims=True)
        return jnp.max(jnp.maximum(bad, jnp.where((l0 <= big) & (l1 <= big), 0.0, 1.0)))

    def first_step(r):
        nb = max(r, 0) + (2 if r >= 0 else 1)
        kb, vb = keys(i - (nb - 1), nb * t)
        for c in range(2):
            s = _dot(kb, q_s[c])
            parts = [s[0:r * t]] if r > 0 else []
            if r >= 0:
                parts.append(s[(nb - 2) * t:(nb - 1) * t] + bias_s[0])
            parts.append(s[(nb - 1) * t:nb * t] + bias_s[1])
            e = [jnp.exp2(p - m_s[c]) for p in parts]
            l_s[c] = functools.reduce(jnp.add, [jnp.sum(p, axis=0, keepdims=True) for p in e])
            e = [p.astype(BF16) for p in e]
            acc_s[c] = _dot(vb, e[0] if len(e) == 1 else jnp.concatenate(e, axis=0))

    k0 = k_ref[pl.ds(pl.multiple_of(i * t, t), CHUNK), :]
    for c in range(2):
        m_s[c] = jnp.max(_dot(k0, q_s[c]) + bias_s[1, 0:CHUNK, :], axis=0, keepdims=True)

    nfar = jnp.maximum(i - 1, 0)
    folded = jnp.bitwise_and(nfar, far_blocks - 1)

    @pl.when(i == 0)
    def _():
        first_step(-1)

    for r in range(far_blocks):
        @pl.when((i >= 1) & (folded == r))
        def _(r=r):
            first_step(r)

    groups = lax.div(nfar, far_blocks)
    odd = jnp.bitwise_and(groups, 1)

    @pl.when(odd == 1)
    def _():
        fixed_shift_block(0, far_blocks * t, None)

    def big_step(p, carry):
        fixed_shift_block((odd + 2 * p) * far_blocks, 2 * far_blocks * t, None)
        return carry

    lax.fori_loop(0, lax.shift_right_logical(groups, 1), big_step, 0)

    @pl.when(finish() > 0.0)
    def _():
        m_s[...] = jnp.full(m_s.shape, NEG_INF, F32)
        l_s[...] = jnp.zeros(l_s.shape, F32)
        acc_s[...] = jnp.zeros(acc_s.shape, F32)

        def step(j, carry):
            running_max_block(j, None)
            return carry

        lax.fori_loop(0, nfar, step, 0)

        @pl.when(i >= 1)
        def _():
            running_max_block(i - 1, 0)

        running_max_block(i, 1)
        finish()


def _attn_prompt_kernel(*refs, t, tiles, far_blocks, lam_init):
    def one_tile(n, carry):
        _attn_prompt_tile(*refs, i=pl.program_id(1) * tiles + n, qsl=pl.ds(pl.multiple_of(n * t, t), t),
                          t=t, far_blocks=far_blocks, lam_init=lam_init)
        return carry

    lax.fori_loop(0, tiles, one_tile, 0)


def _attn_prompt(qT, kbf, vT, rel_bias, lam_p, sub_g, lam_init, t):
    d, s = qT.shape
    assert s % t == 0 and t % 64 == 0 and t % CHUNK == 0 and CHUNK & (CHUNK - 1) == 0
    assert _bucket_np(np.array([-(t + 1)]))[0] == FAR_BUCKET
    assert ATTN_FAR_BLOCKS & (ATTN_FAR_BLOCKS - 1) == 0
    nq = s // t
    tiles = math.gcd(nq, ATTN_TILES_PER_STEP)
    return pl.pallas_call(
        functools.partial(_attn_prompt_kernel, t=t, tiles=tiles, far_blocks=ATTN_FAR_BLOCKS, lam_init=lam_init),
        grid=(H_A, nq // tiles),
        in_specs=[pl.BlockSpec(memory_space=pltpu.SMEM),
                  pl.BlockSpec((4, HD_A), lambda h, i: (0, 0)),
                  pl.BlockSpec((HE_A, tiles * t), lambda h, i: (h, i)),
                  pl.BlockSpec((s, HE_A), lambda h, i: (0, h)),
                  pl.BlockSpec((HE_A, s), lambda h, i: (h, 0)),
                  pl.BlockSpec((1, HE_A), lambda h, i: (0, 0))],
        out_specs=pl.BlockSpec((tiles * t, HE_A), lambda h, i: (i, h)),
        out_shape=jax.ShapeDtypeStruct((s, d), BF16),
        scratch_shapes=[pltpu.VMEM((2, HE_A, t), BF16), pltpu.VMEM((2, t, t), F32),
                        pltpu.VMEM((2, 1, t), F32), pltpu.VMEM((2, 1, t), F32), pltpu.VMEM((2, HE_A, t), F32)],
        compiler_params=_cparams(("arbitrary", "arbitrary")),
        name="attn_prompt",
    )(rel_bias.astype(F32), lam_p, qT, kbf, vT, sub_g)


def _attn_sample_kernel(rb_ref, lamp_ref, q_ref, kc_ref, vc_ref, kn_ref, vn_ref, subg_ref, o_ref,
                        *, past, lam_init):
    t = q_ref.shape[0]
    chunk_shift = CHUNK.bit_length() - 1
    lam = _lam(lamp_ref[...]) + lam_init

    def positions(n_keys, first_key):
        qpos = past + lax.broadcasted_iota(jnp.int32, (t, n_keys), 0)
        kpos = first_key + lax.broadcasted_iota(jnp.int32, (t, n_keys), 1)
        return kpos - qpos, jnp.right_shift(kpos, chunk_shift) <= jnp.right_shift(qpos, chunk_shift)

    rel_c, vis_c = positions(past, 0)
    rel_n, vis_n = positions(t, past)
    for h in range(H_A):
        sl = slice(h * HE_A, (h + 1) * HE_A)
        bias_c = jnp.where(vis_c, _rel_bias_tile(rb_ref, h, rel_c, -(past + t - 1), -1, 0.0, 1.0), NEG_INF)
        bias_n = jnp.where(vis_n, _rel_bias_tile(rb_ref, h, rel_n, -(t - 1), t - 1, 0.0, 1.0), NEG_INF)
        q = q_ref[:, sl]
        lane = lax.broadcasted_iota(jnp.int32, q.shape, 1)
        zero = jnp.zeros_like(q)
        q2 = jnp.concatenate([jnp.where(lane < HD_A, q, zero), jnp.where(lane >= HD_A, q, zero)], axis=0)
        kc = kc_ref[pl.ds(h, past, stride=H_A), :].astype(BF16)
        sc = lax.dot_general(q2, kc, NT_DIMS, preferred_element_type=F32) + jnp.concatenate([bias_c, bias_c], 0)
        sn = (lax.dot_general(q2, kn_ref[:, sl], NT_DIMS, preferred_element_type=F32)
              + jnp.concatenate([bias_n, bias_n], 0))
        m = jnp.maximum(jnp.max(sc, axis=-1, keepdims=True), jnp.max(sn, axis=-1, keepdims=True))
        ec = jnp.exp(sc - m)
        en = jnp.exp(sn - m)
        l = jnp.sum(ec, axis=-1, keepdims=True) + jnp.sum(en, axis=-1, keepdims=True)
        pc = ec / l
        pn = en / l
        ac = pc[0:t] - lam * pc[t:2 * t]
        an = pn[0:t] - lam * pn[t:2 * t]
        vc = vc_ref[pl.ds(h, past, stride=H_A), :].astype(BF16)
        o = _dot(ac.astype(BF16), vc) + _dot(an.astype(BF16), vn_ref[:, sl])
        o_ref[:, sl] = (_rms(o, subg_ref[...]) * (1.0 - lam_init)).astype(BF16)


def _attn_sample(q, kbf, vbf, cache_k, cache_v, layer, rel_bias, lam_p, sub_g, lam_init, b, t):
    r, d = q.shape
    past = cache_k.shape[2]
    rows = pl.BlockSpec((t, d), lambda i: (i, 0))
    cache_k = cache_k.reshape(-1, HE_A)
    cache_v = cache_v.reshape(-1, HE_A)
    cache = pl.BlockSpec((past * H_A, HE_A), lambda i: (layer * b + i, 0))
    return pl.pallas_call(
        functools.partial(_attn_sample_kernel, past=past, lam_init=lam_init),
        grid=(b,),
        in_specs=[pl.BlockSpec(memory_space=pltpu.SMEM),
                  pl.BlockSpec((4, HD_A), lambda i: (0, 0)),
                  rows, cache, cache, rows, rows,
                  pl.BlockSpec((1, HE_A), lambda i: (0, 0))],
        out_specs=rows,
        out_shape=jax.ShapeDtypeStruct((r, d), BF16),
        compiler_params=_cparams(("arbitrary",)),
        name="attn_sample",
    )(rel_bias.astype(F32), lam_p, q, cache_k, cache_v, kbf, vbf, sub_g)


def _mlstm_kernel(q_ref, k_ref, v_ref, om_ref, g_ref, gT_ref, mg_ref, c0_ref, n0_ref, m0_ref, *refs,
                  tl, layer, creates):
    h_ref, c_ref, n_ref, m_ref, c_s, n_s, m_s = refs[-7:]
    step = pl.program_id(1)

    @pl.when(step == 0)
    def _():
        c_s[...] = c0_ref[...]
        n_s[...] = n0_ref[...]
        m_s[...] = m0_ref[...]

    ti = lax.broadcasted_iota(jnp.int32, (tl, tl), 0)
    si = lax.broadcasted_iota(jnp.int32, (tl, tl), 1)
    causal = si <= ti
    gates_c = g_ref[...]
    gates_r = gT_ref[...]
    lf_cols = _log_sigmoid(gates_c)
    lf_rows = _log_sigmoid(gates_r)
    for h in range(H_M):
        sl = slice(h * DH_M, (h + 1) * DH_M)
        q = q_ref[:, sl]
        k = k_ref[:, sl]
        v = v_ref[:, sl]
        ig_c = gates_c[:, h:h + 1]
        lf_c = lf_cols[:, H_M + h:H_M + h + 1]
        ig_r = gates_r[h:h + 1, :]
        lf_r = lf_rows[H_M + h:H_M + h + 1, :]
        m = m_s[h]
        n_row = n_s[h]
        b_c = jnp.sum(jnp.where(causal, lf_r, 0.0), axis=1, keepdims=True)
        b_r = jnp.sum(jnp.where(ti <= si, lf_c, 0.0), axis=0, keepdims=True)
        dmat = jnp.where(causal, b_c - b_r + ig_r, -jnp.inf)
        inter = b_c + m
        m_t = jnp.maximum(inter, jnp.max(dmat, axis=1, keepdims=True))
        w_intra = jnp.exp(dmat - m_t)
        w_inter = jnp.exp(inter - m_t)
        a = w_intra * lax.dot_general(q, k, NT_DIMS, preferred_element_type=F32)
        num = w_inter * _dot(q, c_s[h].astype(BF16)) + _dot(a.astype(BF16), v)
        den = (w_inter * jnp.sum(q.astype(F32) * n_row, axis=1, keepdims=True)
               + jnp.sum(a, axis=1, keepdims=True))
        hh = num / jnp.maximum(jnp.abs(den), jnp.exp(-m_t))
        b_last = b_r[:, tl - 1:tl]
        g_c = b_last - b_c + ig_c
        m_new = jnp.maximum(b_last + m, jnp.max(g_c, axis=0, keepdims=True))
        decay = jnp.exp(b_last + m - m_new)
        kw = jnp.exp(g_c - m_new) * k.astype(F32)
        c_s[h] = decay * c_s[h] + _dot(kw.T.astype(BF16), v)
        n_s[h] = decay * n_row + jnp.sum(kw, axis=0, keepdims=True)
        m_s[h] = m_new
        hn = _rms(hh, mg_ref[:, sl])
        h_ref[:, sl] = (hn * jax.nn.sigmoid(om_ref[:, sl].astype(F32))).astype(BF16)

    @pl.when(step == pl.num_programs(1) - 1)
    def _():
        if creates:
            for l in range(c_ref.shape[0]):
                c_ref[l] = c_s[...] if l == layer else jnp.zeros(c_s.shape, F32)
        else:
            c_ref[...] = c_s[...]
        n_ref[...] = n_s[...]
        m_ref[...] = m_s[...]


def _mlstm(qm, km, vm, om, gates, m_g, c0, c0_layer, n0, m0, b, t, tl, layer, depth, c_states):
    r, d = qm.shape
    nc = t // tl
    gates_t = jnp.transpose(gates.reshape(b * nc, tl, 2 * H_M), (0, 2, 1))
    rows = pl.BlockSpec((tl, d), lambda i, c: (i * nc + c, 0))
    st_n = pl.BlockSpec((None, H_M, 1, DH_M), lambda i, c: (i, 0, 0, 0))
    st_m = pl.BlockSpec((None, H_M, 1, 1), lambda i, c: (i, 0, 0, 0))
    prev = () if c_states is None else (c_states,)
    c_out = (pl.BlockSpec((None, None, H_M, DH_M, DH_M), lambda i, c: (layer, i, 0, 0, 0)) if prev else
             pl.BlockSpec((depth, None, H_M, DH_M, DH_M), lambda i, c: (0, i, 0, 0, 0)))
    h, c_states, n_new, m_new = pl.pallas_call(
        functools.partial(_mlstm_kernel, tl=tl, layer=layer, creates=not prev),
        grid=(b, nc),
        in_specs=[rows, rows, rows, rows,
                  pl.BlockSpec((tl, 2 * H_M), lambda i, c: (i * nc + c, 0)),
                  pl.BlockSpec((None, 2 * H_M, tl), lambda i, c: (i * nc + c, 0, 0)),
                  pl.BlockSpec((1, d), lambda i, c: (0, 0)),
                  pl.BlockSpec((None, None, H_M, DH_M, DH_M), lambda i, c: (c0_layer, i, 0, 0, 0)),
                  st_n, st_m] + [pl.BlockSpec(memory_space=pl.ANY)] * len(prev),
        out_specs=[rows, c_out, st_n, st_m],
        out_shape=[jax.ShapeDtypeStruct((r, d), BF16),
                   jax.ShapeDtypeStruct((depth, b, H_M, DH_M, DH_M), F32),
                   jax.ShapeDtypeStruct((b, H_M, 1, DH_M), F32),
                   jax.ShapeDtypeStruct((b, H_M, 1, 1), F32)],
        input_output_aliases={10: 1} if prev else {},
        scratch_shapes=[pltpu.VMEM((H_M, DH_M, DH_M), F32), pltpu.VMEM((H_M, 1, DH_M), F32),
                        pltpu.VMEM((H_M, 1, 1), F32)],
        compiler_params=_cparams(("arbitrary", "arbitrary")),
        name="mlstm",
    )(qm, km, vm, om, gates, gates_t, m_g, c0, n0.reshape(b, H_M, 1, DH_M), m0.reshape(b, H_M, 1, 1), *prev)
    return h, c_states, n_new.reshape(b, H_M, DH_M), m_new.reshape(b, H_M)


def _merge_kernel(oa_ref, hm_ref, gts_ref, x_ref, g1_ref, wa_ref, wm_ref, wo_ref, o_ref, *, d):
    ya = _dot(oa_ref[...], wa_ref[...])
    ym = _dot(hm_ref[...], wm_ref[...])
    mix = (jax.nn.sigmoid(gts_ref[:, 0:d].astype(F32)) * ya
           + jax.nn.sigmoid(gts_ref[:, d:2 * d].astype(F32)) * ym)
    o_ref[...] = x_ref[...] + g1_ref[...] * _dot(mix.astype(BF16), wo_ref[...])


def _merge(oa, hm, gts, x, mod, per_row, wa, wm, wo):
    r, d = x.shape
    tm = min(r, ROW_TILE)
    nat = pl.BlockSpec((tm, d), lambda i: (i, 0))
    wspec = pl.BlockSpec((d, d), lambda i: (0, 0))
    return pl.pallas_call(
        functools.partial(_merge_kernel, d=d),
        grid=(r // tm,),
        in_specs=[nat, nat, pl.BlockSpec((tm, 2 * d), lambda i: (i, 0)), nat,
                  _mod_spec(per_row, tm, d, 2, lambda i: i), wspec, wspec, wspec],
        out_specs=nat,
        out_shape=jax.ShapeDtypeStruct((r, d), F32),
        compiler_params=_cparams(("arbitrary",)),
        name="merge_out",
    )(oa, hm, gts, x, mod, wa, wm, wo)


def _ffn_kernel(x_ref, g_ref, sc_ref, sh_ref, g2_ref, wu_ref, cw_ref, cb_ref, wd_ref, buf_ref, fg_ref,
                o_ref, nb_ref, tail_s, *, tm, dff, nc, final):
    first = pl.program_id(1) == 0
    ck = dff // nc
    x = x_ref[...]
    xn = (_rms(x, g_ref[...]) * (1.0 + sc_ref[...]) + sh_ref[...]).astype(BF16)
    rowi = lax.broadcasted_iota(jnp.int32, (tm, ck), 0)
    acc = None
    for c in range(nc):
        cs = slice(c * ck, (c + 1) * ck)
        u = _dot(xn, wu_ref[:, cs])
        gate = _dot(xn, wu_ref[:, dff + c * ck:dff + (c + 1) * ck])
        prev2 = jnp.where(first, buf_ref[0:1, cs], tail_s[6:7, cs])
        prev1 = jnp.where(first, buf_ref[1:2, cs], tail_s[7:8, cs])
        u1 = jnp.where(rowi == 0, prev1, pltpu.roll(u, 1, 0))
        u2 = jnp.where(rowi == 0, prev2, jnp.where(rowi == 1, prev1, pltpu.roll(u, 2, 0)))
        conv = cb_ref[:, cs] + cw_ref[0:1, cs] * u2 + cw_ref[1:2, cs] * u1 + cw_ref[2:3, cs] * u
        part = _dot((jax.nn.gelu(conv) * gate).astype(BF16), wd_ref[cs, :])
        acc = part if acc is None else acc + part
        tail_s[:, cs] = u[tm - 8:tm, :]
        nb_ref[:, cs] = u[tm - (CONV_W - 1):tm, :]
    y = x + g2_ref[...] * acc
    if final:
        y = _rms(y, fg_ref[...])
    o_ref[...] = y


def _ffn(x, mod, per_row, g, w_up, conv_w, conv_b, w_down, conv_buf, final_g, b, t, final):
    r, d = x.shape
    dff = w_down.shape[0]
    tm = min(t, ROW_TILE)
    nt = t // tm
    nc = 2
    assert dff % (nc * 128) == 0 and tm >= 8
    row = lambda i, j: i * nt + j
    nat = pl.BlockSpec((tm, d), lambda i, j: (i * nt + j, 0))
    whole = lambda a: pl.BlockSpec(a.shape, lambda i, j: (0,) * a.ndim)
    y, tails = pl.pallas_call(
        functools.partial(_ffn_kernel, tm=tm, dff=dff, nc=nc, final=final),
        grid=(b, nt),
        in_specs=[nat, whole(g), _mod_spec(per_row, tm, d, 4, row), _mod_spec(per_row, tm, d, 3, row),
                  _mod_spec(per_row, tm, d, 5, row), whole(w_up), whole(conv_w), whole(conv_b), whole(w_down),
                  pl.BlockSpec((None, CONV_W - 1, dff), lambda i, j: (i, 0, 0)), whole(final_g)],
        out_specs=[nat, pl.BlockSpec((None, None, CONV_W - 1, dff), lambda i, j: (i, j, 0, 0))],
        out_shape=[jax.ShapeDtypeStruct((r, d), F32), jax.ShapeDtypeStruct((b, nt, CONV_W - 1, dff), F32)],
        scratch_shapes=[pltpu.VMEM((8, dff), F32)],
        compiler_params=_cparams(("arbitrary", "arbitrary")),
        name="conv_ffn",
    )(x, g, mod, mod, mod, w_up, conv_w, conv_b, w_down, conv_buf, final_g)
    return y, tails[:, nt - 1]


def kernel(x_prompt, x_sample, c_prompt, c_sample, cache_k, cache_v, state_C, state_n, state_m, state_conv,
           rel_bias, ada_w, ada_b, norm1_g, norm2_g, w_in, b_if, lam_p, attn_sub_g, mlstm_g, w_br_a, w_br_m,
           w_out, w_up, conv_w, conv_b, w_down, final_g):
    bp, s, d = x_prompt.shape
    bs, ts, _ = x_sample.shape
    depth = w_in.shape[0]
    dff = w_down.shape[1]
    assert bp == 1 and d == H_A * HE_A == H_M * DH_M

    c_all = jnp.concatenate([c_prompt, c_sample, jnp.zeros((16 - bp - bs, d), F32)], axis=0)
    mod = _ada(c_all, ada_w, ada_b)

    xp = x_prompt.reshape(bp * s, d)
    xs = x_sample.reshape(bs * ts, d)
    tl_p = min(MLSTM_TILE, s)
    tl_s = min(CHUNK, ts)
    zero_c = jnp.zeros((1, bp, H_M, DH_M, DH_M), F32)
    zero_n = jnp.zeros((bp, H_M, DH_M), F32)
    zero_m = jnp.zeros((bp, H_M), F32)
    zero_buf = jnp.zeros((bp, CONV_W - 1, dff), F32)
    w_in_bf = w_in.astype(BF16)
    assert (w_in.shape[2] - 2 * H_M) == 9 * d
    st_p, st_s = [], []
    kv_p = kv_s = c_p = c_s = None
    for l in range(depth):
        lam_init = 0.8 - 0.6 * math.exp(-0.3 * l)
        last = l == depth - 1
        w_if = jnp.pad(w_in[l, :, 9 * d:], ((0, 0), (0, 128 - 2 * H_M))).astype(BF16)
        bif = jnp.pad(b_if[l], (0, 128 - 2 * H_M)).reshape(1, 128)
        wa, wm, wo = w_br_a[l].astype(BF16), w_br_m[l].astype(BF16), w_out[l].astype(BF16)
        wu, wd = w_up[l].astype(BF16), w_down[l].astype(BF16)
        g1, g2 = norm1_g[l].reshape(1, d), norm2_g[l].reshape(1, d)
        sub_g, m_g = attn_sub_g[l].reshape(1, HE_A), mlstm_g[l].reshape(1, d)
        cb = conv_b[l].reshape(1, dff)
        fg = final_g.reshape(1, d)

        mod_p = mod[l, 0:bp]
        qT, k_all, kbf, v_all, vT = _in_attn(xp, mod_p, False, g1, w_in_bf, True, l, depth, kv_p)
        kv_p = (k_all, v_all)
        qm, km, vm, om, gts, gates = _in_mlstm(xp, mod_p, False, g1, w_in_bf, l, w_if, bif)
        oa = _attn_prompt(qT, kbf, vT, rel_bias, lam_p[l], sub_g, lam_init, min(ATTN_TILE, s))
        hm, c_p, n_new, m_new = _mlstm(qm, km, vm, om, gates, m_g, zero_c, 0, zero_n, zero_m, bp, s, tl_p,
                                       l, depth, c_p)
        xp = _merge(oa, hm, gts, xp, mod_p, False, wa, wm, wo)
        xp, buf_new = _ffn(xp, mod_p, False, g2, wu, conv_w[l], cb, wd, zero_buf, fg, bp, s, last)
        st_p.append((n_new, m_new, buf_new))

        mod_s = jnp.repeat(mod[l, bp:bp + bs], ts, axis=0)
        q, k_all, kbf, v_all, vbf = _in_attn(xs, mod_s, True, g1, w_in_bf, False, l, depth, kv_s)
        kv_s = (k_all, v_all)
        qm, km, vm, om, gts, gates = _in_mlstm(xs, mod_s, True, g1, w_in_bf, l, w_if, bif)
        oa = _attn_sample(q, kbf, vbf, cache_k, cache_v, l, rel_bias, lam_p[l], sub_g, lam_init, bs, ts)
        hm, c_s, n_new, m_new = _mlstm(qm, km, vm, om, gates, m_g, state_C, l, state_n[l], state_m[l], bs, ts, tl_s,
                                       l, depth, c_s)
        xs = _merge(oa, hm, gts, xs, mod_s, True, wa, wm, wo)
        xs, buf_new = _ffn(xs, mod_s, True, g2, wu, conv_w[l], cb, wd, state_conv[l], fg, bs, ts, last)
        st_s.append((n_new, m_new, buf_new))

    outs_p = [a.reshape(depth, bp, s, H_A, HE_A) for a in kv_p] + [c_p] + [jnp.stack(a) for a in zip(*st_p)]
    outs_s = [a.reshape(depth, bs, ts, H_A, HE_A) for a in kv_s] + [c_s] + [jnp.stack(a) for a in zip(*st_s)]
    return (xp.reshape(bp, s, d), xs.reshape(bs, ts, d), *outs_p, *outs_s)
```

```python
import functools
import math

import numpy as np
import jax
import jax.numpy as jnp
from jax import lax
from jax.experimental import pallas as pl
from jax.experimental.pallas import tpu as pltpu

F32 = jnp.float32
BF16 = jnp.bfloat16

CHUNK = 64
H_A = 8
HD_A = 64
HE_A = 2 * HD_A
H_M = 4
DH_M = 256
N_BUCKETS = 32
CONV_W = 3
NEG_INF = -1e30
EPS = 1e-6
FAR_BUCKET = N_BUCKETS // 2 - 1
LOG2E = math.log2(math.e)

V7X_VMEM_LIMIT = 56 * 1024 * 1024
ROW_TILE = 512
ATTN_TILE = 512
ATTN_FAR_BLOCKS = 4
ATTN_TILES_PER_STEP = 8
BF16_ROWS = 16
MLSTM_TILE = 256

NT_DIMS = (((1,), (1,)), ((), ()))


def _cparams(sem):
    return pltpu.CompilerParams(dimension_semantics=sem, vmem_limit_bytes=V7X_VMEM_LIMIT)


def _dot(a, b):
    return jnp.dot(a, b, preferred_element_type=F32)


def _rms(x, g):
    return x * lax.rsqrt(jnp.mean(x * x, axis=-1, keepdims=True) + EPS) * g


def _log_sigmoid(x):
    return jnp.minimum(x, 0.0) - jnp.log1p(jnp.exp(-jnp.abs(x)))


def _mod_spec(per_row, tm, d, col, row_index):
    if per_row:
        return pl.BlockSpec((tm, d), lambda *g: (row_index(*g), col))
    return pl.BlockSpec((1, d), lambda *g: (0, col))


def _ada_kernel(c_ref, w_ref, b_ref, o_ref):
    a = jax.nn.silu(c_ref[...]).astype(BF16)
    o_ref[...] = _dot(a, w_ref[...].astype(BF16)) + b_ref[...]


def _ada(c_all, ada_w, ada_b):
    depth, d, n = ada_w.shape
    rc = c_all.shape[0]
    tn = 1536
    return pl.pallas_call(
        _ada_kernel,
        grid=(depth, n // tn),
        in_specs=[pl.BlockSpec((rc, d), lambda l, j: (0, 0)),
                  pl.BlockSpec((None, d, tn), lambda l, j: (l, 0, j)),
                  pl.BlockSpec((None, 1, tn), lambda l, j: (l, 0, j))],
        out_specs=pl.BlockSpec((None, rc, tn), lambda l, j: (l, 0, j)),
        out_shape=jax.ShapeDtypeStruct((depth, rc, n), F32),
        compiler_params=_cparams(("arbitrary", "arbitrary")),
        name="ada_mod",
    )(c_all, ada_w, ada_b.reshape(depth, 1, n))


def _in_attn_kernel(x_ref, g_ref, sc_ref, sh_ref, w_ref, *refs, d, transposed, layer, creates):
    q_ref, k32_ref, kbf_ref, v32_ref, vbf_ref = refs[-5:]
    xn = (_rms(x_ref[...], g_ref[...]) * (1.0 + sc_ref[...]) + sh_ref[...]).astype(BF16)
    q = _dot(xn, w_ref[:, 0:d]) * (HD_A ** -0.5 * (LOG2E if transposed else 1.0))
    k = _dot(xn, w_ref[:, d:2 * d])
    v = _dot(xn, w_ref[:, 2 * d:3 * d])
    if creates:
        for l in range(k32_ref.shape[0]):
            k32_ref[l] = k if l == layer else jnp.zeros_like(k)
            v32_ref[l] = v if l == layer else jnp.zeros_like(v)
    else:
        k32_ref[...] = k
        v32_ref[...] = v
    kbf_ref[...] = k.astype(BF16)
    if transposed:
        q_ref[...] = q.T.astype(BF16)
        vbf_ref[...] = v.T.astype(BF16)
    else:
        q_ref[...] = q.astype(BF16)
        vbf_ref[...] = v.astype(BF16)


def _in_attn(x, mod, per_row, g, w_in, transposed, layer, depth, kv_states):
    r, d = x.shape
    tm = min(r, ROW_TILE)
    row = lambda i: i
    nat = pl.BlockSpec((tm, d), lambda i: (i, 0))
    tr = pl.BlockSpec((d, tm), lambda i: (0, i))
    prev = () if kv_states is None else tuple(kv_states)
    state = (pl.BlockSpec((None, tm, d), lambda i: (layer, i, 0)) if prev else
             pl.BlockSpec((depth, tm, d), lambda i: (0, i, 0)))
    state_shape = jax.ShapeDtypeStruct((depth, r, d), F32)
    bf_shape = jax.ShapeDtypeStruct((d, r) if transposed else (r, d), BF16)
    return pl.pallas_call(
        functools.partial(_in_attn_kernel, d=d, transposed=transposed, layer=layer, creates=not prev),
        grid=(r // tm,),
        in_specs=[nat, pl.BlockSpec((1, d), lambda i: (0, 0)),
                  _mod_spec(per_row, tm, d, 1, row), _mod_spec(per_row, tm, d, 0, row),
                  pl.BlockSpec((None, d, 3 * d), lambda i: (layer, 0, 0))]
        + [pl.BlockSpec(memory_space=pl.ANY)] * len(prev),
        out_specs=[tr if transposed else nat, state, nat, state, tr if transposed else nat],
        out_shape=[bf_shape, state_shape, jax.ShapeDtypeStruct((r, d), BF16), state_shape, bf_shape],
        input_output_aliases={5: 1, 6: 3} if prev else {},
        compiler_params=_cparams(("arbitrary",)),
        name="in_attn",
    )(x, g, mod, mod, w_in, *prev)


def _in_mlstm_kernel(x_ref, g_ref, sc_ref, sh_ref, w1_ref, w2_ref, wif_ref, bif_ref,
                     qm_ref, km_ref, vm_ref, om_ref, gts_ref, gate_ref, *, d):
    xn = (_rms(x_ref[...], g_ref[...]) * (1.0 + sc_ref[...]) + sh_ref[...]).astype(BF16)
    qm_ref[...] = _dot(xn, w1_ref[:, 0:d]).astype(BF16)
    km_ref[...] = (_dot(xn, w1_ref[:, d:2 * d]) * (DH_M ** -0.5)).astype(BF16)
    vm_ref[...] = _dot(xn, w1_ref[:, 2 * d:3 * d]).astype(BF16)
    om_ref[...] = _dot(xn, w2_ref[:, 0:d]).astype(BF16)
    gts_ref[...] = _dot(xn, w2_ref[:, d:3 * d]).astype(BF16)
    pre = _dot(xn, wif_ref[...]) + bif_ref[...]
    gate_ref[...] = pre[:, 0:2 * H_M]


def _in_mlstm(x, mod, per_row, g, w_in, layer, w_if, b_if):
    r, d = x.shape
    tm = min(r, ROW_TILE)
    row = lambda i: i
    nat = pl.BlockSpec((tm, d), lambda i: (i, 0))
    const = lambda shape: pl.BlockSpec(shape, lambda i: (0, 0))
    return pl.pallas_call(
        functools.partial(_in_mlstm_kernel, d=d),
        grid=(r // tm,),
        in_specs=[nat, const((1, d)), _mod_spec(per_row, tm, d, 1, row), _mod_spec(per_row, tm, d, 0, row),
                  pl.BlockSpec((None, d, 3 * d), lambda i: (layer, 0, 1)),
                  pl.BlockSpec((None, d, 3 * d), lambda i: (layer, 0, 2)),
                  const(w_if.shape), const(b_if.shape)],
        out_specs=[nat, nat, nat, nat, pl.BlockSpec((tm, 2 * d), lambda i: (i, 0)),
                   pl.BlockSpec((tm, 2 * H_M), lambda i: (i, 0))],
        out_shape=[jax.ShapeDtypeStruct((r, d), BF16)] * 4
        + [jax.ShapeDtypeStruct((r, 2 * d), BF16), jax.ShapeDtypeStruct((r, 2 * H_M), F32)],
        compiler_params=_cparams(("arbitrary",)),
        name="in_mlstm",
    )(x, g, mod, mod, w_in, w_in, w_if, b_if)


def _bucket_np(rel):
    half = N_BUCKETS // 2
    max_exact = half // 2
    n = np.abs(rel)
    thresholds = np.ceil(max_exact * 2.0 ** (np.arange(1, half - max_exact) / 2.0)).astype(np.int64)
    large = max_exact + (n[..., None] >= thresholds).sum(-1)
    return np.where(rel > 0, half, 0) + np.where(n < max_exact, n, np.minimum(large, half - 1))


def _lam(lp):
    s1 = jnp.sum(lp[0:1] * lp[1:2], axis=-1, keepdims=True)
    s2 = jnp.sum(lp[2:3] * lp[3:4], axis=-1, keepdims=True)
    return jnp.exp(s1) - jnp.exp(s2)


def _bias_steps(lo, hi):
    rel = np.arange(lo, hi + 1)
    b = _bucket_np(rel)
    starts = np.concatenate([[0], np.nonzero(b[1:] != b[:-1])[0] + 1])
    return [(int(rel[k]), int(b[k])) for k in starts]


def _rel_bias_tile(rb_ref, h, rel, lo, hi, shift, scale):
    steps = _bias_steps(lo, hi)
    v = jnp.full(rel.shape, (rb_ref[steps[0][1], h] - shift) * scale, F32)
    for start, bucket in steps[1:]:
        v = jnp.where(rel >= start, (rb_ref[bucket, h] - shift) * scale, v)
    return v


def _attn_prompt_tile(rb_ref, lamp_ref, qT_ref, k_ref, vT_ref, subg_ref, o_ref,
                      q_s, bias_s, m_s, l_s, acc_s, *, i, qsl, t, far_blocks, lam_init):
    h = pl.program_id(0)
    rows = 64
    chunk_shift = CHUNK.bit_length() - 1

    @pl.when(i == 0)
    def _():
        far = rb_ref[FAR_BUCKET, h]
        for tile, shift in enumerate((t, 0)):

            def fill(r, carry, tile=tile, shift=shift):
                r0 = pl.multiple_of(r * rows, rows)
                kk = lax.broadcasted_iota(jnp.int32, (rows, t), 0) + r0
                qq = lax.broadcasted_iota(jnp.int32, (rows, t), 1)
                v = _rel_bias_tile(rb_ref, h, kk - qq - shift, -shift - (t - 1), -shift + (t - 1), far, LOG2E)
                if tile == 1:
                    visible = jnp.right_shift(kk, chunk_shift) <= jnp.right_shift(qq, chunk_shift)
                    v = jnp.where(visible, v, NEG_INF)
                bias_s[tile, pl.ds(r0, rows), :] = v
                return carry

            lax.fori_loop(0, t // rows, fill, 0)

    lam = _lam(lamp_ref[...]) + lam_init
    qT = qT_ref[:, qsl]
    rowi = lax.broadcasted_iota(jnp.int32, qT.shape, 0)
    zero = jnp.zeros_like(qT)
    q_s[0] = jnp.where(rowi < HD_A, qT, zero)
    q_s[1] = jnp.where(rowi >= HD_A, qT, zero)

    def keys(j0, n):
        off = pl.multiple_of(j0 * t, t)
        return k_ref[pl.ds(off, n), :], vT_ref[:, pl.ds(off, n)]

    def scores(kb, c, tile):
        s = _dot(kb, q_s[c])
        return s if tile is None else s + bias_s[tile]

    def fixed_shift_block(j0, n, tile):
        kb, vb = keys(j0, n)
        for c in range(2):
            e = jnp.exp2(scores(kb, c, tile) - m_s[c])
            l_s[c] += jnp.sum(e, axis=0, keepdims=True)
            acc_s[c] += _dot(vb, e.astype(BF16))

    def running_max_block(j0, tile):
        kb, vb = keys(j0, t)
        for c in range(2):
            s = scores(kb, c, tile)
            m_old = m_s[c]
            m_new = jnp.maximum(m_old, jnp.max(s, axis=0, keepdims=True))
            alpha = jnp.exp2(m_old - m_new)
            e = jnp.exp2(s - m_new)
            l_s[c] = alpha * l_s[c] + jnp.sum(e, axis=0, keepdims=True)
            acc_s[c] = alpha * acc_s[c] + _dot(vb, e.astype(BF16))
            m_s[c] = m_new

    def finish():
        l0, l1 = l_s[0], l_s[1]
        oT = acc_s[0] * (1.0 / l0) - acc_s[1] * (lam / l1)
        inv = lax.rsqrt(jnp.mean(oT * oT, axis=0, keepdims=True) + EPS)
        o_ref[qsl, :] = ((oT * inv).T * (subg_ref[...] * (1.0 - lam_init))).astype(BF16)
        big = float(np.finfo(np.float32).max)
        bad = jnp.max(jnp.where(jnp.abs(oT) <= big, 0.0, 1.0), axis=0, keepdims=True)
        return jnp.max(jnp.maximum(bad, jnp.where((l0 <= big) & (l1 <= big), 0.0, 1.0)))

    def first_step(r):
        nb = max(r, 0) + (2 if r >= 0 else 1)
        kb, vb = keys(i - (nb - 1), nb * t)
        for c in range(2):
            s = _dot(kb, q_s[c])
            parts = [s[0:r * t]] if r > 0 else []
            if r >= 0:
                parts.append(s[(nb - 2) * t:(nb - 1) * t] + bias_s[0])
            parts.append(s[(nb - 1) * t:nb * t] + bias_s[1])
            e = [jnp.exp2(p - m_s[c]) for p in parts]
            l_s[c] = functools.reduce(jnp.add, [jnp.sum(p, axis=0, keepdims=True) for p in e])
            e = [p.astype(BF16) for p in e]
            acc_s[c] = _dot(vb, e[0] if len(e) == 1 else jnp.concatenate(e, axis=0))

    k0 = k_ref[pl.ds(pl.multiple_of(i * t, t), CHUNK), :]
    for c in range(2):
        m_s[c] = jnp.max(_dot(k0, q_s[c]) + bias_s[1, 0:CHUNK, :], axis=0, keepdims=True)

    nfar = jnp.maximum(i - 1, 0)
    folded = jnp.bitwise_and(nfar, far_blocks - 1)

    @pl.when(i == 0)
    def _():
        first_step(-1)

    for r in range(far_blocks):
        @pl.when((i >= 1) & (folded == r))
        def _(r=r):
            first_step(r)

    groups = lax.div(nfar, far_blocks)
    odd = jnp.bitwise_and(groups, 1)

    @pl.when(odd == 1)
    def _():
        fixed_shift_block(0, far_blocks * t, None)

    def big_step(p, carry):
        fixed_shift_block((odd + 2 * p) * far_blocks, 2 * far_blocks * t, None)
        return carry

    lax.fori_loop(0, lax.shift_right_logical(groups, 1), big_step, 0)

    @pl.when(finish() > 0.0)
    def _():
        m_s[...] = jnp.full(m_s.shape, NEG_INF, F32)
        l_s[...] = jnp.zeros(l_s.shape, F32)
        acc_s[...] = jnp.zeros(acc_s.shape, F32)

        def step(j, carry):
            running_max_block(j, None)
            return carry

        lax.fori_loop(0, nfar, step, 0)

        @pl.when(i >= 1)
        def _():
            running_max_block(i - 1, 0)

        running_max_block(i, 1)
        finish()


def _attn_prompt_kernel(*refs, t, tiles, far_blocks, lam_init):
    def one_tile(n, carry):
        _attn_prompt_tile(*refs, i=pl.program_id(1) * tiles + n, qsl=pl.ds(pl.multiple_of(n * t, t), t),
                          t=t, far_blocks=far_blocks, lam_init=lam_init)
        return carry

    lax.fori_loop(0, tiles, one_tile, 0)


def _attn_prompt(qT, kbf, vT, rel_bias, lam_p, sub_g, lam_init, t):
    d, s = qT.shape
    assert s % t == 0 and t % 64 == 0 and t % CHUNK == 0 and CHUNK & (CHUNK - 1) == 0
    assert _bucket_np(np.array([-(t + 1)]))[0] == FAR_BUCKET
    assert ATTN_FAR_BLOCKS & (ATTN_FAR_BLOCKS - 1) == 0
    nq = s // t
    tiles = math.gcd(nq, ATTN_TILES_PER_STEP)
    return pl.pallas_call(
        functools.partial(_attn_prompt_kernel, t=t, tiles=tiles, far_blocks=ATTN_FAR_BLOCKS, lam_init=lam_init),
        grid=(H_A, nq // tiles),
        in_specs=[pl.BlockSpec(memory_space=pltpu.SMEM),
                  pl.BlockSpec((4, HD_A), lambda h, i: (0, 0)),
                  pl.BlockSpec((HE_A, tiles * t), lambda h, i: (h, i)),
                  pl.BlockSpec((s, HE_A), lambda h, i: (0, h)),
                  pl.BlockSpec((HE_A, s), lambda h, i: (h, 0)),
                  pl.BlockSpec((1, HE_A), lambda h, i: (0, 0))],
        out_specs=pl.BlockSpec((tiles * t, HE_A), lambda h, i: (i, h)),
        out_shape=jax.ShapeDtypeStruct((s, d), BF16),
        scratch_shapes=[pltpu.VMEM((2, HE_A, t), BF16), pltpu.VMEM((2, t, t), F32),
                        pltpu.VMEM((2, 1, t), F32), pltpu.VMEM((2, 1, t), F32), pltpu.VMEM((2, HE_A, t), F32)],
        compiler_params=_cparams(("arbitrary", "arbitrary")),
        name="attn_prompt",
    )(rel_bias.astype(F32), lam_p, qT, kbf, vT, sub_g)


def _attn_sample_kernel(rb_ref, lamp_ref, q_ref, kc_ref, vc_ref, kn_ref, vn_ref, subg_ref, o_ref,
                        *, past, lam_init):
    t = q_ref.shape[0]
    chunk_shift = CHUNK.bit_length() - 1
    lam = _lam(lamp_ref[...]) + lam_init

    def positions(n_keys, first_key):
        qpos = past + lax.broadcasted_iota(jnp.int32, (t, n_keys), 0)
        kpos = first_key + lax.broadcasted_iota(jnp.int32, (t, n_keys), 1)
        return kpos - qpos, jnp.right_shift(kpos, chunk_shift) <= jnp.right_shift(qpos, chunk_shift)

    rel_c, vis_c = positions(past, 0)
    rel_n, vis_n = positions(t, past)
    for h in range(H_A):
        sl = slice(h * HE_A, (h + 1) * HE_A)
        bias_c = jnp.where(vis_c, _rel_bias_tile(rb_ref, h, rel_c, -(past + t - 1), -1, 0.0, 1.0), NEG_INF)
        bias_n = jnp.where(vis_n, _rel_bias_tile(rb_ref, h, rel_n, -(t - 1), t - 1, 0.0, 1.0), NEG_INF)
        q = q_ref[:, sl]
        lane = lax.broadcasted_iota(jnp.int32, q.shape, 1)
        zero = jnp.zeros_like(q)
        q2 = jnp.concatenate([jnp.where(lane < HD_A, q, zero), jnp.where(lane >= HD_A, q, zero)], axis=0)
        kc = kc_ref[pl.ds(h, past, stride=H_A), :].astype(BF16)
        sc = lax.dot_general(q2, kc, NT_DIMS, preferred_element_type=F32) + jnp.concatenate([bias_c, bias_c], 0)
        sn = (lax.dot_general(q2, kn_ref[:, sl], NT_DIMS, preferred_element_type=F32)
              + jnp.concatenate([bias_n, bias_n], 0))
        m = jnp.maximum(jnp.max(sc, axis=-1, keepdims=True), jnp.max(sn, axis=-1, keepdims=True))
        ec = jnp.exp(sc - m)
        en = jnp.exp(sn - m)
        l = jnp.sum(ec, axis=-1, keepdims=True) + jnp.sum(en, axis=-1, keepdims=True)
        pc = ec / l
        pn = en / l
        ac = pc[0:t] - lam * pc[t:2 * t]
        an = pn[0:t] - lam * pn[t:2 * t]
        vc = vc_ref[pl.ds(h, past, stride=H_A), :].astype(BF16)
        o = _dot(ac.astype(BF16), vc) + _dot(an.astype(BF16), vn_ref[:, sl])
        o_ref[:, sl] = (_rms(o, subg_ref[...]) * (1.0 - lam_init)).astype(BF16)


def _attn_sample(q, kbf, vbf, cache_k, cache_v, layer, rel_bias, lam_p, sub_g, lam_init, b, t):
    r, d = q.shape
    past = cache_k.shape[2]
    rows = pl.BlockSpec((t, d), lambda i: (i, 0))
    cache_k = cache_k.reshape(-1, HE_A)
    cache_v = cache_v.reshape(-1, HE_A)
    cache = pl.BlockSpec((past * H_A, HE_A), lambda i: (layer * b + i, 0))
    return pl.pallas_call(
        functools.partial(_attn_sample_kernel, past=past, lam_init=lam_init),
        grid=(b,),
        in_specs=[pl.BlockSpec(memory_space=pltpu.SMEM),
                  pl.BlockSpec((4, HD_A), lambda i: (0, 0)),
                  rows, cache, cache, rows, rows,
                  pl.BlockSpec((1, HE_A), lambda i: (0, 0))],
        out_specs=rows,
        out_shape=jax.ShapeDtypeStruct((r, d), BF16),
        compiler_params=_cparams(("arbitrary",)),
        name="attn_sample",
    )(rel_bias.astype(F32), lam_p, q, cache_k, cache_v, kbf, vbf, sub_g)


def _mlstm_kernel(q_ref, k_ref, v_ref, om_ref, g_ref, gT_ref, mg_ref, c0_ref, n0_ref, m0_ref, *refs,
                  tl, layer, creates):
    h_ref, c_ref, n_ref, m_ref, c_s, n_s, m_s = refs[-7:]
    step = pl.program_id(1)

    @pl.when(step == 0)
    def _():
        c_s[...] = c0_ref[...]
        n_s[...] = n0_ref[...]
        m_s[...] = m0_ref[...]

    ti = lax.broadcasted_iota(jnp.int32, (tl, tl), 0)
    si = lax.broadcasted_iota(jnp.int32, (tl, tl), 1)
    causal = si <= ti
    gates_c = g_ref[...]
    gates_r = gT_ref[...]
    lf_cols = _log_sigmoid(gates_c)
    lf_rows = _log_sigmoid(gates_r)
    for h in range(H_M):
        sl = slice(h * DH_M, (h + 1) * DH_M)
        q = q_ref[:, sl]
        k = k_ref[:, sl]
        v = v_ref[:, sl]
        ig_c = gates_c[:, h:h + 1]
        lf_c = lf_cols[:, H_M + h:H_M + h + 1]
        ig_r = gates_r[h:h + 1, :]
        lf_r = lf_rows[H_M + h:H_M + h + 1, :]
        m = m_s[h]
        n_row = n_s[h]
        b_c = jnp.sum(jnp.where(causal, lf_r, 0.0), axis=1, keepdims=True)
        b_r = jnp.sum(jnp.where(ti <= si, lf_c, 0.0), axis=0, keepdims=True)
        dmat = jnp.where(causal, b_c - b_r + ig_r, -jnp.inf)
        inter = b_c + m
        m_t = jnp.maximum(inter, jnp.max(dmat, axis=1, keepdims=True))
        w_intra = jnp.exp(dmat - m_t)
        w_inter = jnp.exp(inter - m_t)
        a = w_intra * lax.dot_general(q, k, NT_DIMS, preferred_element_type=F32)
        num = w_inter * _dot(q, c_s[h].astype(BF16)) + _dot(a.astype(BF16), v)
        den = (w_inter * jnp.sum(q.astype(F32) * n_row, axis=1, keepdims=True)
               + jnp.sum(a, axis=1, keepdims=True))
        hh = num / jnp.maximum(jnp.abs(den), jnp.exp(-m_t))
        b_last = b_r[:, tl - 1:tl]
        g_c = b_last - b_c + ig_c
        m_new = jnp.maximum(b_last + m, jnp.max(g_c, axis=0, keepdims=True))
        decay = jnp.exp(b_last + m - m_new)
        kw = jnp.exp(g_c - m_new) * k.astype(F32)
        c_s[h] = decay * c_s[h] + _dot(kw.T.astype(BF16), v)
        n_s[h] = decay * n_row + jnp.sum(kw, axis=0, keepdims=True)
        m_s[h] = m_new
        hn = _rms(hh, mg_ref[:, sl])
        h_ref[:, sl] = (hn * jax.nn.sigmoid(om_ref[:, sl].astype(F32))).astype(BF16)

    @pl.when(step == pl.num_programs(1) - 1)
    def _():
        if creates:
            for l in range(c_ref.shape[0]):
                c_ref[l] = c_s[...] if l == layer else jnp.zeros(c_s.shape, F32)
        else:
            c_ref[...] = c_s[...]
        n_ref[...] = n_s[...]
        m_ref[...] = m_s[...]


def _mlstm(qm, km, vm, om, gates, m_g, c0, c0_layer, n0, m0, b, t, tl, layer, depth, c_states):
    r, d = qm.shape
    nc = t // tl
    gates_t = jnp.transpose(gates.reshape(b * nc, tl, 2 * H_M), (0, 2, 1))
    rows = pl.BlockSpec((tl, d), lambda i, c: (i * nc + c, 0))
    st_n = pl.BlockSpec((None, H_M, 1, DH_M), lambda i, c: (i, 0, 0, 0))
    st_m = pl.BlockSpec((None, H_M, 1, 1), lambda i, c: (i, 0, 0, 0))
    prev = () if c_states is None else (c_states,)
    c_out = (pl.BlockSpec((None, None, H_M, DH_M, DH_M), lambda i, c: (layer, i, 0, 0, 0)) if prev else
             pl.BlockSpec((depth, None, H_M, DH_M, DH_M), lambda i, c: (0, i, 0, 0, 0)))
    h, c_states, n_new, m_new = pl.pallas_call(
        functools.partial(_mlstm_kernel, tl=tl, layer=layer, creates=not prev),
        grid=(b, nc),
        in_specs=[rows, rows, rows, rows,
                  pl.BlockSpec((tl, 2 * H_M), lambda i, c: (i * nc + c, 0)),
                  pl.BlockSpec((None, 2 * H_M, tl), lambda i, c: (i * nc + c, 0, 0)),
                  pl.BlockSpec((1, d), lambda i, c: (0, 0)),
                  pl.BlockSpec((None, None, H_M, DH_M, DH_M), lambda i, c: (c0_layer, i, 0, 0, 0)),
                  st_n, st_m] + [pl.BlockSpec(memory_space=pl.ANY)] * len(prev),
        out_specs=[rows, c_out, st_n, st_m],
        out_shape=[jax.ShapeDtypeStruct((r, d), BF16),
                   jax.ShapeDtypeStruct((depth, b, H_M, DH_M, DH_M), F32),
                   jax.ShapeDtypeStruct((b, H_M, 1, DH_M), F32),
                   jax.ShapeDtypeStruct((b, H_M, 1, 1), F32)],
        input_output_aliases={10: 1} if prev else {},
        scratch_shapes=[pltpu.VMEM((H_M, DH_M, DH_M), F32), pltpu.VMEM((H_M, 1, DH_M), F32),
                        pltpu.VMEM((H_M, 1, 1), F32)],
        compiler_params=_cparams(("arbitrary", "arbitrary")),
        name="mlstm",
    )(qm, km, vm, om, gates, gates_t, m_g, c0, n0.reshape(b, H_M, 1, DH_M), m0.reshape(b, H_M, 1, 1), *prev)
    return h, c_states, n_new.reshape(b, H_M, DH_M), m_new.reshape(b, H_M)


def _merge_kernel(oa_ref, hm_ref, gts_ref, x_ref, g1_ref, wa_ref, wm_ref, wo_ref, o_ref, *, d):
    ya = _dot(oa_ref[...], wa_ref[...])
    ym = _dot(hm_ref[...], wm_ref[...])
    mix = (jax.nn.sigmoid(gts_ref[:, 0:d].astype(F32)) * ya
           + jax.nn.sigmoid(gts_ref[:, d:2 * d].astype(F32)) * ym)
    o_ref[...] = x_ref[...] + g1_ref[...] * _dot(mix.astype(BF16), wo_ref[...])


def _merge(oa, hm, gts, x, mod, per_row, wa, wm, wo, layer):
    r, d = x.shape
    tm = min(r, ROW_TILE)
    nat = pl.BlockSpec((tm, d), lambda i: (i, 0))
    wspec = pl.BlockSpec((None, d, d), lambda i: (layer, 0, 0))
    return pl.pallas_call(
        functools.partial(_merge_kernel, d=d),
        grid=(r // tm,),
        in_specs=[nat, nat, pl.BlockSpec((tm, 2 * d), lambda i: (i, 0)), nat,
                  _mod_spec(per_row, tm, d, 2, lambda i: i), wspec, wspec, wspec],
        out_specs=nat,
        out_shape=jax.ShapeDtypeStruct((r, d), F32),
        compiler_params=_cparams(("arbitrary",)),
        name="merge_out",
    )(oa, hm, gts, x, mod, wa, wm, wo)


def _ffn_kernel(x_ref, g_ref, sc_ref, sh_ref, g2_ref, wu_ref, cw_ref, cb_ref, wd_ref, buf_ref, fg_ref,
                o_ref, nb_ref, tail_s, *, tm, dff, nc, final):
    first = pl.program_id(1) == 0
    ck = dff // nc
    x = x_ref[...]
    xn = (_rms(x, g_ref[...]) * (1.0 + sc_ref[...]) + sh_ref[...]).astype(BF16)
    rowi = lax.broadcasted_iota(jnp.int32, (tm, ck), 0)
    acc = None
    for c in range(nc):
        cs = slice(c * ck, (c + 1) * ck)
        u = _dot(xn, wu_ref[:, cs])
        gate = _dot(xn, wu_ref[:, dff + c * ck:dff + (c + 1) * ck])
        prev2 = jnp.where(first, buf_ref[0:1, cs], tail_s[6:7, cs])
        prev1 = jnp.where(first, buf_ref[1:2, cs], tail_s[7:8, cs])
        u1 = jnp.where(rowi == 0, prev1, pltpu.roll(u, 1, 0))
        u2 = jnp.where(rowi == 0, prev2, jnp.where(rowi == 1, prev1, pltpu.roll(u, 2, 0)))
        conv = cb_ref[:, cs] + cw_ref[0:1, cs] * u2 + cw_ref[1:2, cs] * u1 + cw_ref[2:3, cs] * u
        part = _dot((jax.nn.gelu(conv) * gate).astype(BF16), wd_ref[cs, :])
        acc = part if acc is None else acc + part
        tail_s[:, cs] = u[tm - 8:tm, :]
        nb_ref[:, cs] = u[tm - (CONV_W - 1):tm, :]
    y = x + g2_ref[...] * acc
    if final:
        y = _rms(y, fg_ref[...])
    o_ref[...] = y


def _ffn(x, mod, per_row, g, w_up, conv_w, conv_b, w_down, layer, conv_buf, final_g, b, t, final):
    r, d = x.shape
    dff = w_down.shape[1]
    tm = min(t, ROW_TILE)
    nt = t // tm
    nc = 2
    assert dff % (nc * 128) == 0 and tm >= 8
    row = lambda i, j: i * nt + j
    nat = pl.BlockSpec((tm, d), lambda i, j: (i * nt + j, 0))
    whole = lambda a: pl.BlockSpec(a.shape, lambda i, j: (0,) * a.ndim)
    at_layer = lambda a: pl.BlockSpec((None,) + a.shape[1:], lambda i, j: (layer, 0, 0))
    y, tails = pl.pallas_call(
        functools.partial(_ffn_kernel, tm=tm, dff=dff, nc=nc, final=final),
        grid=(b, nt),
        in_specs=[nat, whole(g), _mod_spec(per_row, tm, d, 4, row), _mod_spec(per_row, tm, d, 3, row),
                  _mod_spec(per_row, tm, d, 5, row), at_layer(w_up), whole(conv_w), whole(conv_b), at_layer(w_down),
                  pl.BlockSpec((None, CONV_W - 1, dff), lambda i, j: (i, 0, 0)), whole(final_g)],
        out_specs=[nat, pl.BlockSpec((None, None, CONV_W - 1, dff), lambda i, j: (i, j, 0, 0))],
        out_shape=[jax.ShapeDtypeStruct((r, d), F32), jax.ShapeDtypeStruct((b, nt, CONV_W - 1, dff), F32)],
        scratch_shapes=[pltpu.VMEM((8, dff), F32)],
        compiler_params=_cparams(("arbitrary", "arbitrary")),
        name="conv_ffn",
    )(x, g, mod, mod, mod, w_up, conv_w, conv_b, w_down, conv_buf, final_g)
    return y, tails[:, nt - 1]


def kernel(x_prompt, x_sample, c_prompt, c_sample, cache_k, cache_v, state_C, state_n, state_m, state_conv,
           rel_bias, ada_w, ada_b, norm1_g, norm2_g, w_in, b_if, lam_p, attn_sub_g, mlstm_g, w_br_a, w_br_m,
           w_out, w_up, conv_w, conv_b, w_down, final_g):
    bp, s, d = x_prompt.shape
    bs, ts, _ = x_sample.shape
    depth = w_in.shape[0]
    dff = w_down.shape[1]
    assert bp == 1 and d == H_A * HE_A == H_M * DH_M

    c_all = jnp.concatenate([c_prompt, c_sample, jnp.zeros((16 - bp - bs, d), F32)], axis=0)
    mod = _ada(c_all, ada_w, ada_b)

    xp = x_prompt.reshape(bp * s, d)
    xs = x_sample.reshape(bs * ts, d)
    tl_p = min(MLSTM_TILE, s)
    tl_s = min(CHUNK, ts)
    zero_c = jnp.zeros((1, bp, H_M, DH_M, DH_M), F32)
    zero_n = jnp.zeros((bp, H_M, DH_M), F32)
    zero_m = jnp.zeros((bp, H_M), F32)
    zero_buf = jnp.zeros((bp, CONV_W - 1, dff), F32)
    w_in_bf = w_in.astype(BF16)
    wa, wm, wo = w_br_a.astype(BF16), w_br_m.astype(BF16), w_out.astype(BF16)
    wu, wd = w_up.astype(BF16), w_down.astype(BF16)
    assert (w_in.shape[2] - 2 * H_M) == 9 * d
    st_p, st_s = [], []
    kv_p = kv_s = c_p = c_s = None
    for l in range(depth):
        lam_init = 0.8 - 0.6 * math.exp(-0.3 * l)
        last = l == depth - 1
        w_if = jnp.pad(w_in[l, :, 9 * d:], ((0, 0), (0, 128 - 2 * H_M))).astype(BF16)
        bif = jnp.pad(b_if[l], (0, 128 - 2 * H_M)).reshape(1, 128)
        g1, g2 = norm1_g[l].reshape(1, d), norm2_g[l].reshape(1, d)
        sub_g, m_g = attn_sub_g[l].reshape(1, HE_A), mlstm_g[l].reshape(1, d)
        cb = conv_b[l].reshape(1, dff)
        fg = final_g.reshape(1, d)

        mod_p = mod[l, 0:bp]
        qT, k_all, kbf, v_all, vT = _in_attn(xp, mod_p, False, g1, w_in_bf, True, l, depth, kv_p)
        kv_p = (k_all, v_all)
        qm, km, vm, om, gts, gates = _in_mlstm(xp, mod_p, False, g1, w_in_bf, l, w_if, bif)
        oa = _attn_prompt(qT, kbf, vT, rel_bias, lam_p[l], sub_g, lam_init, min(ATTN_TILE, s))
        hm, c_p, n_new, m_new = _mlstm(qm, km, vm, om, gates, m_g, zero_c, 0, zero_n, zero_m, bp, s, tl_p,
                                       l, depth, c_p)
        xp = _merge(oa, hm, gts, xp, mod_p, False, wa, wm, wo, l)
        xp, buf_new = _ffn(xp, mod_p, False, g2, wu, conv_w[l], cb, wd, l, zero_buf, fg, bp, s, last)
        st_p.append((n_new, m_new, buf_new))

        mod_s = jnp.repeat(mod[l, bp:bp + bs], ts, axis=0)
        q, k_all, kbf, v_all, vbf = _in_attn(xs, mod_s, True, g1, w_in_bf, False, l, depth, kv_s)
        kv_s = (k_all, v_all)
        qm, km, vm, om, gts, gates = _in_mlstm(xs, mod_s, True, g1, w_in_bf, l, w_if, bif)
        oa = _attn_sample(q, kbf, vbf, cache_k, cache_v, l, rel_bias, lam_p[l], sub_g, lam_init, bs, ts)
        hm, c_s, n_new, m_new = _mlstm(qm, km, vm, om, gates, m_g, state_C, l, state_n[l], state_m[l], bs, ts, tl_s,
                                       l, depth, c_s)
        xs = _merge(oa, hm, gts, xs, mod_s, True, wa, wm, wo, l)
        xs, buf_new = _ffn(xs, mod_s, True, g2, wu, conv_w[l], cb, wd, l, state_conv[l], fg, bs, ts, last)
        st_s.append((n_new, m_new, buf_new))

    outs_p = [a.reshape(depth, bp, s, H_A, HE_A) for a in kv_p] + [c_p] + [jnp.stack(a) for a in zip(*st_p)]
    outs_s = [a.reshape(depth, bs, ts, H_A, HE_A) for a in kv_s] + [c_s] + [jnp.stack(a) for a in zip(*st_s)]
    return (xp.reshape(bp, s, d), xs.reshape(bs, ts, d), *outs_p, *outs_s)
```

```python
import functools
import math

import numpy as np
import jax
import jax.numpy as jnp
from jax import lax
from jax.experimental import pallas as pl
from jax.experimental.pallas import tpu as pltpu

F32 = jnp.float32
BF16 = jnp.bfloat16

CHUNK = 64
H_A = 8
HD_A = 64
HE_A = 2 * HD_A
H_M = 4
DH_M = 256
N_BUCKETS = 32
CONV_W = 3
NEG_INF = -1e30
EPS = 1e-6
FAR_BUCKET = N_BUCKETS // 2 - 1
LOG2E = math.log2(math.e)

V7X_VMEM_LIMIT = 56 * 1024 * 1024
ROW_TILE = 512
ATTN_TILE = 512
ATTN_FAR_BLOCKS = 4
ATTN_TILES_PER_STEP = 8
MLSTM_TILE = 256

NT_DIMS = (((1,), (1,)), ((), ()))


def _cparams(sem):
    return pltpu.CompilerParams(dimension_semantics=sem, vmem_limit_bytes=V7X_VMEM_LIMIT)


def _dot(a, b):
    return jnp.dot(a, b, preferred_element_type=F32)


def _rms(x, g):
    return x * lax.rsqrt(jnp.mean(x * x, axis=-1, keepdims=True) + EPS) * g


def _log_sigmoid(x):
    return jnp.minimum(x, 0.0) - jnp.log1p(jnp.exp(-jnp.abs(x)))


def _mod_spec(per_row, tm, d, col, row_index):
    if per_row:
        return pl.BlockSpec((tm, d), lambda *g: (row_index(*g), col))
    return pl.BlockSpec((1, d), lambda *g: (0, col))


def _ada_kernel(c_ref, w_ref, b_ref, o_ref):
    a = jax.nn.silu(c_ref[...]).astype(BF16)
    o_ref[...] = _dot(a, w_ref[...].astype(BF16)) + b_ref[...]


def _ada(c_all, ada_w, ada_b):
    depth, d, n = ada_w.shape
    rc = c_all.shape[0]
    tn = 1536
    return pl.pallas_call(
        _ada_kernel,
        grid=(depth, n // tn),
        in_specs=[pl.BlockSpec((rc, d), lambda l, j: (0, 0)),
                  pl.BlockSpec((None, d, tn), lambda l, j: (l, 0, j)),
                  pl.BlockSpec((None, 1, tn), lambda l, j: (l, 0, j))],
        out_specs=pl.BlockSpec((None, rc, tn), lambda l, j: (l, 0, j)),
        out_shape=jax.ShapeDtypeStruct((depth, rc, n), F32),
        compiler_params=_cparams(("arbitrary", "arbitrary")),
        name="ada_mod",
    )(c_all, ada_w, ada_b.reshape(depth, 1, n))


def _in_attn_kernel(x_ref, g_ref, sc_ref, sh_ref, w_ref, *refs, d, transposed, layer, creates):
    q_ref, k32_ref, kbf_ref, v32_ref, vbf_ref = refs[-5:]
    xn = (_rms(x_ref[...], g_ref[...]) * (1.0 + sc_ref[...]) + sh_ref[...]).astype(BF16)
    q = _dot(xn, w_ref[:, 0:d]) * (HD_A ** -0.5 * (LOG2E if transposed else 1.0))
    k = _dot(xn, w_ref[:, d:2 * d])
    v = _dot(xn, w_ref[:, 2 * d:3 * d])
    if creates:
        for l in range(k32_ref.shape[0]):
            k32_ref[l] = k if l == layer else jnp.zeros_like(k)
            v32_ref[l] = v if l == layer else jnp.zeros_like(v)
    else:
        k32_ref[...] = k
        v32_ref[...] = v
    kbf_ref[...] = k.astype(BF16)
    if transposed:
        q_ref[...] = q.T.astype(BF16)
        vbf_ref[...] = v.T.astype(BF16)
    else:
        q_ref[...] = q.astype(BF16)
        vbf_ref[...] = v.astype(BF16)


def _in_attn(x, mod, per_row, g, w_in, transposed, layer, depth, kv_states):
    r, d = x.shape
    tm = min(r, ROW_TILE)
    row = lambda i: i
    nat = pl.BlockSpec((tm, d), lambda i: (i, 0))
    tr = pl.BlockSpec((d, tm), lambda i: (0, i))
    prev = () if kv_states is None else tuple(kv_states)
    state = (pl.BlockSpec((None, tm, d), lambda i: (layer, i, 0)) if prev else
             pl.BlockSpec((depth, tm, d), lambda i: (0, i, 0)))
    state_shape = jax.ShapeDtypeStruct((depth, r, d), F32)
    bf_shape = jax.ShapeDtypeStruct((d, r) if transposed else (r, d), BF16)
    return pl.pallas_call(
        functools.partial(_in_attn_kernel, d=d, transposed=transposed, layer=layer, creates=not prev),
        grid=(r // tm,),
        in_specs=[nat, pl.BlockSpec((1, d), lambda i: (0, 0)),
                  _mod_spec(per_row, tm, d, 1, row), _mod_spec(per_row, tm, d, 0, row),
                  pl.BlockSpec((None, d, 3 * d), lambda i: (layer, 0, 0))]
        + [pl.BlockSpec(memory_space=pl.ANY)] * len(prev),
        out_specs=[tr if transposed else nat, state, nat, state, tr if transposed else nat],
        out_shape=[bf_shape, state_shape, jax.ShapeDtypeStruct((r, d), BF16), state_shape, bf_shape],
        input_output_aliases={5: 1, 6: 3} if prev else {},
        compiler_params=_cparams(("arbitrary",)),
        name="in_attn",
    )(x, g, mod, mod, w_in, *prev)


def _in_mlstm_kernel(x_ref, g_ref, sc_ref, sh_ref, w1_ref, w2_ref, wif_ref, bif_ref,
                     qm_ref, km_ref, vm_ref, om_ref, gts_ref, gate_ref, *, d):
    xn = (_rms(x_ref[...], g_ref[...]) * (1.0 + sc_ref[...]) + sh_ref[...]).astype(BF16)
    qm_ref[...] = _dot(xn, w1_ref[:, 0:d]).astype(BF16)
    km_ref[...] = (_dot(xn, w1_ref[:, d:2 * d]) * (DH_M ** -0.5)).astype(BF16)
    vm_ref[...] = _dot(xn, w1_ref[:, 2 * d:3 * d]).astype(BF16)
    om_ref[...] = _dot(xn, w2_ref[:, 0:d]).astype(BF16)
    gts_ref[...] = _dot(xn, w2_ref[:, d:3 * d]).astype(BF16)
    pre = _dot(xn, wif_ref[...]) + bif_ref[...]
    gate_ref[...] = pre[:, 0:2 * H_M]


def _in_mlstm(x, mod, per_row, g, w_in, layer, w_if, b_if):
    r, d = x.shape
    tm = min(r, ROW_TILE)
    row = lambda i: i
    nat = pl.BlockSpec((tm, d), lambda i: (i, 0))
    const = lambda shape: pl.BlockSpec(shape, lambda i: (0, 0))
    return pl.pallas_call(
        functools.partial(_in_mlstm_kernel, d=d),
        grid=(r // tm,),
        in_specs=[nat, const((1, d)), _mod_spec(per_row, tm, d, 1, row), _mod_spec(per_row, tm, d, 0, row),
                  pl.BlockSpec((None, d, 3 * d), lambda i: (layer, 0, 1)),
                  pl.BlockSpec((None, d, 3 * d), lambda i: (layer, 0, 2)),
                  const(w_if.shape), const(b_if.shape)],
        out_specs=[nat, nat, nat, nat, pl.BlockSpec((tm, 2 * d), lambda i: (i, 0)),
                   pl.BlockSpec((tm, 2 * H_M), lambda i: (i, 0))],
        out_shape=[jax.ShapeDtypeStruct((r, d), BF16)] * 4
        + [jax.ShapeDtypeStruct((r, 2 * d), BF16), jax.ShapeDtypeStruct((r, 2 * H_M), F32)],
        compiler_params=_cparams(("arbitrary",)),
        name="in_mlstm",
    )(x, g, mod, mod, w_in, w_in, w_if, b_if)


def _bucket_np(rel):
    half = N_BUCKETS // 2
    max_exact = half // 2
    n = np.abs(rel)
    thresholds = np.ceil(max_exact * 2.0 ** (np.arange(1, half - max_exact) / 2.0)).astype(np.int64)
    large = max_exact + (n[..., None] >= thresholds).sum(-1)
    return np.where(rel > 0, half, 0) + np.where(n < max_exact, n, np.minimum(large, half - 1))


def _lam(lp):
    s1 = jnp.sum(lp[0:1] * lp[1:2], axis=-1, keepdims=True)
    s2 = jnp.sum(lp[2:3] * lp[3:4], axis=-1, keepdims=True)
    return jnp.exp(s1) - jnp.exp(s2)


def _bias_steps(lo, hi):
    rel = np.arange(lo, hi + 1)
    b = _bucket_np(rel)
    starts = np.concatenate([[0], np.nonzero(b[1:] != b[:-1])[0] + 1])
    return [(int(rel[k]), int(b[k])) for k in starts]


def _rel_bias_tile(rb_ref, h, rel, lo, hi, shift, scale):
    steps = _bias_steps(lo, hi)
    v = jnp.full(rel.shape, (rb_ref[steps[0][1], h] - shift) * scale, F32)
    for start, bucket in steps[1:]:
        v = jnp.where(rel >= start, (rb_ref[bucket, h] - shift) * scale, v)
    return v


def _attn_prompt_tile(rb_ref, lamp_ref, qT_ref, k_ref, vT_ref, subg_ref, o_ref,
                      q_s, bias_s, m_s, l_s, acc_s, *, i, qsl, t, far_blocks, lam_init):
    h = pl.program_id(0)
    rows = 64
    chunk_shift = CHUNK.bit_length() - 1

    @pl.when(i == 0)
    def _():
        far = rb_ref[FAR_BUCKET, h]
        for tile, shift in enumerate((t, 0)):

            def fill(r, carry, tile=tile, shift=shift):
                r0 = pl.multiple_of(r * rows, rows)
                kk = lax.broadcasted_iota(jnp.int32, (rows, t), 0) + r0
                qq = lax.broadcasted_iota(jnp.int32, (rows, t), 1)
                v = _rel_bias_tile(rb_ref, h, kk - qq - shift, -shift - (t - 1), -shift + (t - 1), far, LOG2E)
                if tile == 1:
                    visible = jnp.right_shift(kk, chunk_shift) <= jnp.right_shift(qq, chunk_shift)
                    v = jnp.where(visible, v, NEG_INF)
                bias_s[tile, pl.ds(r0, rows), :] = v
                return carry

            lax.fori_loop(0, t // rows, fill, 0)

    lam = _lam(lamp_ref[...]) + lam_init
    qT = qT_ref[:, qsl]
    rowi = lax.broadcasted_iota(jnp.int32, qT.shape, 0)
    zero = jnp.zeros_like(qT)
    q_s[0] = jnp.where(rowi < HD_A, qT, zero)
    q_s[1] = jnp.where(rowi >= HD_A, qT, zero)

    def keys(j0, n):
        off = pl.multiple_of(j0 * t, t)
        return k_ref[pl.ds(off, n), :], vT_ref[:, pl.ds(off, n)]

    def scores(kb, c, tile):
        s = _dot(kb, q_s[c])
        return s if tile is None else s + bias_s[tile]

    def fixed_shift_block(j0, n, tile):
        kb, vb = keys(j0, n)
        for c in range(2):
            e = jnp.exp2(scores(kb, c, tile) - m_s[c])
            l_s[c] += jnp.sum(e, axis=0, keepdims=True)
            acc_s[c] += _dot(vb, e.astype(BF16))

    def running_max_block(j0, tile):
        kb, vb = keys(j0, t)
        for c in range(2):
            s = scores(kb, c, tile)
            m_old = m_s[c]
            m_new = jnp.maximum(m_old, jnp.max(s, axis=0, keepdims=True))
            alpha = jnp.exp2(m_old - m_new)
            e = jnp.exp2(s - m_new)
            l_s[c] = alpha * l_s[c] + jnp.sum(e, axis=0, keepdims=True)
            acc_s[c] = alpha * acc_s[c] + _dot(vb, e.astype(BF16))
            m_s[c] = m_new

    def finish():
        l0, l1 = l_s[0], l_s[1]
        oT = acc_s[0] * (1.0 / l0) - acc_s[1] * (lam / l1)
        inv = lax.rsqrt(jnp.mean(oT * oT, axis=0, keepdims=True) + EPS)
        o_ref[qsl, :] = ((oT * inv).T * (subg_ref[...] * (1.0 - lam_init))).astype(BF16)
        big = float(np.finfo(np.float32).max)
        bad = jnp.max(jnp.where(jnp.abs(oT) <= big, 0.0, 1.0), axis=0, keepdims=True)
        return jnp.max(jnp.maximum(bad, jnp.where((l0 <= big) & (l1 <= big), 0.0, 1.0)))

    def first_step(r):
        nb = max(r, 0) + (2 if r >= 0 else 1)
        kb, vb = keys(i - (nb - 1), nb * t)
        for c in range(2):
            s = _dot(kb, q_s[c])
            parts = [s[0:r * t]] if r > 0 else []
            if r >= 0:
                parts.append(s[(nb - 2) * t:(nb - 1) * t] + bias_s[0])
            parts.append(s[(nb - 1) * t:nb * t] + bias_s[1])
            e = [jnp.exp2(p - m_s[c]) for p in parts]
            l_s[c] = functools.reduce(jnp.add, [jnp.sum(p, axis=0, keepdims=True) for p in e])
            e = [p.astype(BF16) for p in e]
            acc_s[c] = _dot(vb, e[0] if len(e) == 1 else jnp.concatenate(e, axis=0))

    k0 = k_ref[pl.ds(pl.multiple_of(i * t, t), CHUNK), :]
    for c in range(2):
        m_s[c] = jnp.max(_dot(k0, q_s[c]) + bias_s[1, 0:CHUNK, :], axis=0, keepdims=True)

    nfar = jnp.maximum(i - 1, 0)
    folded = jnp.bitwise_and(nfar, far_blocks - 1)

    @pl.when(i == 0)
    def _():
        first_step(-1)

    for r in range(far_blocks):
        @pl.when((i >= 1) & (folded == r))
        def _(r=r):
            first_step(r)

    groups = lax.div(nfar, far_blocks)
    odd = jnp.bitwise_and(groups, 1)

    @pl.when(odd == 1)
    def _():
        fixed_shift_block(0, far_blocks * t, None)

    def big_step(p, carry):
        fixed_shift_block((odd + 2 * p) * far_blocks, 2 * far_blocks * t, None)
        return carry

    lax.fori_loop(0, lax.shift_right_logical(groups, 1), big_step, 0)

    @pl.when(finish() > 0.0)
    def _():
        m_s[...] = jnp.full(m_s.shape, NEG_INF, F32)
        l_s[...] = jnp.zeros(l_s.shape, F32)
        acc_s[...] = jnp.zeros(acc_s.shape, F32)

        def step(j, carry):
            running_max_block(j, None)
            return carry

        lax.fori_loop(0, nfar, step, 0)

        @pl.when(i >= 1)
        def _():
            running_max_block(i - 1, 0)

        running_max_block(i, 1)
        finish()


def _attn_prompt_kernel(*refs, t, tiles, far_blocks, lam_init):
    def one_tile(n, carry):
        _attn_prompt_tile(*refs, i=pl.program_id(1) * tiles + n, qsl=pl.ds(pl.multiple_of(n * t, t), t),
                          t=t, far_blocks=far_blocks, lam_init=lam_init)
        return carry

    lax.fori_loop(0, tiles, one_tile, 0)


def _attn_prompt(qT, kbf, vT, rel_bias, lam_p, sub_g, lam_init, t):
    d, s = qT.shape
    assert s % t == 0 and t % 64 == 0 and t % CHUNK == 0 and CHUNK & (CHUNK - 1) == 0
    assert _bucket_np(np.array([-(t + 1)]))[0] == FAR_BUCKET
    assert ATTN_FAR_BLOCKS & (ATTN_FAR_BLOCKS - 1) == 0
    nq = s // t
    tiles = math.gcd(nq, ATTN_TILES_PER_STEP)
    return pl.pallas_call(
        functools.partial(_attn_prompt_kernel, t=t, tiles=tiles, far_blocks=ATTN_FAR_BLOCKS, lam_init=lam_init),
        grid=(H_A, nq // tiles),
        in_specs=[pl.BlockSpec(memory_space=pltpu.SMEM),
                  pl.BlockSpec((4, HD_A), lambda h, i: (0, 0)),
                  pl.BlockSpec((HE_A, tiles * t), lambda h, i: (h, i)),
                  pl.BlockSpec((s, HE_A), lambda h, i: (0, h)),
                  pl.BlockSpec((HE_A, s), lambda h, i: (h, 0)),
                  pl.BlockSpec((1, HE_A), lambda h, i: (0, 0))],
        out_specs=pl.BlockSpec((tiles * t, HE_A), lambda h, i: (i, h)),
        out_shape=jax.ShapeDtypeStruct((s, d), BF16),
        scratch_shapes=[pltpu.VMEM((2, HE_A, t), BF16), pltpu.VMEM((2, t, t), F32),
                        pltpu.VMEM((2, 1, t), F32), pltpu.VMEM((2, 1, t), F32), pltpu.VMEM((2, HE_A, t), F32)],
        compiler_params=_cparams(("arbitrary", "arbitrary")),
        name="attn_prompt",
    )(rel_bias.astype(F32), lam_p, qT, kbf, vT, sub_g)


def _attn_sample_kernel(rb_ref, lamp_ref, q_ref, kc_ref, vc_ref, kn_ref, vn_ref, subg_ref, o_ref,
                        *, past, lam_init):
    t = q_ref.shape[0]
    chunk_shift = CHUNK.bit_length() - 1
    lam = _lam(lamp_ref[...]) + lam_init

    def positions(n_keys, first_key):
        qpos = past + lax.broadcasted_iota(jnp.int32, (t, n_keys), 0)
        kpos = first_key + lax.broadcasted_iota(jnp.int32, (t, n_keys), 1)
        return kpos - qpos, jnp.right_shift(kpos, chunk_shift) <= jnp.right_shift(qpos, chunk_shift)

    rel_c, vis_c = positions(past, 0)
    rel_n, vis_n = positions(t, past)
    for h in range(H_A):
        sl = slice(h * HE_A, (h + 1) * HE_A)
        bias_c = jnp.where(vis_c, _rel_bias_tile(rb_ref, h, rel_c, -(past + t - 1), -1, 0.0, 1.0), NEG_INF)
        bias_n = jnp.where(vis_n, _rel_bias_tile(rb_ref, h, rel_n, -(t - 1), t - 1, 0.0, 1.0), NEG_INF)
        q = q_ref[:, sl]
        lane = lax.broadcasted_iota(jnp.int32, q.shape, 1)
        zero = jnp.zeros_like(q)
        q2 = jnp.concatenate([jnp.where(lane < HD_A, q, zero), jnp.where(lane >= HD_A, q, zero)], axis=0)
        kc = kc_ref[pl.ds(h, past, stride=H_A), :].astype(BF16)
        sc = lax.dot_general(q2, kc, NT_DIMS, preferred_element_type=F32) + jnp.concatenate([bias_c, bias_c], 0)
        sn = (lax.dot_general(q2, kn_ref[:, sl], NT_DIMS, preferred_element_type=F32)
              + jnp.concatenate([bias_n, bias_n], 0))
        m = jnp.maximum(jnp.max(sc, axis=-1, keepdims=True), jnp.max(sn, axis=-1, keepdims=True))
        ec = jnp.exp(sc - m)
        en = jnp.exp(sn - m)
        l = jnp.sum(ec, axis=-1, keepdims=True) + jnp.sum(en, axis=-1, keepdims=True)
        pc = ec / l
        pn = en / l
        ac = pc[0:t] - lam * pc[t:2 * t]
        an = pn[0:t] - lam * pn[t:2 * t]
        vc = vc_ref[pl.ds(h, past, stride=H_A), :].astype(BF16)
        o = _dot(ac.astype(BF16), vc) + _dot(an.astype(BF16), vn_ref[:, sl])
        o_ref[:, sl] = (_rms(o, subg_ref[...]) * (1.0 - lam_init)).astype(BF16)


def _attn_sample(q, kbf, vbf, cache_k, cache_v, layer, rel_bias, lam_p, sub_g, lam_init, b, t):
    r, d = q.shape
    past = cache_k.shape[2]
    rows = pl.BlockSpec((t, d), lambda i: (i, 0))
    cache_k = cache_k.reshape(-1, HE_A)
    cache_v = cache_v.reshape(-1, HE_A)
    cache = pl.BlockSpec((past * H_A, HE_A), lambda i: (layer * b + i, 0))
    return pl.pallas_call(
        functools.partial(_attn_sample_kernel, past=past, lam_init=lam_init),
        grid=(b,),
        in_specs=[pl.BlockSpec(memory_space=pltpu.SMEM),
                  pl.BlockSpec((4, HD_A), lambda i: (0, 0)),
                  rows, cache, cache, rows, rows,
                  pl.BlockSpec((1, HE_A), lambda i: (0, 0))],
        out_specs=rows,
        out_shape=jax.ShapeDtypeStruct((r, d), BF16),
        compiler_params=_cparams(("arbitrary",)),
        name="attn_sample",
    )(rel_bias.astype(F32), lam_p, q, cache_k, cache_v, kbf, vbf, sub_g)


def _mlstm_kernel(q_ref, k_ref, v_ref, om_ref, g_ref, gT_ref, mg_ref, c0_ref, n0_ref, m0_ref, *refs,
                  tl, layer, creates):
    h_ref, c_ref, n_ref, m_ref, c_s, n_s, m_s = refs[-7:]
    step = pl.program_id(1)

    @pl.when(step == 0)
    def _():
        c_s[...] = c0_ref[...]
        n_s[...] = n0_ref[...]
        m_s[...] = m0_ref[...]

    ti = lax.broadcasted_iota(jnp.int32, (tl, tl), 0)
    si = lax.broadcasted_iota(jnp.int32, (tl, tl), 1)
    causal = si <= ti
    gates_c = g_ref[...]
    gates_r = gT_ref[...]
    lf_cols = _log_sigmoid(gates_c)
    lf_rows = _log_sigmoid(gates_r)
    for h in range(H_M):
        sl = slice(h * DH_M, (h + 1) * DH_M)
        q = q_ref[:, sl]
        k = k_ref[:, sl]
        v = v_ref[:, sl]
        ig_c = gates_c[:, h:h + 1]
        lf_c = lf_cols[:, H_M + h:H_M + h + 1]
        ig_r = gates_r[h:h + 1, :]
        lf_r = lf_rows[H_M + h:H_M + h + 1, :]
        m = m_s[h]
        n_row = n_s[h]
        b_c = jnp.sum(jnp.where(causal, lf_r, 0.0), axis=1, keepdims=True)
        b_r = jnp.sum(jnp.where(ti <= si, lf_c, 0.0), axis=0, keepdims=True)
        dmat = jnp.where(causal, b_c - b_r + ig_r, -jnp.inf)
        inter = b_c + m
        m_t = jnp.maximum(inter, jnp.max(dmat, axis=1, keepdims=True))
        w_intra = jnp.exp(dmat - m_t)
        w_inter = jnp.exp(inter - m_t)
        a = w_intra * lax.dot_general(q, k, NT_DIMS, preferred_element_type=F32)
        num = w_inter * _dot(q, c_s[h].astype(BF16)) + _dot(a.astype(BF16), v)
        den = (w_inter * jnp.sum(q.astype(F32) * n_row, axis=1, keepdims=True)
               + jnp.sum(a, axis=1, keepdims=True))
        hh = num / jnp.maximum(jnp.abs(den), jnp.exp(-m_t))
        b_last = b_r[:, tl - 1:tl]
        g_c = b_last - b_c + ig_c
        m_new = jnp.maximum(b_last + m, jnp.max(g_c, axis=0, keepdims=True))
        decay = jnp.exp(b_last + m - m_new)
        kw = jnp.exp(g_c - m_new) * k.astype(F32)
        c_s[h] = decay * c_s[h] + _dot(kw.T.astype(BF16), v)
        n_s[h] = decay * n_row + jnp.sum(kw, axis=0, keepdims=True)
        m_s[h] = m_new
        hn = _rms(hh, mg_ref[:, sl])
        h_ref[:, sl] = (hn * jax.nn.sigmoid(om_ref[:, sl].astype(F32))).astype(BF16)

    @pl.when(step == pl.num_programs(1) - 1)
    def _():
        if creates:
            for l in range(c_ref.shape[0]):
                c_ref[l] = c_s[...] if l == layer else jnp.zeros(c_s.shape, F32)
        else:
            c_ref[...] = c_s[...]
        n_ref[...] = n_s[...]
        m_ref[...] = m_s[...]


def _mlstm(qm, km, vm, om, gates, m_g, c0, c0_layer, n0, m0, b, t, tl, layer, depth, c_states):
    r, d = qm.shape
    nc = t // tl
    gates_t = jnp.transpose(gates.reshape(b * nc, tl, 2 * H_M), (0, 2, 1))
    rows = pl.BlockSpec((tl, d), lambda i, c: (i * nc + c, 0))
    st_n = pl.BlockSpec((None, H_M, 1, DH_M), lambda i, c: (i, 0, 0, 0))
    st_m = pl.BlockSpec((None, H_M, 1, 1), lambda i, c: (i, 0, 0, 0))
    prev = () if c_states is None else (c_states,)
    c_out = (pl.BlockSpec((None, None, H_M, DH_M, DH_M), lambda i, c: (layer, i, 0, 0, 0)) if prev else
             pl.BlockSpec((depth, None, H_M, DH_M, DH_M), lambda i, c: (0, i, 0, 0, 0)))
    h, c_states, n_new, m_new = pl.pallas_call(
        functools.partial(_mlstm_kernel, tl=tl, layer=layer, creates=not prev),
        grid=(b, nc),
        in_specs=[rows, rows, rows, rows,
                  pl.BlockSpec((tl, 2 * H_M), lambda i, c: (i * nc + c, 0)),
                  pl.BlockSpec((None, 2 * H_M, tl), lambda i, c: (i * nc + c, 0, 0)),
                  pl.BlockSpec((1, d), lambda i, c: (0, 0)),
                  pl.BlockSpec((None, None, H_M, DH_M, DH_M), lambda i, c: (c0_layer, i, 0, 0, 0)),
                  st_n, st_m] + [pl.BlockSpec(memory_space=pl.ANY)] * len(prev),
        out_specs=[rows, c_out, st_n, st_m],
        out_shape=[jax.ShapeDtypeStruct((r, d), BF16),
                   jax.ShapeDtypeStruct((depth, b, H_M, DH_M, DH_M), F32),
                   jax.ShapeDtypeStruct((b, H_M, 1, DH_M), F32),
                   jax.ShapeDtypeStruct((b, H_M, 1, 1), F32)],
        input_output_aliases={10: 1} if prev else {},
        scratch_shapes=[pltpu.VMEM((H_M, DH_M, DH_M), F32), pltpu.VMEM((H_M, 1, DH_M), F32),
                        pltpu.VMEM((H_M, 1, 1), F32)],
        compiler_params=_cparams(("arbitrary", "arbitrary")),
        name="mlstm",
    )(qm, km, vm, om, gates, gates_t, m_g, c0, n0.reshape(b, H_M, 1, DH_M), m0.reshape(b, H_M, 1, 1), *prev)
    return h, c_states, n_new.reshape(b, H_M, DH_M), m_new.reshape(b, H_M)


def _merge_kernel(oa_ref, hm_ref, gts_ref, x_ref, g1_ref, wa_ref, wm_ref, wo_ref, o_ref, *, d):
    ya = _dot(oa_ref[...], wa_ref[...])
    ym = _dot(hm_ref[...], wm_ref[...])
    mix = (jax.nn.sigmoid(gts_ref[:, 0:d].astype(F32)) * ya
           + jax.nn.sigmoid(gts_ref[:, d:2 * d].astype(F32)) * ym)
    o_ref[...] = x_ref[...] + g1_ref[...] * _dot(mix.astype(BF16), wo_ref[...])


def _merge(oa, hm, gts, x, mod, per_row, wa, wm, wo, layer):
    r, d = x.shape
    tm = min(r, ROW_TILE)
    nat = pl.BlockSpec((tm, d), lambda i: (i, 0))
    wspec = pl.BlockSpec((None, d, d), lambda i: (layer, 0, 0))
    return pl.pallas_call(
        functools.partial(_merge_kernel, d=d),
        grid=(r // tm,),
        in_specs=[nat, nat, pl.BlockSpec((tm, 2 * d), lambda i: (i, 0)), nat,
                  _mod_spec(per_row, tm, d, 2, lambda i: i), wspec, wspec, wspec],
        out_specs=nat,
        out_shape=jax.ShapeDtypeStruct((r, d), F32),
        compiler_params=_cparams(("arbitrary",)),
        name="merge_out",
    )(oa, hm, gts, x, mod, wa, wm, wo)


def _ffn_kernel(x_ref, g_ref, sc_ref, sh_ref, g2_ref, wu_ref, cw_ref, cb_ref, wd_ref, buf_ref, fg_ref,
                o_ref, nb_ref, tail_s, *, tm, dff, nc, final):
    first = pl.program_id(1) == 0
    ck = dff // nc
    x = x_ref[...]
    xn = (_rms(x, g_ref[...]) * (1.0 + sc_ref[...]) + sh_ref[...]).astype(BF16)
    rowi = lax.broadcasted_iota(jnp.int32, (tm, ck), 0)
    acc = None
    for c in range(nc):
        cs = slice(c * ck, (c + 1) * ck)
        u = _dot(xn, wu_ref[:, cs])
        gate = _dot(xn, wu_ref[:, dff + c * ck:dff + (c + 1) * ck])
        prev2 = jnp.where(first, buf_ref[0:1, cs], tail_s[6:7, cs])
        prev1 = jnp.where(first, buf_ref[1:2, cs], tail_s[7:8, cs])
        u1 = jnp.where(rowi == 0, prev1, pltpu.roll(u, 1, 0))
        u2 = jnp.where(rowi == 0, prev2, jnp.where(rowi == 1, prev1, pltpu.roll(u, 2, 0)))
        conv = cb_ref[:, cs] + cw_ref[0:1, cs] * u2 + cw_ref[1:2, cs] * u1 + cw_ref[2:3, cs] * u
        part = _dot((jax.nn.gelu(conv) * gate).astype(BF16), wd_ref[cs, :])
        acc = part if acc is None else acc + part
        tail_s[:, cs] = u[tm - 8:tm, :]
        nb_ref[:, cs] = u[tm - (CONV_W - 1):tm, :]
    y = x + g2_ref[...] * acc
    if final:
        y = _rms(y, fg_ref[...])
    o_ref[...] = y


def _ffn(x, mod, per_row, g, w_up, conv_w, conv_b, w_down, layer, conv_buf, final_g, b, t, final):
    r, d = x.shape
    dff = w_down.shape[1]
    tm = min(t, ROW_TILE)
    nt = t // tm
    nc = 2
    assert dff % (nc * 128) == 0 and tm >= 8
    row = lambda i, j: i * nt + j
    nat = pl.BlockSpec((tm, d), lambda i, j: (i * nt + j, 0))
    whole = lambda a: pl.BlockSpec(a.shape, lambda i, j: (0,) * a.ndim)
    at_layer = lambda a: pl.BlockSpec((None,) + a.shape[1:], lambda i, j: (layer, 0, 0))
    y, tails = pl.pallas_call(
        functools.partial(_ffn_kernel, tm=tm, dff=dff, nc=nc, final=final),
        grid=(b, nt),
        in_specs=[nat, whole(g), _mod_spec(per_row, tm, d, 4, row), _mod_spec(per_row, tm, d, 3, row),
                  _mod_spec(per_row, tm, d, 5, row), at_layer(w_up), whole(conv_w), whole(conv_b), at_layer(w_down),
                  pl.BlockSpec((None, CONV_W - 1, dff), lambda i, j: (i, 0, 0)), whole(final_g)],
        out_specs=[nat, pl.BlockSpec((None, None, CONV_W - 1, dff), lambda i, j: (i, j, 0, 0))],
        out_shape=[jax.ShapeDtypeStruct((r, d), F32), jax.ShapeDtypeStruct((b, nt, CONV_W - 1, dff), F32)],
        scratch_shapes=[pltpu.VMEM((8, dff), F32)],
        compiler_params=_cparams(("arbitrary", "arbitrary")),
        name="conv_ffn",
    )(x, g, mod, mod, mod, w_up, conv_w, conv_b, w_down, conv_buf, final_g)
    return y, tails[:, nt - 1]


def kernel(x_prompt, x_sample, c_prompt, c_sample, cache_k, cache_v, state_C, state_n, state_m, state_conv,
           rel_bias, ada_w, ada_b, norm1_g, norm2_g, w_in, b_if, lam_p, attn_sub_g, mlstm_g, w_br_a, w_br_m,
           w_out, w_up, conv_w, conv_b, w_down, final_g):
    bp, s, d = x_prompt.shape
    bs, ts, _ = x_sample.shape
    depth = w_in.shape[0]
    dff = w_down.shape[1]
    assert bp == 1 and d == H_A * HE_A == H_M * DH_M

    c_all = jnp.concatenate([c_prompt, c_sample, jnp.zeros((16 - bp - bs, d), F32)], axis=0)
    mod = _ada(c_all, ada_w, ada_b)

    xp = x_prompt.reshape(bp * s, d)
    xs = x_sample.reshape(bs * ts, d)
    tl_p = min(MLSTM_TILE, s)
    tl_s = min(CHUNK, ts)
    zero_c = jnp.zeros((1, bp, H_M, DH_M, DH_M), F32)
    zero_n = jnp.zeros((bp, H_M, DH_M), F32)
    zero_m = jnp.zeros((bp, H_M), F32)
    zero_buf = jnp.zeros((bp, CONV_W - 1, dff), F32)
    w_in_bf = w_in.astype(BF16)
    wa, wm, wo = w_br_a.astype(BF16), w_br_m.astype(BF16), w_out.astype(BF16)
    wu, wd = w_up.astype(BF16), w_down.astype(BF16)
    assert (w_in.shape[2] - 2 * H_M) == 9 * d
    st_p, st_s = [], []
    kv_p = kv_s = c_p = c_s = None
    for l in range(depth):
        lam_init = 0.8 - 0.6 * math.exp(-0.3 * l)
        last = l == depth - 1
        w_if = jnp.pad(w_in[l, :, 9 * d:], ((0, 0), (0, 128 - 2 * H_M))).astype(BF16)
        bif = jnp.pad(b_if[l], (0, 128 - 2 * H_M)).reshape(1, 128)
        g1, g2 = norm1_g[l].reshape(1, d), norm2_g[l].reshape(1, d)
        sub_g, m_g = attn_sub_g[l].reshape(1, HE_A), mlstm_g[l].reshape(1, d)
        cb = conv_b[l].reshape(1, dff)
        fg = final_g.reshape(1, d)

        mod_p = mod[l, 0:bp]
        qT, k_all, kbf, v_all, vT = _in_attn(xp, mod_p, False, g1, w_in_bf, True, l, depth, kv_p)
        kv_p = (k_all, v_all)
        qm, km, vm, om, gts, gates = _in_mlstm(xp, mod_p, False, g1, w_in_bf, l, w_if, bif)
        oa = _attn_prompt(qT, kbf, vT, rel_bias, lam_p[l], sub_g, lam_init, min(ATTN_TILE, s))
        hm, c_p, n_new, m_new = _mlstm(qm, km, vm, om, gates, m_g, zero_c, 0, zero_n, zero_m, bp, s, tl_p,
                                       l, depth, c_p)
        xp = _merge(oa, hm, gts, xp, mod_p, False, wa, wm, wo, l)
        xp, buf_new = _ffn(xp, mod_p, False, g2, wu, conv_w[l], cb, wd, l, zero_buf, fg, bp, s, last)
        st_p.append((n_new, m_new, buf_new))

        mod_s = jnp.repeat(mod[l, bp:bp + bs], ts, axis=0)
        q, k_all, kbf, v_all, vbf = _in_attn(xs, mod_s, True, g1, w_in_bf, False, l, depth, kv_s)
        kv_s = (k_all, v_all)
        qm, km, vm, om, gts, gates = _in_mlstm(xs, mod_s, True, g1, w_in_bf, l, w_if, bif)
        oa = _attn_sample(q, kbf, vbf, cache_k, cache_v, l, rel_bias, lam_p[l], sub_g, lam_init, bs, ts)
        hm, c_s, n_new, m_new = _mlstm(qm, km, vm, om, gates, m_g, state_C, l, state_n[l], state_m[l], bs, ts, tl_s,
                                       l, depth, c_s)
        xs = _merge(oa, hm, gts, xs, mod_s, True, wa, wm, wo, l)
        xs, buf_new = _ffn(xs, mod_s, True, g2, wu, conv_w[l], cb, wd, l, state_conv[l], fg, bs, ts, last)
        st_s.append((n_new, m_new, buf_new))

    outs_p = [a.reshape(depth, bp, s, H_A, HE_A) for a in kv_p] + [c_p] + [jnp.stack(a) for a in zip(*st_p)]
    outs_s = [a.reshape(depth, bs, ts, H_A, HE_A) for a in kv_s] + [c_s] + [jnp.stack(a) for a in zip(*st_s)]
    return (xp.reshape(bp, s, d), xs.reshape(bs, ts, d), *outs_p, *outs_s)
```

```python
import functools
import math

import numpy as np
import jax
import jax.numpy as jnp
from jax import lax
from jax.experimental import pallas as pl
from jax.experimental.pallas import tpu as pltpu

F32 = jnp.float32
BF16 = jnp.bfloat16

CHUNK = 64
H_A = 8
HD_A = 64
HE_A = 2 * HD_A
H_M = 4
DH_M = 256
N_BUCKETS = 32
CONV_W = 3
NEG_INF = -1e30
EPS = 1e-6
FAR_BUCKET = N_BUCKETS // 2 - 1
LOG2E = math.log2(math.e)

V7X_VMEM_LIMIT = 56 * 1024 * 1024
ROW_TILE = 512
ATTN_TILE = 512
ATTN_FAR_BLOCKS = 4
ATTN_TILES_PER_STEP = 8
MLSTM_TILE = 256

NT_DIMS = (((1,), (1,)), ((), ()))


def _cparams(sem, flags=None):
    return pltpu.CompilerParams(dimension_semantics=sem, vmem_limit_bytes=V7X_VMEM_LIMIT, flags=flags)


def _dot(a, b):
    return jnp.dot(a, b, preferred_element_type=F32)


def _rms(x, g):
    return x * lax.rsqrt(jnp.mean(x * x, axis=-1, keepdims=True) + EPS) * g


def _log_sigmoid(x):
    return jnp.minimum(x, 0.0) - jnp.log1p(jnp.exp(-jnp.abs(x)))


def _mod_spec(per_row, tm, d, col, row_index):
    if per_row:
        return pl.BlockSpec((tm, d), lambda *g: (row_index(*g), col))
    return pl.BlockSpec((1, d), lambda *g: (0, col))


def _ada_kernel(c_ref, w_ref, b_ref, o_ref):
    a = jax.nn.silu(c_ref[...]).astype(BF16)
    o_ref[...] = _dot(a, w_ref[...].astype(BF16)) + b_ref[...]


def _ada(c_all, ada_w, ada_b):
    depth, d, n = ada_w.shape
    rc = c_all.shape[0]
    tn = 1536
    return pl.pallas_call(
        _ada_kernel,
        grid=(depth, n // tn),
        in_specs=[pl.BlockSpec((rc, d), lambda l, j: (0, 0)),
                  pl.BlockSpec((None, d, tn), lambda l, j: (l, 0, j)),
                  pl.BlockSpec((None, 1, tn), lambda l, j: (l, 0, j))],
        out_specs=pl.BlockSpec((None, rc, tn), lambda l, j: (l, 0, j)),
        out_shape=jax.ShapeDtypeStruct((depth, rc, n), F32),
        compiler_params=_cparams(("arbitrary", "arbitrary")),
        name="ada_mod",
    )(c_all, ada_w, ada_b.reshape(depth, 1, n))


def _in_attn_kernel(x_ref, g_ref, sc_ref, sh_ref, w_ref, *refs, d, transposed, layer, creates):
    q_ref, k32_ref, kbf_ref, v32_ref, vbf_ref = refs[-5:]
    xn = (_rms(x_ref[...], g_ref[...]) * (1.0 + sc_ref[...]) + sh_ref[...]).astype(BF16)
    q = _dot(xn, w_ref[:, 0:d]) * (HD_A ** -0.5 * (LOG2E if transposed else 1.0))
    k = _dot(xn, w_ref[:, d:2 * d])
    v = _dot(xn, w_ref[:, 2 * d:3 * d])
    if creates:
        for l in range(k32_ref.shape[0]):
            k32_ref[l] = k if l == layer else jnp.zeros_like(k)
            v32_ref[l] = v if l == layer else jnp.zeros_like(v)
    else:
        k32_ref[...] = k
        v32_ref[...] = v
    kbf_ref[...] = k.astype(BF16)
    if transposed:
        q_ref[...] = q.T.astype(BF16)
        vbf_ref[...] = v.T.astype(BF16)
    else:
        q_ref[...] = q.astype(BF16)
        vbf_ref[...] = v.astype(BF16)


def _in_attn(x, mod, per_row, g, w_in, transposed, layer, depth, kv_states):
    r, d = x.shape
    tm = min(r, ROW_TILE)
    row = lambda i: i
    nat = pl.BlockSpec((tm, d), lambda i: (i, 0))
    tr = pl.BlockSpec((d, tm), lambda i: (0, i))
    prev = () if kv_states is None else tuple(kv_states)
    state = (pl.BlockSpec((None, tm, d), lambda i: (layer, i, 0)) if prev else
             pl.BlockSpec((depth, tm, d), lambda i: (0, i, 0)))
    state_shape = jax.ShapeDtypeStruct((depth, r, d), F32)
    bf_shape = jax.ShapeDtypeStruct((d, r) if transposed else (r, d), BF16)
    return pl.pallas_call(
        functools.partial(_in_attn_kernel, d=d, transposed=transposed, layer=layer, creates=not prev),
        grid=(r // tm,),
        in_specs=[nat, pl.BlockSpec((1, d), lambda i: (0, 0)),
                  _mod_spec(per_row, tm, d, 1, row), _mod_spec(per_row, tm, d, 0, row),
                  pl.BlockSpec((None, d, 3 * d), lambda i: (layer, 0, 0))]
        + [pl.BlockSpec(memory_space=pl.ANY)] * len(prev),
        out_specs=[tr if transposed else nat, state, nat, state, tr if transposed else nat],
        out_shape=[bf_shape, state_shape, jax.ShapeDtypeStruct((r, d), BF16), state_shape, bf_shape],
        input_output_aliases={5: 1, 6: 3} if prev else {},
        compiler_params=_cparams(("arbitrary",)),
        name="in_attn",
    )(x, g, mod, mod, w_in, *prev)


def _in_mlstm_kernel(x_ref, g_ref, sc_ref, sh_ref, w1_ref, w2_ref, wif_ref, bif_ref,
                     qm_ref, km_ref, vm_ref, om_ref, gts_ref, gate_ref, *, d):
    xn = (_rms(x_ref[...], g_ref[...]) * (1.0 + sc_ref[...]) + sh_ref[...]).astype(BF16)
    qm_ref[...] = _dot(xn, w1_ref[:, 0:d]).astype(BF16)
    km_ref[...] = (_dot(xn, w1_ref[:, d:2 * d]) * (DH_M ** -0.5)).astype(BF16)
    vm_ref[...] = _dot(xn, w1_ref[:, 2 * d:3 * d]).astype(BF16)
    om_ref[...] = _dot(xn, w2_ref[:, 0:d]).astype(BF16)
    gts_ref[...] = _dot(xn, w2_ref[:, d:3 * d]).astype(BF16)
    pre = _dot(xn, wif_ref[...]) + bif_ref[...]
    gate_ref[...] = pre[:, 0:2 * H_M]


def _in_mlstm(x, mod, per_row, g, w_in, layer, w_if, b_if):
    r, d = x.shape
    tm = min(r, ROW_TILE)
    row = lambda i: i
    nat = pl.BlockSpec((tm, d), lambda i: (i, 0))
    const = lambda shape: pl.BlockSpec(shape, lambda i: (0, 0))
    return pl.pallas_call(
        functools.partial(_in_mlstm_kernel, d=d),
        grid=(r // tm,),
        in_specs=[nat, const((1, d)), _mod_spec(per_row, tm, d, 1, row), _mod_spec(per_row, tm, d, 0, row),
                  pl.BlockSpec((None, d, 3 * d), lambda i: (layer, 0, 1)),
                  pl.BlockSpec((None, d, 3 * d), lambda i: (layer, 0, 2)),
                  const(w_if.shape), const(b_if.shape)],
        out_specs=[nat, nat, nat, nat, pl.BlockSpec((tm, 2 * d), lambda i: (i, 0)),
                   pl.BlockSpec((tm, 2 * H_M), lambda i: (i, 0))],
        out_shape=[jax.ShapeDtypeStruct((r, d), BF16)] * 4
        + [jax.ShapeDtypeStruct((r, 2 * d), BF16), jax.ShapeDtypeStruct((r, 2 * H_M), F32)],
        compiler_params=_cparams(("arbitrary",)),
        name="in_mlstm",
    )(x, g, mod, mod, w_in, w_in, w_if, b_if)


def _bucket_np(rel):
    half = N_BUCKETS // 2
    max_exact = half // 2
    n = np.abs(rel)
    thresholds = np.ceil(max_exact * 2.0 ** (np.arange(1, half - max_exact) / 2.0)).astype(np.int64)
    large = max_exact + (n[..., None] >= thresholds).sum(-1)
    return np.where(rel > 0, half, 0) + np.where(n < max_exact, n, np.minimum(large, half - 1))


def _lam(lp):
    s1 = jnp.sum(lp[0:1] * lp[1:2], axis=-1, keepdims=True)
    s2 = jnp.sum(lp[2:3] * lp[3:4], axis=-1, keepdims=True)
    return jnp.exp(s1) - jnp.exp(s2)


def _bias_steps(lo, hi):
    rel = np.arange(lo, hi + 1)
    b = _bucket_np(rel)
    starts = np.concatenate([[0], np.nonzero(b[1:] != b[:-1])[0] + 1])
    return [(int(rel[k]), int(b[k])) for k in starts]


def _rel_bias_tile(rb_ref, h, rel, lo, hi, shift, scale):
    steps = _bias_steps(lo, hi)
    v = jnp.full(rel.shape, (rb_ref[steps[0][1], h] - shift) * scale, F32)
    for start, bucket in steps[1:]:
        v = jnp.where(rel >= start, (rb_ref[bucket, h] - shift) * scale, v)
    return v


def _attn_prompt_tile(rb_ref, lamp_ref, qT_ref, k_ref, vT_ref, subg_ref, o_ref,
                      q_s, bias_s, m_s, l_s, acc_s, *, i, qsl, t, far_blocks, lam_init):
    h = pl.program_id(0)
    rows = 64
    chunk_shift = CHUNK.bit_length() - 1

    @pl.when(i == 0)
    def _():
        far = rb_ref[FAR_BUCKET, h]
        for tile, shift in enumerate((t, 0)):

            def fill(r, carry, tile=tile, shift=shift):
                r0 = pl.multiple_of(r * rows, rows)
                kk = lax.broadcasted_iota(jnp.int32, (rows, t), 0) + r0
                qq = lax.broadcasted_iota(jnp.int32, (rows, t), 1)
                v = _rel_bias_tile(rb_ref, h, kk - qq - shift, -shift - (t - 1), -shift + (t - 1), far, LOG2E)
                if tile == 1:
                    visible = jnp.right_shift(kk, chunk_shift) <= jnp.right_shift(qq, chunk_shift)
                    v = jnp.where(visible, v, NEG_INF)
                bias_s[tile, pl.ds(r0, rows), :] = v
                return carry

            lax.fori_loop(0, t // rows, fill, 0)

    lam = _lam(lamp_ref[...]) + lam_init
    qT = qT_ref[:, qsl]
    rowi = lax.broadcasted_iota(jnp.int32, qT.shape, 0)
    zero = jnp.zeros_like(qT)
    q_s[0] = jnp.where(rowi < HD_A, qT, zero)
    q_s[1] = jnp.where(rowi >= HD_A, qT, zero)

    def keys(j0, n):
        off = pl.multiple_of(j0 * t, t)
        return k_ref[pl.ds(off, n), :], vT_ref[:, pl.ds(off, n)]

    def scores(kb, c, tile):
        s = _dot(kb, q_s[c])
        return s if tile is None else s + bias_s[tile]

    def fixed_shift_block(j0, n, tile):
        kb, vb = keys(j0, n)
        for c in range(2):
            e = jnp.exp2(scores(kb, c, tile) - m_s[c])
            l_s[c] += jnp.sum(e, axis=0, keepdims=True)
            acc_s[c] += _dot(vb, e.astype(BF16))

    def running_max_block(j0, tile):
        kb, vb = keys(j0, t)
        for c in range(2):
            s = scores(kb, c, tile)
            m_old = m_s[c]
            m_new = jnp.maximum(m_old, jnp.max(s, axis=0, keepdims=True))
            alpha = jnp.exp2(m_old - m_new)
            e = jnp.exp2(s - m_new)
            l_s[c] = alpha * l_s[c] + jnp.sum(e, axis=0, keepdims=True)
            acc_s[c] = alpha * acc_s[c] + _dot(vb, e.astype(BF16))
            m_s[c] = m_new

    def finish():
        l0, l1 = l_s[0], l_s[1]
        oT = acc_s[0] * (1.0 / l0) - acc_s[1] * (lam / l1)
        inv = lax.rsqrt(jnp.mean(oT * oT, axis=0, keepdims=True) + EPS)
        o_ref[:, qsl] = (oT * inv * (subg_ref[...] * (1.0 - lam_init))).astype(BF16)
        big = float(np.finfo(np.float32).max)
        bad = jnp.max(jnp.where(jnp.abs(oT) <= big, 0.0, 1.0), axis=0, keepdims=True)
        return jnp.max(jnp.maximum(bad, jnp.where((l0 <= big) & (l1 <= big), 0.0, 1.0)))

    def first_step(r):
        nb = max(r, 0) + (2 if r >= 0 else 1)
        kb, vb = keys(i - (nb - 1), nb * t)
        for c in range(2):
            s = _dot(kb, q_s[c])
            parts = [s[0:r * t]] if r > 0 else []
            if r >= 0:
                parts.append(s[(nb - 2) * t:(nb - 1) * t] + bias_s[0])
            parts.append(s[(nb - 1) * t:nb * t] + bias_s[1])
            e = [jnp.exp2(p - m_s[c]) for p in parts]
            l_s[c] = functools.reduce(jnp.add, [jnp.sum(p, axis=0, keepdims=True) for p in e])
            e = [p.astype(BF16) for p in e]
            acc_s[c] = _dot(vb, e[0] if len(e) == 1 else jnp.concatenate(e, axis=0))

    k0 = k_ref[pl.ds(pl.multiple_of(i * t, t), CHUNK), :]
    for c in range(2):
        m_s[c] = jnp.max(_dot(k0, q_s[c]) + bias_s[1, 0:CHUNK, :], axis=0, keepdims=True)

    nfar = jnp.maximum(i - 1, 0)
    folded = jnp.bitwise_and(nfar, far_blocks - 1)

    @pl.when(i == 0)
    def _():
        first_step(-1)

    for r in range(far_blocks):
        @pl.when((i >= 1) & (folded == r))
        def _(r=r):
            first_step(r)

    groups = lax.div(nfar, far_blocks)
    odd = jnp.bitwise_and(groups, 1)

    @pl.when(odd == 1)
    def _():
        fixed_shift_block(0, far_blocks * t, None)

    def big_step(p, carry):
        fixed_shift_block((odd + 2 * p) * far_blocks, 2 * far_blocks * t, None)
        return carry

    lax.fori_loop(0, lax.shift_right_logical(groups, 1), big_step, 0)

    @pl.when(finish() > 0.0)
    def _():
        m_s[...] = jnp.full(m_s.shape, NEG_INF, F32)
        l_s[...] = jnp.zeros(l_s.shape, F32)
        acc_s[...] = jnp.zeros(acc_s.shape, F32)

        def step(j, carry):
            running_max_block(j, None)
            return carry

        lax.fori_loop(0, nfar, step, 0)

        @pl.when(i >= 1)
        def _():
            running_max_block(i - 1, 0)

        running_max_block(i, 1)
        finish()


def _attn_prompt_kernel(*refs, t, tiles, far_blocks, lam_init):
    def one_tile(n, carry):
        _attn_prompt_tile(*refs, i=pl.program_id(1) * tiles + n, qsl=pl.ds(pl.multiple_of(n * t, t), t),
                          t=t, far_blocks=far_blocks, lam_init=lam_init)
        return carry

    lax.fori_loop(0, tiles, one_tile, 0)


def _attn_prompt(qT, kbf, vT, rel_bias, lam_p, sub_g, lam_init, t):
    d, s = qT.shape
    assert s % t == 0 and t % 64 == 0 and t % CHUNK == 0 and CHUNK & (CHUNK - 1) == 0
    assert _bucket_np(np.array([-(t + 1)]))[0] == FAR_BUCKET
    assert ATTN_FAR_BLOCKS & (ATTN_FAR_BLOCKS - 1) == 0
    nq = s // t
    tiles = math.gcd(nq, ATTN_TILES_PER_STEP)
    return pl.pallas_call(
        functools.partial(_attn_prompt_kernel, t=t, tiles=tiles, far_blocks=ATTN_FAR_BLOCKS, lam_init=lam_init),
        grid=(H_A, nq // tiles),
        in_specs=[pl.BlockSpec(memory_space=pltpu.SMEM),
                  pl.BlockSpec((4, HD_A), lambda h, i: (0, 0)),
                  pl.BlockSpec((HE_A, tiles * t), lambda h, i: (h, i)),
                  pl.BlockSpec((s, HE_A), lambda h, i: (0, h)),
                  pl.BlockSpec((HE_A, s), lambda h, i: (h, 0)),
                  pl.BlockSpec((HE_A, 1), lambda h, i: (0, 0))],
        out_specs=pl.BlockSpec((HE_A, tiles * t), lambda h, i: (h, i)),
        out_shape=jax.ShapeDtypeStruct((d, s), BF16),
        scratch_shapes=[pltpu.VMEM((2, HE_A, t), BF16), pltpu.VMEM((2, t, t), F32),
                        pltpu.VMEM((2, 1, t), F32), pltpu.VMEM((2, 1, t), F32), pltpu.VMEM((2, HE_A, t), F32)],
        compiler_params=_cparams(("arbitrary", "arbitrary")),
        name="attn_prompt",
    )(rel_bias.astype(F32), lam_p, qT, kbf, vT, sub_g.reshape(HE_A, 1))


def _attn_sample_kernel(rb_ref, lamp_ref, q_ref, kc_ref, vc_ref, kn_ref, vn_ref, subg_ref, o_ref,
                        *, past, lam_init):
    t = q_ref.shape[0]
    chunk_shift = CHUNK.bit_length() - 1
    lam = _lam(lamp_ref[...]) + lam_init

    def positions(n_keys, first_key):
        qpos = past + lax.broadcasted_iota(jnp.int32, (t, n_keys), 0)
        kpos = first_key + lax.broadcasted_iota(jnp.int32, (t, n_keys), 1)
        return kpos - qpos, jnp.right_shift(kpos, chunk_shift) <= jnp.right_shift(qpos, chunk_shift)

    rel_c, vis_c = positions(past, 0)
    rel_n, vis_n = positions(t, past)
    for h in range(H_A):
        sl = slice(h * HE_A, (h + 1) * HE_A)
        bias_c = jnp.where(vis_c, _rel_bias_tile(rb_ref, h, rel_c, -(past + t - 1), -1, 0.0, 1.0), NEG_INF)
        bias_n = jnp.where(vis_n, _rel_bias_tile(rb_ref, h, rel_n, -(t - 1), t - 1, 0.0, 1.0), NEG_INF)
        q = q_ref[:, sl]
        lane = lax.broadcasted_iota(jnp.int32, q.shape, 1)
        zero = jnp.zeros_like(q)
        q2 = jnp.concatenate([jnp.where(lane < HD_A, q, zero), jnp.where(lane >= HD_A, q, zero)], axis=0)
        kc = kc_ref[pl.ds(h, past, stride=H_A), :].astype(BF16)
        sc = lax.dot_general(q2, kc, NT_DIMS, preferred_element_type=F32) + jnp.concatenate([bias_c, bias_c], 0)
        sn = (lax.dot_general(q2, kn_ref[:, sl], NT_DIMS, preferred_element_type=F32)
              + jnp.concatenate([bias_n, bias_n], 0))
        m = jnp.maximum(jnp.max(sc, axis=-1, keepdims=True), jnp.max(sn, axis=-1, keepdims=True))
        ec = jnp.exp(sc - m)
        en = jnp.exp(sn - m)
        l = jnp.sum(ec, axis=-1, keepdims=True) + jnp.sum(en, axis=-1, keepdims=True)
        pc = ec / l
        pn = en / l
        ac = pc[0:t] - lam * pc[t:2 * t]
        an = pn[0:t] - lam * pn[t:2 * t]
        vc = vc_ref[pl.ds(h, past, stride=H_A), :].astype(BF16)
        o = _dot(ac.astype(BF16), vc) + _dot(an.astype(BF16), vn_ref[:, sl])
        o_ref[:, sl] = (_rms(o, subg_ref[...]) * (1.0 - lam_init)).astype(BF16)


def _attn_sample(q, kbf, vbf, cache_k, cache_v, layer, rel_bias, lam_p, sub_g, lam_init, b, t):
    r, d = q.shape
    past = cache_k.shape[2]
    rows = pl.BlockSpec((t, d), lambda i: (i, 0))
    cache_k = cache_k.reshape(-1, HE_A)
    cache_v = cache_v.reshape(-1, HE_A)
    cache = pl.BlockSpec((past * H_A, HE_A), lambda i: (layer * b + i, 0))
    return pl.pallas_call(
        functools.partial(_attn_sample_kernel, past=past, lam_init=lam_init),
        grid=(b,),
        in_specs=[pl.BlockSpec(memory_space=pltpu.SMEM),
                  pl.BlockSpec((4, HD_A), lambda i: (0, 0)),
                  rows, cache, cache, rows, rows,
                  pl.BlockSpec((1, HE_A), lambda i: (0, 0))],
        out_specs=rows,
        out_shape=jax.ShapeDtypeStruct((r, d), BF16),
        compiler_params=_cparams(("arbitrary",)),
        name="attn_sample",
    )(rel_bias.astype(F32), lam_p, q, cache_k, cache_v, kbf, vbf, sub_g)


def _mlstm_kernel(q_ref, k_ref, v_ref, om_ref, g_ref, gT_ref, mg_ref, c0_ref, n0_ref, m0_ref, *refs,
                  tl, layer, creates):
    h_ref, c_ref, n_ref, m_ref, c_s, n_s, m_s = refs[-7:]
    step = pl.program_id(1)

    @pl.when(step == 0)
    def _():
        c_s[...] = c0_ref[...]
        n_s[...] = n0_ref[...]
        m_s[...] = m0_ref[...]

    ti = lax.broadcasted_iota(jnp.int32, (tl, tl), 0)
    si = lax.broadcasted_iota(jnp.int32, (tl, tl), 1)
    causal = si <= ti
    gates_c = g_ref[...]
    gates_r = gT_ref[...]
    lf_cols = _log_sigmoid(gates_c)
    lf_rows = _log_sigmoid(gates_r)
    for h in range(H_M):
        sl = slice(h * DH_M, (h + 1) * DH_M)
        q = q_ref[:, sl]
        k = k_ref[:, sl]
        v = v_ref[:, sl]
        ig_c = gates_c[:, h:h + 1]
        lf_c = lf_cols[:, H_M + h:H_M + h + 1]
        ig_r = gates_r[h:h + 1, :]
        lf_r = lf_rows[H_M + h:H_M + h + 1, :]
        m = m_s[h]
        n_row = n_s[h]
        b_c = jnp.sum(jnp.where(causal, lf_r, 0.0), axis=1, keepdims=True)
        b_r = jnp.sum(jnp.where(ti <= si, lf_c, 0.0), axis=0, keepdims=True)
        dmat = jnp.where(causal, b_c - b_r + ig_r, -jnp.inf)
        inter = b_c + m
        m_t = jnp.maximum(inter, jnp.max(dmat, axis=1, keepdims=True))
        w_intra = jnp.exp(dmat - m_t)
        w_inter = jnp.exp(inter - m_t)
        a = w_intra * lax.dot_general(q, k, NT_DIMS, preferred_element_type=F32)
        num = w_inter * _dot(q, c_s[h].astype(BF16)) + _dot(a.astype(BF16), v)
        den = (w_inter * jnp.sum(q.astype(F32) * n_row, axis=1, keepdims=True)
               + jnp.sum(a, axis=1, keepdims=True))
        hh = num / jnp.maximum(jnp.abs(den), jnp.exp(-m_t))
        b_last = b_r[:, tl - 1:tl]
        g_c = b_last - b_c + ig_c
        m_new = jnp.maximum(b_last + m, jnp.max(g_c, axis=0, keepdims=True))
        decay = jnp.exp(b_last + m - m_new)
        kw = jnp.exp(g_c - m_new) * k.astype(F32)
        c_s[h] = decay * c_s[h] + _dot(kw.T.astype(BF16), v)
        n_s[h] = decay * n_row + jnp.sum(kw, axis=0, keepdims=True)
        m_s[h] = m_new
        hn = _rms(hh, mg_ref[:, sl])
        h_ref[:, sl] = (hn * jax.nn.sigmoid(om_ref[:, sl].astype(F32))).astype(BF16)

    @pl.when(step == pl.num_programs(1) - 1)
    def _():
        if creates:
            for l in range(c_ref.shape[0]):
                c_ref[l] = c_s[...] if l == layer else jnp.zeros(c_s.shape, F32)
        else:
            c_ref[...] = c_s[...]
        n_ref[...] = n_s[...]
        m_ref[...] = m_s[...]


def _mlstm(qm, km, vm, om, gates, m_g, c0, c0_layer, n0, m0, b, t, tl, layer, depth, c_states):
    r, d = qm.shape
    nc = t // tl
    gates_t = jnp.transpose(gates.reshape(b * nc, tl, 2 * H_M), (0, 2, 1))
    rows = pl.BlockSpec((tl, d), lambda i, c: (i * nc + c, 0))
    st_n = pl.BlockSpec((None, H_M, 1, DH_M), lambda i, c: (i, 0, 0, 0))
    st_m = pl.BlockSpec((None, H_M, 1, 1), lambda i, c: (i, 0, 0, 0))
    prev = () if c_states is None else (c_states,)
    c_out = (pl.BlockSpec((None, None, H_M, DH_M, DH_M), lambda i, c: (layer, i, 0, 0, 0)) if prev else
             pl.BlockSpec((depth, None, H_M, DH_M, DH_M), lambda i, c: (0, i, 0, 0, 0)))
    h, c_states, n_new, m_new = pl.pallas_call(
        functools.partial(_mlstm_kernel, tl=tl, layer=layer, creates=not prev),
        grid=(b, nc),
        in_specs=[rows, rows, rows, rows,
                  pl.BlockSpec((tl, 2 * H_M), lambda i, c: (i * nc + c, 0)),
                  pl.BlockSpec((None, 2 * H_M, tl), lambda i, c: (i * nc + c, 0, 0)),
                  pl.BlockSpec((1, d), lambda i, c: (0, 0)),
                  pl.BlockSpec((None, None, H_M, DH_M, DH_M), lambda i, c: (c0_layer, i, 0, 0, 0)),
                  st_n, st_m] + [pl.BlockSpec(memory_space=pl.ANY)] * len(prev),
        out_specs=[rows, c_out, st_n, st_m],
        out_shape=[jax.ShapeDtypeStruct((r, d), BF16),
                   jax.ShapeDtypeStruct((depth, b, H_M, DH_M, DH_M), F32),
                   jax.ShapeDtypeStruct((b, H_M, 1, DH_M), F32),
                   jax.ShapeDtypeStruct((b, H_M, 1, 1), F32)],
        input_output_aliases={10: 1} if prev else {},
        scratch_shapes=[pltpu.VMEM((H_M, DH_M, DH_M), F32), pltpu.VMEM((H_M, 1, DH_M), F32),
                        pltpu.VMEM((H_M, 1, 1), F32)],
        compiler_params=_cparams(("arbitrary", "arbitrary")),
        name="mlstm",
    )(qm, km, vm, om, gates, gates_t, m_g, c0, n0.reshape(b, H_M, 1, DH_M), m0.reshape(b, H_M, 1, 1), *prev)
    return h, c_states, n_new.reshape(b, H_M, DH_M), m_new.reshape(b, H_M)


def _merge_kernel(oa_ref, hm_ref, gts_ref, x_ref, g1_ref, wa_ref, wm_ref, wo_ref, o_ref, *, d, oa_transposed):
    if oa_transposed:
        ya = lax.dot_general(oa_ref[...], wa_ref[...], (((0,), (0,)), ((), ())), preferred_element_type=F32)
    else:
        ya = _dot(oa_ref[...], wa_ref[...])
    ym = _dot(hm_ref[...], wm_ref[...])
    mix = (jax.nn.sigmoid(gts_ref[:, 0:d].astype(F32)) * ya
           + jax.nn.sigmoid(gts_ref[:, d:2 * d].astype(F32)) * ym)
    o_ref[...] = x_ref[...] + g1_ref[...] * _dot(mix.astype(BF16), wo_ref[...])


def _merge(oa, hm, gts, x, mod, per_row, wa, wm, wo, layer, oa_transposed):
    r, d = x.shape
    tm = min(r, ROW_TILE)
    nat = pl.BlockSpec((tm, d), lambda i: (i, 0))
    wspec = pl.BlockSpec((None, d, d), lambda i: (layer, 0, 0))
    return pl.pallas_call(
        functools.partial(_merge_kernel, d=d, oa_transposed=oa_transposed),
        grid=(r // tm,),
        in_specs=[pl.BlockSpec((d, tm), lambda i: (0, i)) if oa_transposed else nat, nat, pl.BlockSpec((tm, 2 * d), lambda i: (i, 0)), nat,
                  _mod_spec(per_row, tm, d, 2, lambda i: i), wspec, wspec, wspec],
        out_specs=nat,
        out_shape=jax.ShapeDtypeStruct((r, d), F32),
        compiler_params=_cparams(("arbitrary",)),
        name="merge_out",
    )(oa, hm, gts, x, mod, wa, wm, wo)


def _ffn_kernel(x_ref, g_ref, sc_ref, sh_ref, g2_ref, wu_ref, cw_ref, cb_ref, wd_ref, buf_ref, fg_ref,
                o_ref, nb_ref, tail_s, *, tm, dff, nc, final):
    first = pl.program_id(1) == 0
    ck = dff // nc
    x = x_ref[...]
    xn = (_rms(x, g_ref[...]) * (1.0 + sc_ref[...]) + sh_ref[...]).astype(BF16)
    rowi = lax.broadcasted_iota(jnp.int32, (tm, ck), 0)
    acc = None
    for c in range(nc):
        cs = slice(c * ck, (c + 1) * ck)
        u = _dot(xn, wu_ref[:, cs])
        gate = _dot(xn, wu_ref[:, dff + c * ck:dff + (c + 1) * ck])
        prev2 = jnp.where(first, buf_ref[0:1, cs], tail_s[6:7, cs])
        prev1 = jnp.where(first, buf_ref[1:2, cs], tail_s[7:8, cs])
        u1 = jnp.where(rowi == 0, prev1, pltpu.roll(u, 1, 0))
        u2 = jnp.where(rowi == 0, prev2, jnp.where(rowi == 1, prev1, pltpu.roll(u, 2, 0)))
        conv = cb_ref[:, cs] + cw_ref[0:1, cs] * u2 + cw_ref[1:2, cs] * u1 + cw_ref[2:3, cs] * u
        part = _dot((jax.nn.gelu(conv) * gate).astype(BF16), wd_ref[cs, :])
        acc = part if acc is None else acc + part
        tail_s[:, cs] = u[tm - 8:tm, :]
        nb_ref[:, cs] = u[tm - (CONV_W - 1):tm, :]
    y = x + g2_ref[...] * acc
    if final:
        y = _rms(y, fg_ref[...])
    o_ref[...] = y


def _ffn(x, mod, per_row, g, w_up, conv_w, conv_b, w_down, layer, conv_buf, final_g, b, t, final):
    r, d = x.shape
    dff = w_down.shape[1]
    tm = min(t, ROW_TILE)
    nt = t // tm
    nc = 2
    assert dff % (nc * 128) == 0 and tm >= 8
    row = lambda i, j: i * nt + j
    nat = pl.BlockSpec((tm, d), lambda i, j: (i * nt + j, 0))
    whole = lambda a: pl.BlockSpec(a.shape, lambda i, j: (0,) * a.ndim)
    at_layer = lambda a: pl.BlockSpec((None,) + a.shape[1:], lambda i, j: (layer, 0, 0))
    y, tails = pl.pallas_call(
        functools.partial(_ffn_kernel, tm=tm, dff=dff, nc=nc, final=final),
        grid=(b, nt),
        in_specs=[nat, whole(g), _mod_spec(per_row, tm, d, 4, row), _mod_spec(per_row, tm, d, 3, row),
                  _mod_spec(per_row, tm, d, 5, row), at_layer(w_up), whole(conv_w), whole(conv_b), at_layer(w_down),
                  pl.BlockSpec((None, CONV_W - 1, dff), lambda i, j: (i, 0, 0)), whole(final_g)],
        out_specs=[nat, pl.BlockSpec((None, None, CONV_W - 1, dff), lambda i, j: (i, j, 0, 0))],
        out_shape=[jax.ShapeDtypeStruct((r, d), F32), jax.ShapeDtypeStruct((b, nt, CONV_W - 1, dff), F32)],
        scratch_shapes=[pltpu.VMEM((8, dff), F32)],
        compiler_params=_cparams(("arbitrary", "arbitrary")),
        name="conv_ffn",
    )(x, g, mod, mod, mod, w_up, conv_w, conv_b, w_down, conv_buf, final_g)
    return y, tails[:, nt - 1]


def kernel(x_prompt, x_sample, c_prompt, c_sample, cache_k, cache_v, state_C, state_n, state_m, state_conv,
           rel_bias, ada_w, ada_b, norm1_g, norm2_g, w_in, b_if, lam_p, attn_sub_g, mlstm_g, w_br_a, w_br_m,
           w_out, w_up, conv_w, conv_b, w_down, final_g):
    bp, s, d = x_prompt.shape
    bs, ts, _ = x_sample.shape
    depth = w_in.shape[0]
    dff = w_down.shape[1]
    assert bp == 1 and d == H_A * HE_A == H_M * DH_M

    c_all = jnp.concatenate([c_prompt, c_sample, jnp.zeros((16 - bp - bs, d), F32)], axis=0)
    mod = _ada(c_all, ada_w, ada_b)

    xp = x_prompt.reshape(bp * s, d)
    xs = x_sample.reshape(bs * ts, d)
    tl_p = min(MLSTM_TILE, s)
    tl_s = min(CHUNK, ts)
    zero_c = jnp.zeros((1, bp, H_M, DH_M, DH_M), F32)
    zero_n = jnp.zeros((bp, H_M, DH_M), F32)
    zero_m = jnp.zeros((bp, H_M), F32)
    zero_buf = jnp.zeros((bp, CONV_W - 1, dff), F32)
    w_in_bf = w_in.astype(BF16)
    wa, wm, wo = w_br_a.astype(BF16), w_br_m.astype(BF16), w_out.astype(BF16)
    wu, wd = w_up.astype(BF16), w_down.astype(BF16)
    assert (w_in.shape[2] - 2 * H_M) == 9 * d
    st_p, st_s = [], []
    kv_p = kv_s = c_p = c_s = None
    for l in range(depth):
        lam_init = 0.8 - 0.6 * math.exp(-0.3 * l)
        last = l == depth - 1
        w_if = jnp.pad(w_in[l, :, 9 * d:], ((0, 0), (0, 128 - 2 * H_M))).astype(BF16)
        bif = jnp.pad(b_if[l], (0, 128 - 2 * H_M)).reshape(1, 128)
        g1, g2 = norm1_g[l].reshape(1, d), norm2_g[l].reshape(1, d)
        sub_g, m_g = attn_sub_g[l].reshape(1, HE_A), mlstm_g[l].reshape(1, d)
        cb = conv_b[l].reshape(1, dff)
        fg = final_g.reshape(1, d)

        mod_p = mod[l, 0:bp]
        qT, k_all, kbf, v_all, vT = _in_attn(xp, mod_p, False, g1, w_in_bf, True, l, depth, kv_p)
        kv_p = (k_all, v_all)
        qm, km, vm, om, gts, gates = _in_mlstm(xp, mod_p, False, g1, w_in_bf, l, w_if, bif)
        oa = _attn_prompt(qT, kbf, vT, rel_bias, lam_p[l], sub_g, lam_init, min(ATTN_TILE, s))
        hm, c_p, n_new, m_new = _mlstm(qm, km, vm, om, gates, m_g, zero_c, 0, zero_n, zero_m, bp, s, tl_p,
                                       l, depth, c_p)
        xp = _merge(oa, hm, gts, xp, mod_p, False, wa, wm, wo, l, True)
        xp, buf_new = _ffn(xp, mod_p, False, g2, wu, conv_w[l], cb, wd, l, zero_buf, fg, bp, s, last)
        st_p.append((n_new, m_new, buf_new))

        mod_s = jnp.repeat(mod[l, bp:bp + bs], ts, axis=0)
        q, k_all, kbf, v_all, vbf = _in_attn(xs, mod_s, True, g1, w_in_bf, False, l, depth, kv_s)
        kv_s = (k_all, v_all)
        qm, km, vm, om, gts, gates = _in_mlstm(xs, mod_s, True, g1, w_in_bf, l, w_if, bif)
        oa = _attn_sample(q, kbf, vbf, cache_k, cache_v, l, rel_bias, lam_p[l], sub_g, lam_init, bs, ts)
        hm, c_s, n_new, m_new = _mlstm(qm, km, vm, om, gates, m_g, state_C, l, state_n[l], state_m[l], bs, ts, tl_s,
                                       l, depth, c_s)
        xs = _merge(oa, hm, gts, xs, mod_s, True, wa, wm, wo, l, False)
        xs, buf_new = _ffn(xs, mod_s, True, g2, wu, conv_w[l], cb, wd, l, state_conv[l], fg, bs, ts, last)
        st_s.append((n_new, m_new, buf_new))

    outs_p = [a.reshape(depth, bp, s, H_A, HE_A) for a in kv_p] + [c_p] + [jnp.stack(a) for a in zip(*st_p)]
    outs_s = [a.reshape(depth, bs, ts, H_A, HE_A) for a in kv_s] + [c_s] + [jnp.stack(a) for a in zip(*st_s)]
    return (xp.reshape(bp, s, d), xs.reshape(bs, ts, d), *outs_p, *outs_s)
```
